```python
import jax, jax.numpy as jnp
from jax import lax
import numpy as np

D_MODEL = 1024
BATCH = 8
SEQ = 4096
DEPTH = 1

HEAD_DIM = 64
A_HEADS = D_MODEL // (2 * HEAD_DIM)
A_KV_HEADS = max(1, A_HEADS // 4)
A_WINDOW = 128
B_HEADS = D_MODEL // (2 * HEAD_DIM)
B_PATTERNS = ((128, 1), (512, 4), (2048, 16))
BLOCK = 128
ROPE_THETA = 10000.0
EPS = 1e-6
NEG = -1e30

A_WIDTH = A_HEADS * HEAD_DIM
A_KV_WIDTH = A_KV_HEADS * HEAD_DIM
B_WIDTH = B_HEADS * HEAD_DIM
MIX_WIDTH = A_WIDTH + B_WIDTH
IN_SPLITS = (A_WIDTH, A_KV_WIDTH, A_KV_WIDTH, A_WIDTH, B_WIDTH, B_WIDTH, B_WIDTH, B_WIDTH)
IN_WIDTH = sum(IN_SPLITS)

kernel_name = "hybrid_swa_sink_dilated_gated"


def rmsnorm(t, gain):
    tf = t.astype(jnp.float32)
    tf = tf * lax.rsqrt(jnp.mean(tf * tf, axis=-1, keepdims=True) + EPS)
    return (tf * gain.astype(jnp.float32)).astype(t.dtype)


def rope(t, pos):
    half = HEAD_DIM // 2
    inv = ROPE_THETA ** (-jnp.arange(half, dtype=jnp.float32) / half)
    ang = pos.astype(jnp.float32)[:, None] * inv[None, :]
    cos = jnp.cos(ang)[:, None, :]
    sin = jnp.sin(ang)[:, None, :]
    tf = t.astype(jnp.float32)
    t1, t2 = tf[..., :half], tf[..., half:]
    return jnp.concatenate([t1 * cos - t2 * sin, t2 * cos + t1 * sin], axis=-1).astype(t.dtype)


def banded_attention(q, k, v, max_dist, sinks=None):
    n, L, h, d = q.shape
    hkv = k.shape[2]
    g = h // hkv
    nb = -(-L // BLOCK)
    lp = nb * BLOCK
    pad = lp - L
    q = jnp.pad(q, ((0, 0), (0, pad), (0, 0), (0, 0)))
    k = jnp.pad(k, ((0, 0), (BLOCK, pad), (0, 0), (0, 0)))
    v = jnp.pad(v, ((0, 0), (BLOCK, pad), (0, 0), (0, 0)))
    qb = q.reshape(n, nb, BLOCK, hkv, g, d)
    kb = k.reshape(n, nb + 1, BLOCK, hkv, d)
    vb = v.reshape(n, nb + 1, BLOCK, hkv, d)
    kw = jnp.concatenate([kb[:, :-1], kb[:, 1:]], axis=2)
    vw = jnp.concatenate([vb[:, :-1], vb[:, 1:]], axis=2)
    s = jnp.einsum("nbqkgd,nbskd->nbkgqs", qb, kw,
                   preferred_element_type=jnp.float32) * (d ** -0.5)
    qi = jnp.arange(BLOCK)[:, None]
    sj = jnp.arange(2 * BLOCK)[None, :]
    dist = qi - sj + BLOCK
    key_pos = jnp.arange(nb)[:, None] * BLOCK - BLOCK + sj
    valid = ((dist >= 0) & (dist <= max_dist))[None] & (key_pos >= 0)[:, None, :]
    s = jnp.where(valid[None, :, None, None], s, NEG)
    m = s.max(axis=-1)
    if sinks is not None:
        sk = sinks.astype(jnp.float32).reshape(hkv, g)[None, None, :, :, None]
        m = jnp.maximum(m, sk)
    p = jnp.exp(s - m[..., None])
    l = p.sum(axis=-1)
    if sinks is not None:
        l = l + jnp.exp(sk - m)
    o = jnp.einsum("nbkgqs,nbskd->nbqkgd", p, vw.astype(jnp.float32))
    m = m.transpose(0, 1, 4, 2, 3)
    l = l.transpose(0, 1, 4, 2, 3)
    o = o / l[..., None]
    o = o.reshape(n, lp, h, d)[:, :L]
    m = m.reshape(n, lp, h)[:, :L]
    l = l.reshape(n, lp, h)[:, :L]
    return o, m, l


def dilated_attention(q, k, v):
    b, S, h, d = q.shape
    outs, ms, ls = [], [], []
    for window, dil in B_PATTERNS:
        L = S // dil

        def fold(t):
            return t.reshape(b, L, dil, h, d).transpose(0, 2, 1, 3, 4).reshape(b * dil, L, h, d)

        o, m, l = banded_attention(fold(q), fold(k), fold(v), window // dil)
        outs.append(o.reshape(b, dil, L, h, d).transpose(0, 2, 1, 3, 4).reshape(b, S, h, d))
        ms.append(m.reshape(b, dil, L, h).transpose(0, 2, 1, 3).reshape(b, S, h))
        ls.append(l.reshape(b, dil, L, h).transpose(0, 2, 1, 3).reshape(b, S, h))
    o = jnp.stack(outs)
    m = jnp.stack(ms)
    l = jnp.stack(ls)
    w = l * jnp.exp(m - m.max(axis=0, keepdims=True))
    return (w[..., None] * o).sum(axis=0) / w.sum(axis=0)[..., None]


def setup_inputs(seed: int = 0) -> dict:
    key = jax.random.key(seed)
    ks = jax.random.split(key, 10)
    f32 = jnp.float32
    x = jax.random.normal(ks[0], (BATCH, SEQ, D_MODEL), f32)
    norm_gain = 1.0 + 0.1 * jax.random.normal(ks[1], (DEPTH, D_MODEL), f32)
    w_in = jax.random.normal(ks[2], (DEPTH, D_MODEL, IN_WIDTH), f32) * D_MODEL ** -0.5
    q_norm_a = 1.0 + 0.1 * jax.random.normal(ks[3], (DEPTH, HEAD_DIM), f32)
    k_norm_a = 1.0 + 0.1 * jax.random.normal(ks[4], (DEPTH, HEAD_DIM), f32)
    sinks_a = 0.5 * jax.random.normal(ks[5], (DEPTH, A_HEADS), f32)
    q_norm_b = 1.0 + 0.1 * jax.random.normal(ks[6], (DEPTH, HEAD_DIM), f32)
    k_norm_b = 1.0 + 0.1 * jax.random.normal(ks[7], (DEPTH, HEAD_DIM), f32)
    w_out = jax.random.normal(ks[8], (DEPTH, MIX_WIDTH, D_MODEL), f32) * MIX_WIDTH ** -0.5
    return {"x": x, "norm_gain": norm_gain, "w_in": w_in,
            "q_norm_a": q_norm_a, "k_norm_a": k_norm_a, "sinks_a": sinks_a,
            "q_norm_b": q_norm_b, "k_norm_b": k_norm_b, "w_out": w_out}


def reference(x, norm_gain, w_in, q_norm_a, k_norm_a, sinks_a, q_norm_b, k_norm_b, w_out):
    b, S, _ = x.shape
    pos = jnp.arange(S)
    split_at = [int(c) for c in np.cumsum(IN_SPLITS)[:-1]]
    for i in range(DEPTH):
        hdn = rmsnorm(x, norm_gain[i])
        proj = jnp.einsum("bsd,de->bse", hdn, w_in[i])
        q_a, k_a, v_a, g_a, q_b, k_b, v_b, g_b = jnp.split(proj, split_at, axis=-1)

        q_a = rope(rmsnorm(q_a.reshape(b, S, A_HEADS, HEAD_DIM), q_norm_a[i]), pos)
        k_a = rope(rmsnorm(k_a.reshape(b, S, A_KV_HEADS, HEAD_DIM), k_norm_a[i]), pos)
        v_a = v_a.reshape(b, S, A_KV_HEADS, HEAD_DIM)
        o_a, _, _ = banded_attention(q_a, k_a, v_a, A_WINDOW - 1, sinks=sinks_a[i])
        o_a = o_a.reshape(b, S, A_WIDTH).astype(x.dtype) * jax.nn.silu(g_a)

        q_b = rope(rmsnorm(q_b.reshape(b, S, B_HEADS, HEAD_DIM), q_norm_b[i]), pos)
        k_b = rope(rmsnorm(k_b.reshape(b, S, B_HEADS, HEAD_DIM), k_norm_b[i]), pos)
        v_b = v_b.reshape(b, S, B_HEADS, HEAD_DIM)
        o_b = dilated_attention(q_b, k_b, v_b)
        o_b = o_b.reshape(b, S, B_WIDTH).astype(x.dtype) * jax.nn.silu(g_b)

        mixed = jnp.concatenate([o_a, o_b], axis=-1)
        x = x + jnp.einsum("bse,ed->bsd", mixed, w_out[i])
    return x
```

```python
import functools

import numpy as np
import jax
import jax.numpy as jnp
from jax import lax
from jax.experimental import pallas as pl
from jax.experimental.pallas import tpu as pltpu

HEAD_DIM = 64
HALF = HEAD_DIM // 2
A_HEADS = 8
A_KV_HEADS = 2
A_WINDOW = 128
B_HEADS = 8
B_PATTERNS = ((128, 1), (512, 4), (2048, 16))
BLOCK = 128
ROPE_THETA = 10000.0
EPS = 1e-6
NEG = -1e30

A_WIDTH = A_HEADS * HEAD_DIM
A_KV_WIDTH = A_KV_HEADS * HEAD_DIM
B_WIDTH = B_HEADS * HEAD_DIM

LANES = 128
SEG_TILE = 256
PROJ_ROWS = 512
VMEM_LIMIT = 56 * 1024 * 1024


def _split_hi_lo(v):
    hi = v.astype(jnp.bfloat16)
    lo = (v - hi.astype(jnp.float32)).astype(jnp.bfloat16)
    return hi, lo


def _dot(a, b):
    return jnp.dot(a, b, preferred_element_type=jnp.float32)


def _head_norm_rope(p, seg, gain, cos, sin, hi_mask):
    width = p.shape[-1]
    sq = p * p
    hi, lo = _split_hi_lo(sq)
    step = min(width, SEG_TILE)
    sums = []
    for c in range(0, width, step):
        blk = seg[:step, :step]
        sums.append(_dot(hi[:, c:c + step], blk) + _dot(lo[:, c:c + step], blk))
    ss = sums[0] if len(sums) == 1 else jnp.concatenate(sums, axis=-1)
    y = p * lax.rsqrt(ss * (1.0 / HEAD_DIM) + EPS) * gain
    outs = []
    for c in range(0, width, LANES):
        yc = y[:, c:c + LANES]
        partner = jnp.where(hi_mask, pltpu.roll(yc, HALF, 1), pltpu.roll(yc, LANES - HALF, 1))
        outs.append(yc * cos + partner * sin)
    return outs[0] if len(outs) == 1 else jnp.concatenate(outs, axis=-1)


def _proj_kernel(x_ref, gain_ref, w_ref, seg_ref, cos_ref, sin_ref,
                 gqa_ref, gka_ref, gqb_ref, gkb_ref,
                 qa_ref, ka_ref, va_ref, ga_ref, qb_ref, kb_ref, vb_ref, gb_ref):
    xf = x_ref[...]
    ms = jnp.mean(xf * xf, axis=-1, keepdims=True)
    h = (xf * lax.rsqrt(ms + EPS) * gain_ref[...]).astype(jnp.bfloat16)
    seg = seg_ref[...]
    cos = cos_ref[...]
    sin = sin_ref[...]
    lane = lax.broadcasted_iota(jnp.int32, (1, LANES), 1)
    hi_mask = (lane & HALF) != 0

    def proj(lo, width):
        return _dot(h, w_ref[:, lo:lo + width])

    o = 0
    qa_ref[...] = _head_norm_rope(proj(o, A_WIDTH), seg, gqa_ref[...], cos, sin, hi_mask).astype(qa_ref.dtype)
    o += A_WIDTH
    ka_ref[...] = _head_norm_rope(proj(o, A_KV_WIDTH), seg, gka_ref[...], cos, sin, hi_mask).astype(ka_ref.dtype)
    o += A_KV_WIDTH
    va_ref[...] = proj(o, A_KV_WIDTH).astype(va_ref.dtype)
    o += A_KV_WIDTH
    ga_ref[...] = proj(o, A_WIDTH)
    o += A_WIDTH
    qb_ref[...] = _head_norm_rope(proj(o, B_WIDTH), seg, gqb_ref[...], cos, sin, hi_mask).astype(qb_ref.dtype)
    o += B_WIDTH
    kb_ref[...] = _head_norm_rope(proj(o, B_WIDTH), seg, gkb_ref[...], cos, sin, hi_mask).astype(kb_ref.dtype)
    o += B_WIDTH
    vb_ref[...] = proj(o, B_WIDTH).astype(vb_ref.dtype)
    o += B_WIDTH
    gb_ref[...] = proj(o, B_WIDTH)


def _proj_call(x2d, gain, w_bf, seg, cos_t, sin_t, gqa, gka, gqb, gkb, seq):
    tokens, d_model = x2d.shape
    tm = PROJ_ROWS
    steps_per_seq = seq // tm
    row = lambda w: pl.BlockSpec((tm, w), lambda g: (g, 0))
    full = lambda a: pl.BlockSpec(a.shape, lambda g: (0,) * a.ndim)
    table = pl.BlockSpec((tm, LANES), lambda g: (g % steps_per_seq, 0))
    bf = jnp.bfloat16
    out_shape = [
        jax.ShapeDtypeStruct((tokens, A_WIDTH), bf),
        jax.ShapeDtypeStruct((tokens, A_KV_WIDTH), bf),
        jax.ShapeDtypeStruct((tokens, A_KV_WIDTH), bf),
        jax.ShapeDtypeStruct((tokens, A_WIDTH), jnp.float32),
        jax.ShapeDtypeStruct((tokens, B_WIDTH), bf),
        jax.ShapeDtypeStruct((tokens, B_WIDTH), bf),
        jax.ShapeDtypeStruct((tokens, B_WIDTH), bf),
        jax.ShapeDtypeStruct((tokens, B_WIDTH), jnp.float32),
    ]
    return pl.pallas_call(
        _proj_kernel,
        grid=(tokens // tm,),
        in_specs=[row(d_model), full(gain), full(w_bf), full(seg), table, table,
                  full(gqa), full(gka), full(gqb), full(gkb)],
        out_specs=[row(s.shape[1]) for s in out_shape],
        out_shape=out_shape,
        compiler_params=pltpu.CompilerParams(
            dimension_semantics=("parallel",), vmem_limit_bytes=VMEM_LIMIT),
        name="proj",
    )(x2d, gain, w_bf, seg, cos_t, sin_t, gqa, gka, gqb, gkb)


def _attn_kernel(*refs, heads, kv_heads, tq, has_sinks, has_gate, want_lse):
    it = iter(refs)
    q_ref, kc_ref, kp_ref, vc_ref, vp_ref, bias_ref = (next(it) for _ in range(6))
    sink_ref = next(it) if has_sinks else None
    gate_ref = next(it) if has_gate else None
    o_ref = next(it)
    lse_ref = next(it) if want_lse else None
    kbuf, vbuf = next(it), next(it)

    group = heads // kv_heads
    first_tile = pl.program_id(1) == 0

    kbuf[0:BLOCK, :] = kp_ref[...]
    kbuf[BLOCK:, :] = kc_ref[...]
    vbuf[0:BLOCK, :] = vp_ref[...]
    vbuf[BLOCK:, :] = vc_ref[...]

    def sub_block(j, bias):
        r0 = pl.multiple_of(j * BLOCK, BLOCK)
        for hk in range(kv_heads):
            cols = slice(hk * HEAD_DIM, (hk + 1) * HEAD_DIM)
            kw = kbuf[pl.ds(r0, 2 * BLOCK), cols]
            vw = vbuf[pl.ds(r0, 2 * BLOCK), cols]
            for g in range(group):
                hd = hk * group + g
                hc = slice(hd * HEAD_DIM, (hd + 1) * HEAD_DIM)
                qh = q_ref[pl.ds(r0, BLOCK), hc]
                s = lax.dot_general(qh, kw, (((1,), (1,)), ((), ())),
                                    preferred_element_type=jnp.float32) + bias
                m = jnp.max(s, axis=-1, keepdims=True)
                if has_sinks:
                    sk = sink_ref[:, hd:hd + 1]
                    m = jnp.maximum(m, sk)
                p = jnp.exp(s - m)
                l = jnp.sum(p, axis=-1, keepdims=True)
                if has_sinks:
                    l = l + jnp.exp(sk - m)
                o = _dot(p.astype(jnp.bfloat16), vw) / l
                if has_gate:
                    gt = gate_ref[pl.ds(r0, BLOCK), hc]
                    o = o * (gt * jax.nn.sigmoid(gt))
                o_ref[pl.ds(r0, BLOCK), hc] = o.astype(o_ref.dtype)
                if want_lse:
                    lse_ref[pl.ds(r0, BLOCK), hd:hd + 1] = m + jnp.log(l)

    sub_block(0, bias_ref[jnp.where(first_tile, 1, 0)])

    def body(j, carry):
        sub_block(j, bias_ref[0])
        return carry

    lax.fori_loop(1, tq // BLOCK, body, 0)


def _band_bias(max_dist):
    qi = np.arange(BLOCK)[:, None]
    sj = np.arange(2 * BLOCK)[None, :]
    dist = qi - sj + BLOCK
    valid = (dist >= 0) & (dist <= max_dist)
    first = valid & (sj >= BLOCK)
    return jnp.asarray(np.where(np.stack([valid, first]), 0.0, NEG), jnp.float32)


def _attn_call(q, k, v, max_dist, *, heads, kv_heads, sinks=None, gate=None, want_lse=False, name):
    n, seq, qw = q.shape
    kvw = k.shape[-1]
    tq = min(seq, 512)
    sub = tq // BLOCK
    cur = lambda w: pl.BlockSpec((None, tq, w), lambda b, i: (b, i, 0))
    prev = lambda w: pl.BlockSpec((None, BLOCK, w), lambda b, i: (b, jnp.maximum(i * sub - 1, 0), 0))
    bias = _band_bias(max_dist)
    args = [q, k, k, v, v, bias]
    in_specs = [cur(qw), cur(kvw), prev(kvw), cur(kvw), prev(kvw),
                pl.BlockSpec(bias.shape, lambda b, i: (0, 0, 0))]
    if sinks is not None:
        args.append(sinks)
        in_specs.append(pl.BlockSpec(sinks.shape, lambda b, i: (0, 0)))
    if gate is not None:
        args.append(gate)
        in_specs.append(cur(qw))
    out_shape = [jax.ShapeDtypeStruct((n, seq, qw), jnp.bfloat16)]
    out_specs = [cur(qw)]
    if want_lse:
        out_shape.append(jax.ShapeDtypeStruct((n, seq, heads), jnp.float32))
        out_specs.append(cur(heads))
    kernel = functools.partial(_attn_kernel, heads=heads, kv_heads=kv_heads, tq=tq,
                               has_sinks=sinks is not None, has_gate=gate is not None,
                               want_lse=want_lse)
    return pl.pallas_call(
        kernel,
        grid=(n, seq // tq),
        in_specs=in_specs,
        out_specs=out_specs,
        out_shape=out_shape,
        scratch_shapes=[pltpu.VMEM((tq + BLOCK, kvw), jnp.bfloat16),
                        pltpu.VMEM((tq + BLOCK, kvw), jnp.bfloat16)],
        compiler_params=pltpu.CompilerParams(
            dimension_semantics=("parallel", "arbitrary"), vmem_limit_bytes=VMEM_LIMIT),
        name=name,
    )(*args)


def _expand_heads(w, expand):
    hi = w.astype(jnp.bfloat16)
    r1 = w - hi.astype(jnp.float32)
    mid = r1.astype(jnp.bfloat16)
    lo = (r1 - mid.astype(jnp.float32)).astype(jnp.bfloat16)
    return _dot(hi, expand) + _dot(mid, expand) + _dot(lo, expand)


def _out_kernel(x_ref, ma_ref, o1_ref, o2_ref, o3_ref, l1_ref, l2_ref, l3_ref, gb_ref,
                expand_ref, w_ref, out_ref):
    l1, l2, l3 = l1_ref[...], l2_ref[...], l3_ref[...]
    top = jnp.maximum(jnp.maximum(l1, l2), l3)
    e1, e2, e3 = jnp.exp(l1 - top), jnp.exp(l2 - top), jnp.exp(l3 - top)
    inv = 1.0 / (e1 + e2 + e3)
    expand = expand_ref[...]
    ob = (_expand_heads(e1 * inv, expand) * o1_ref[...].astype(jnp.float32)
          + _expand_heads(e2 * inv, expand) * o2_ref[...].astype(jnp.float32)
          + _expand_heads(e3 * inv, expand) * o3_ref[...].astype(jnp.float32))
    gb = gb_ref[...]
    mb = (ob * (gb * jax.nn.sigmoid(gb))).astype(jnp.bfloat16)
    out_ref[...] = (x_ref[...] + _dot(ma_ref[...], w_ref[0:A_WIDTH, :])
                    + _dot(mb, w_ref[A_WIDTH:, :]))


def _out_call(x2d, mixed_a, o_b, lse_b, gate_b, expand, w_bf):
    tokens, d_model = x2d.shape
    tm = PROJ_ROWS
    row = lambda w: pl.BlockSpec((tm, w), lambda g: (g, 0))
    full = lambda a: pl.BlockSpec(a.shape, lambda g: (0,) * a.ndim)
    return pl.pallas_call(
        _out_kernel,
        grid=(tokens // tm,),
        in_specs=[row(d_model), row(A_WIDTH)] + [row(B_WIDTH)] * 3 + [row(B_HEADS)] * 3
                 + [row(B_WIDTH), full(expand), full(w_bf)],
        out_specs=row(d_model),
        out_shape=jax.ShapeDtypeStruct((tokens, d_model), jnp.float32),
        compiler_params=pltpu.CompilerParams(
            dimension_semantics=("parallel",), vmem_limit_bytes=VMEM_LIMIT),
        name="merge_out",
    )(x2d, mixed_a, *o_b, *lse_b, gate_b, expand, w_bf)


def _rope_tables(seq):
    inv = ROPE_THETA ** (-jnp.arange(HALF, dtype=jnp.float32) / HALF)
    ang = jnp.arange(seq).astype(jnp.float32)[:, None] * inv[None, :]
    reps = LANES // HALF
    cos_t = jnp.tile(jnp.cos(ang), (1, reps))
    sign = jnp.tile(jnp.concatenate([-jnp.ones((HALF,), jnp.float32), jnp.ones((HALF,), jnp.float32)]),
                    LANES // HEAD_DIM)
    sin_t = jnp.tile(jnp.sin(ang), (1, reps)) * sign[None, :]
    return cos_t, sin_t


def _fold(t, dil):
    b, s, w = t.shape
    return t.reshape(b, s // dil, dil, w).transpose(0, 2, 1, 3).reshape(b * dil, s // dil, w)


def _unfold(t, dil, b):
    n, l, w = t.shape
    return t.reshape(b, dil, l, w).transpose(0, 2, 1, 3).reshape(b, l * dil, w)


def kernel(x, norm_gain, w_in, q_norm_a, k_norm_a, sinks_a, q_norm_b, k_norm_b, w_out):
    b, seq, d_model = x.shape
    tokens = b * seq
    depth = norm_gain.shape[0]
    scale = HEAD_DIM ** -0.5
    cos_t, sin_t = _rope_tables(seq)
    seg_i = np.arange(SEG_TILE) // HEAD_DIM
    seg = jnp.asarray(seg_i[:, None] == seg_i[None, :], jnp.bfloat16)
    exp_i = np.arange(B_WIDTH) // HEAD_DIM
    expand = jnp.asarray(np.arange(B_HEADS)[:, None] == exp_i[None, :], jnp.bfloat16)

    for i in range(depth):
        x2d = x.reshape(tokens, d_model)
        tile = lambda g, n, s=1.0: jnp.tile(g * s, n)[None, :]
        qa, ka, va, ga, qb, kb, vb, gb = _proj_call(
            x2d, norm_gain[i][None, :], w_in[i].astype(jnp.bfloat16), seg, cos_t, sin_t,
            tile(q_norm_a[i], A_HEADS, scale), tile(k_norm_a[i], A_KV_HEADS),
            tile(q_norm_b[i], B_HEADS, scale), tile(k_norm_b[i], B_HEADS), seq)

        r3 = lambda t: t.reshape(b, seq, t.shape[-1])
        (mixed_a,) = _attn_call(r3(qa), r3(ka), r3(va), A_WINDOW - 1,
                                heads=A_HEADS, kv_heads=A_KV_HEADS,
                                sinks=sinks_a[i][None, :], gate=r3(ga), name="attn_a")

        o_b, lse_b = [], []
        for window, dil in B_PATTERNS:
            o_p, lse_p = _attn_call(_fold(r3(qb), dil), _fold(r3(kb), dil), _fold(r3(vb), dil),
                                    window // dil, heads=B_HEADS, kv_heads=B_HEADS,
                                    want_lse=True, name=f"attn_b{dil}")
            o_b.append(_unfold(o_p, dil, b).reshape(tokens, B_WIDTH))
            lse_b.append(_unfold(lse_p, dil, b).reshape(tokens, B_HEADS))

        out = _out_call(x2d, mixed_a.reshape(tokens, A_WIDTH), o_b, lse_b, gb, expand,
                        w_out[i].astype(jnp.bfloat16))
        x = out.reshape(b, seq, d_model)
    return x
```

```python
import functools

import numpy as np
import jax
import jax.numpy as jnp
from jax import lax
from jax.experimental import pallas as pl
from jax.experimental.pallas import tpu as pltpu

HEAD_DIM = 64
HALF = HEAD_DIM // 2
A_HEADS = 8
A_KV_HEADS = 2
A_WINDOW = 128
B_HEADS = 8
B_PATTERNS = ((128, 1), (512, 4), (2048, 16))
BLOCK = 128
ROPE_THETA = 10000.0
EPS = 1e-6
NEG = -1e30

A_WIDTH = A_HEADS * HEAD_DIM
A_KV_WIDTH = A_KV_HEADS * HEAD_DIM
B_WIDTH = B_HEADS * HEAD_DIM

LANES = 128
SEG_TILE = 256
PROJ_ROWS = 512
VMEM_LIMIT = 56 * 1024 * 1024

A_KV_PAIR_WIDTH = 2 * A_KV_HEADS * HEAD_DIM


def _split_hi_lo(v):
    hi = v.astype(jnp.bfloat16)
    lo = (v - hi.astype(jnp.float32)).astype(jnp.bfloat16)
    return hi, lo


def _dot(a, b):
    return jnp.dot(a, b, preferred_element_type=jnp.float32)


def _head_norm_rope(p, seg, gain, cos, sin, hi_mask):
    width = p.shape[-1]
    sq = p * p
    hi, lo = _split_hi_lo(sq)
    step = min(width, SEG_TILE)
    sums = []
    for c in range(0, width, step):
        blk = seg[:step, :step]
        sums.append(_dot(hi[:, c:c + step], blk) + _dot(lo[:, c:c + step], blk))
    ss = sums[0] if len(sums) == 1 else jnp.concatenate(sums, axis=-1)
    y = p * lax.rsqrt(ss * (1.0 / HEAD_DIM) + EPS) * gain
    outs = []
    for c in range(0, width, LANES):
        yc = y[:, c:c + LANES]
        partner = jnp.where(hi_mask, pltpu.roll(yc, HALF, 1), pltpu.roll(yc, LANES - HALF, 1))
        outs.append(yc * cos + partner * sin)
    return outs[0] if len(outs) == 1 else jnp.concatenate(outs, axis=-1)


def _proj_kernel(x_ref, gain_ref, w_ref, seg_ref, cos_ref, sin_ref,
                 gqa_ref, gka_ref, gqb_ref, gkb_ref,
                 qa_ref, ka_ref, va_ref, ga_ref, qb_ref, kb_ref, vb_ref, gb_ref):
    xf = x_ref[...]
    ms = jnp.mean(xf * xf, axis=-1, keepdims=True)
    h = (xf * lax.rsqrt(ms + EPS) * gain_ref[...]).astype(jnp.bfloat16)
    seg = seg_ref[...]
    cos = cos_ref[...]
    sin = sin_ref[...]
    lane = lax.broadcasted_iota(jnp.int32, (1, LANES), 1)
    hi_mask = (lane & HALF) != 0

    def proj(lo, width):
        return _dot(h, w_ref[:, lo:lo + width])

    o = 0
    qa_ref[...] = _head_norm_rope(proj(o, A_WIDTH), seg, gqa_ref[...], cos, sin, hi_mask).astype(qa_ref.dtype)
    o += A_WIDTH
    ka_ref[...] = _head_norm_rope(proj(o, A_KV_PAIR_WIDTH), seg, gka_ref[...], cos, sin, hi_mask).astype(ka_ref.dtype)
    o += A_KV_PAIR_WIDTH
    va_ref[...] = proj(o, A_KV_PAIR_WIDTH).astype(va_ref.dtype)
    o += A_KV_PAIR_WIDTH
    ga_ref[...] = proj(o, A_WIDTH)
    o += A_WIDTH
    qb_ref[...] = _head_norm_rope(proj(o, B_WIDTH), seg, gqb_ref[...], cos, sin, hi_mask).astype(qb_ref.dtype)
    o += B_WIDTH
    kb_ref[...] = _head_norm_rope(proj(o, B_WIDTH), seg, gkb_ref[...], cos, sin, hi_mask).astype(kb_ref.dtype)
    o += B_WIDTH
    vb_ref[...] = proj(o, B_WIDTH).astype(vb_ref.dtype)
    o += B_WIDTH
    gb_ref[...] = proj(o, B_WIDTH)


def _proj_call(x2d, gain, w_bf, seg, cos_t, sin_t, gqa, gka, gqb, gkb, seq):
    tokens, d_model = x2d.shape
    tm = PROJ_ROWS
    steps_per_seq = seq // tm
    row = lambda w: pl.BlockSpec((tm, w), lambda g: (g, 0))
    full = lambda a: pl.BlockSpec(a.shape, lambda g: (0,) * a.ndim)
    table = pl.BlockSpec((tm, LANES), lambda g: (g % steps_per_seq, 0))
    bf = jnp.bfloat16
    out_shape = [
        jax.ShapeDtypeStruct((tokens, A_WIDTH), bf),
        jax.ShapeDtypeStruct((tokens, A_KV_PAIR_WIDTH), bf),
        jax.ShapeDtypeStruct((tokens, A_KV_PAIR_WIDTH), bf),
        jax.ShapeDtypeStruct((tokens, A_WIDTH), jnp.float32),
        jax.ShapeDtypeStruct((tokens, B_WIDTH), bf),
        jax.ShapeDtypeStruct((tokens, B_WIDTH), bf),
        jax.ShapeDtypeStruct((tokens, B_WIDTH), bf),
        jax.ShapeDtypeStruct((tokens, B_WIDTH), jnp.float32),
    ]
    return pl.pallas_call(
        _proj_kernel,
        grid=(tokens // tm,),
        in_specs=[row(d_model), full(gain), full(w_bf), full(seg), table, table,
                  full(gqa), full(gka), full(gqb), full(gkb)],
        out_specs=[row(s.shape[1]) for s in out_shape],
        out_shape=out_shape,
        compiler_params=pltpu.CompilerParams(
            dimension_semantics=("parallel",), vmem_limit_bytes=VMEM_LIMIT),
        name="proj",
    )(x2d, gain, w_bf, seg, cos_t, sin_t, gqa, gka, gqb, gkb)


def _attn_kernel(*refs, pairs, slabs, tq, has_sinks, has_gate, want_lse):
    it = iter(refs)
    q_ref, kc_ref, kp_ref, vc_ref, vp_ref, bias_ref = (next(it) for _ in range(6))
    sink_ref = next(it) if has_sinks else None
    gate_ref = next(it) if has_gate else None
    o_ref = next(it)
    lse_ref = next(it) if want_lse else None
    kbuf, vbuf = next(it), next(it)

    first_tile = pl.program_id(1) == 0

    kbuf[0:BLOCK, :] = kp_ref[...]
    kbuf[BLOCK:, :] = kc_ref[...]
    ones = jnp.ones((tq + BLOCK, LANES), vbuf.dtype)
    for sl in range(slabs):
        src = slice(sl * LANES, (sl + 1) * LANES)
        vbuf[0:BLOCK, 2 * sl * LANES:(2 * sl + 1) * LANES] = vp_ref[:, src]
        vbuf[BLOCK:, 2 * sl * LANES:(2 * sl + 1) * LANES] = vc_ref[:, src]
        vbuf[:, (2 * sl + 1) * LANES:(2 * sl + 2) * LANES] = ones

    lane = lax.broadcasted_iota(jnp.int32, (1, LANES), 1)
    low = lane < HEAD_DIM
    zero = jnp.zeros((), q_ref.dtype)

    def sub_block(j, bias):
        r0 = pl.multiple_of(j * BLOCK, BLOCK)
        rows = pl.ds(r0, BLOCK)
        win = pl.ds(r0, 2 * BLOCK)
        for hp in range(pairs):
            sl = hp * slabs // pairs
            qp = q_ref[rows, hp * LANES:(hp + 1) * LANES]
            lhs = jnp.concatenate([jnp.where(low, qp, zero), jnp.where(low, zero, qp)], axis=0)
            kw = kbuf[win, sl * LANES:(sl + 1) * LANES]
            s = lax.dot_general(lhs, kw, (((1,), (1,)), ((), ())),
                                preferred_element_type=jnp.float32)
            ps, ms = [], []
            for e in range(2):
                se = s[e * BLOCK:(e + 1) * BLOCK] + bias
                m = jnp.max(se, axis=-1, keepdims=True)
                if has_sinks:
                    m = jnp.maximum(m, sink_ref[2 * hp + e])
                ps.append(jnp.exp(se - m).astype(jnp.bfloat16))
                ms.append(m)
            vw = vbuf[win, 2 * sl * LANES:(2 * sl + 2) * LANES]
            r = _dot(jnp.concatenate(ps, axis=0), vw)
            outs = []
            for e in range(2):
                acc = r[e * BLOCK:(e + 1) * BLOCK, :LANES]
                l = r[e * BLOCK:(e + 1) * BLOCK, LANES:]
                if has_sinks:
                    l = l + jnp.exp(sink_ref[2 * hp + e] - ms[e])
                outs.append(acc * (1.0 / l))
                if want_lse:
                    lse_ref[rows, 2 * hp + e:2 * hp + e + 1] = ms[e] + jnp.log(l[:, :1])
            o = jnp.where(low, outs[0], outs[1])
            if has_gate:
                gt = gate_ref[rows, hp * LANES:(hp + 1) * LANES]
                o = o * (gt * jax.nn.sigmoid(gt))
            o_ref[rows, hp * LANES:(hp + 1) * LANES] = o.astype(o_ref.dtype)

    sub_block(0, bias_ref[jnp.where(first_tile, 1, 0)])

    def body(j, carry):
        sub_block(j, bias_ref[0])
        return carry

    lax.fori_loop(1, tq // BLOCK, body, 0)


def _band_bias(max_dist):
    qi = np.arange(BLOCK)[:, None]
    sj = np.arange(2 * BLOCK)[None, :]
    dist = qi - sj + BLOCK
    valid = (dist >= 0) & (dist <= max_dist)
    first = valid & (sj >= BLOCK)
    return jnp.asarray(np.where(np.stack([valid, first]), 0.0, NEG), jnp.float32)


def _attn_call(q, k, v, max_dist, *, sinks=None, gate=None, want_lse=False, name):
    n, seq, qw = q.shape
    kvw = k.shape[-1]
    pairs, slabs = qw // LANES, kvw // LANES
    tq = min(seq, 512)
    sub = tq // BLOCK
    cur = lambda w: pl.BlockSpec((None, tq, w), lambda b, i: (b, i, 0))
    prev = lambda w: pl.BlockSpec((None, BLOCK, w), lambda b, i: (b, jnp.maximum(i * sub - 1, 0), 0))
    bias = _band_bias(max_dist)
    args = [q, k, k, v, v, bias]
    in_specs = [cur(qw), cur(kvw), prev(kvw), cur(kvw), prev(kvw),
                pl.BlockSpec(bias.shape, lambda b, i: (0, 0, 0))]
    if sinks is not None:
        args.append(sinks)
        in_specs.append(pl.BlockSpec(memory_space=pltpu.SMEM))
    if gate is not None:
        args.append(gate)
        in_specs.append(cur(qw))
    out_shape = [jax.ShapeDtypeStruct((n, seq, qw), jnp.bfloat16)]
    out_specs = [cur(qw)]
    if want_lse:
        out_shape.append(jax.ShapeDtypeStruct((n, seq, 2 * pairs), jnp.float32))
        out_specs.append(cur(2 * pairs))
    kernel = functools.partial(_attn_kernel, pairs=pairs, slabs=slabs, tq=tq,
                               has_sinks=sinks is not None, has_gate=gate is not None,
                               want_lse=want_lse)
    return pl.pallas_call(
        kernel,
        grid=(n, seq // tq),
        in_specs=in_specs,
        out_specs=out_specs,
        out_shape=out_shape,
        scratch_shapes=[pltpu.VMEM((tq + BLOCK, kvw), jnp.bfloat16),
                        pltpu.VMEM((tq + BLOCK, 2 * kvw), jnp.bfloat16)],
        compiler_params=pltpu.CompilerParams(
            dimension_semantics=("parallel", "arbitrary"), vmem_limit_bytes=VMEM_LIMIT),
        name=name,
    )(*args)


def _expand_heads(w, expand):
    hi = w.astype(jnp.bfloat16)
    r1 = w - hi.astype(jnp.float32)
    mid = r1.astype(jnp.bfloat16)
    lo = (r1 - mid.astype(jnp.float32)).astype(jnp.bfloat16)
    return _dot(hi, expand) + _dot(mid, expand) + _dot(lo, expand)


def _out_kernel(x_ref, ma_ref, o1_ref, o2_ref, o3_ref, l1_ref, l2_ref, l3_ref, gb_ref,
                expand_ref, w_ref, out_ref):
    l1, l2, l3 = l1_ref[...], l2_ref[...], l3_ref[...]
    top = jnp.maximum(jnp.maximum(l1, l2), l3)
    e1, e2, e3 = jnp.exp(l1 - top), jnp.exp(l2 - top), jnp.exp(l3 - top)
    inv = 1.0 / (e1 + e2 + e3)
    expand = expand_ref[...]
    ob = (_expand_heads(e1 * inv, expand) * o1_ref[...].astype(jnp.float32)
          + _expand_heads(e2 * inv, expand) * o2_ref[...].astype(jnp.float32)
          + _expand_heads(e3 * inv, expand) * o3_ref[...].astype(jnp.float32))
    gb = gb_ref[...]
    mb = (ob * (gb * jax.nn.sigmoid(gb))).astype(jnp.bfloat16)
    out_ref[...] = (x_ref[...] + _dot(ma_ref[...], w_ref[0:A_WIDTH, :])
                    + _dot(mb, w_ref[A_WIDTH:, :]))


def _out_call(x2d, mixed_a, o_b, lse_b, gate_b, expand, w_bf):
    tokens, d_model = x2d.shape
    tm = PROJ_ROWS
    row = lambda w: pl.BlockSpec((tm, w), lambda g: (g, 0))
    full = lambda a: pl.BlockSpec(a.shape, lambda g: (0,) * a.ndim)
    return pl.pallas_call(
        _out_kernel,
        grid=(tokens // tm,),
        in_specs=[row(d_model), row(A_WIDTH)] + [row(B_WIDTH)] * 3 + [row(B_HEADS)] * 3
                 + [row(B_WIDTH), full(expand), full(w_bf)],
        out_specs=row(d_model),
        out_shape=jax.ShapeDtypeStruct((tokens, d_model), jnp.float32),
        compiler_params=pltpu.CompilerParams(
            dimension_semantics=("parallel",), vmem_limit_bytes=VMEM_LIMIT),
        name="merge_out",
    )(x2d, mixed_a, *o_b, *lse_b, gate_b, expand, w_bf)


def _rope_tables(seq):
    inv = ROPE_THETA ** (-jnp.arange(HALF, dtype=jnp.float32) / HALF)
    ang = jnp.arange(seq).astype(jnp.float32)[:, None] * inv[None, :]
    reps = LANES // HALF
    cos_t = jnp.tile(jnp.cos(ang), (1, reps))
    sign = jnp.tile(jnp.concatenate([-jnp.ones((HALF,), jnp.float32), jnp.ones((HALF,), jnp.float32)]),
                    LANES // HEAD_DIM)
    sin_t = jnp.tile(jnp.sin(ang), (1, reps)) * sign[None, :]
    return cos_t, sin_t


def _fold(t, dil):
    b, s, w = t.shape
    return t.reshape(b, s // dil, dil, w).transpose(0, 2, 1, 3).reshape(b * dil, s // dil, w)


def _unfold(t, dil, b):
    n, l, w = t.shape
    return t.reshape(b, dil, l, w).transpose(0, 2, 1, 3).reshape(b, l * dil, w)


def _projection_weights(w):
    bounds = np.cumsum([0, A_WIDTH, A_KV_WIDTH, A_KV_WIDTH])
    q_a, k_a, v_a, rest = (w[:, bounds[0]:bounds[1]], w[:, bounds[1]:bounds[2]],
                           w[:, bounds[2]:bounds[3]], w[:, bounds[3]:])
    rep = lambda t: jnp.repeat(t.reshape(t.shape[0], A_KV_HEADS, HEAD_DIM), 2, axis=1).reshape(t.shape[0], -1)
    return jnp.concatenate([q_a, rep(k_a), rep(v_a), rest], axis=1).astype(jnp.bfloat16)


def kernel(x, norm_gain, w_in, q_norm_a, k_norm_a, sinks_a, q_norm_b, k_norm_b, w_out):
    b, seq, d_model = x.shape
    tokens = b * seq
    depth = norm_gain.shape[0]
    scale = HEAD_DIM ** -0.5
    cos_t, sin_t = _rope_tables(seq)
    seg_i = np.arange(SEG_TILE) // HEAD_DIM
    seg = jnp.asarray(seg_i[:, None] == seg_i[None, :], jnp.bfloat16)
    exp_i = np.arange(B_WIDTH) // HEAD_DIM
    expand = jnp.asarray(np.arange(B_HEADS)[:, None] == exp_i[None, :], jnp.bfloat16)

    for i in range(depth):
        x2d = x.reshape(tokens, d_model)
        tile = lambda g, n, s=1.0: jnp.tile(g * s, n)[None, :]
        qa, ka, va, ga, qb, kb, vb, gb = _proj_call(
            x2d, norm_gain[i][None, :], _projection_weights(w_in[i]), seg, cos_t, sin_t,
            tile(q_norm_a[i], A_HEADS, scale), tile(k_norm_a[i], 2 * A_KV_HEADS),
            tile(q_norm_b[i], B_HEADS, scale), tile(k_norm_b[i], B_HEADS), seq)

        r3 = lambda t: t.reshape(b, seq, t.shape[-1])
        (mixed_a,) = _attn_call(r3(qa), r3(ka), r3(va), A_WINDOW - 1,
                                sinks=sinks_a[i], gate=r3(ga), name="attn_a")

        o_b, lse_b = [], []
        for window, dil in B_PATTERNS:
            o_p, lse_p = _attn_call(_fold(r3(qb), dil), _fold(r3(kb), dil), _fold(r3(vb), dil),
                                    window // dil, want_lse=True, name=f"attn_b{dil}")
            o_b.append(_unfold(o_p, dil, b).reshape(tokens, B_WIDTH))
            lse_b.append(_unfold(lse_p, dil, b).reshape(tokens, B_HEADS))

        out = _out_call(x2d, mixed_a.reshape(tokens, A_WIDTH), o_b, lse_b, gb, expand,
                        w_out[i].astype(jnp.bfloat16))
        x = out.reshape(b, seq, d_model)
    return x
```

```python
import functools

import numpy as np
import jax
import jax.numpy as jnp
from jax import lax
from jax.experimental import pallas as pl
from jax.experimental.pallas import tpu as pltpu

HEAD_DIM = 64
HALF = HEAD_DIM // 2
A_HEADS = 8
A_KV_HEADS = 2
A_WINDOW = 128
B_HEADS = 8
B_PATTERNS = ((128, 1), (512, 4), (2048, 16))
BLOCK = 128
ROPE_THETA = 10000.0
EPS = 1e-6
NEG = -1e30
LOG2E = 1.4426950408889634

A_WIDTH = A_HEADS * HEAD_DIM
A_KV_WIDTH = A_KV_HEADS * HEAD_DIM
B_WIDTH = B_HEADS * HEAD_DIM

LANES = 128
BF16_ROWS = 16
SEG_TILE = 256
PROJ_ROWS = 512
VMEM_LIMIT = 56 * 1024 * 1024

A_KV_PAIR_WIDTH = 2 * A_KV_HEADS * HEAD_DIM

CLASSES = max(d for _, d in B_PATTERNS)
LOCAL = CLASSES * BF16_ROWS
MID_DIL = 4
MID_RUN = BLOCK // MID_DIL
STAT_LANES = 3 * B_HEADS


def _dot(a, b):
    return jnp.dot(a, b, preferred_element_type=jnp.float32)


def _split3(v):
    hi = v.astype(jnp.bfloat16).astype(jnp.float32)
    r1 = v - hi
    mid = r1.astype(jnp.bfloat16).astype(jnp.float32)
    lo = (r1 - mid).astype(jnp.bfloat16).astype(jnp.float32)
    return hi, mid, lo


def _head_norm_rope(p, seg, gain, cos, sin, hi_mask):
    width = p.shape[-1]
    sq = p * p
    hi = sq.astype(jnp.bfloat16)
    lo = (sq - hi.astype(jnp.float32)).astype(jnp.bfloat16)
    step = min(width, SEG_TILE)
    sums = []
    for c in range(0, width, step):
        blk = seg[:step, :step]
        sums.append(_dot(hi[:, c:c + step], blk) + _dot(lo[:, c:c + step], blk))
    ss = sums[0] if len(sums) == 1 else jnp.concatenate(sums, axis=-1)
    y = p * lax.rsqrt(ss * (1.0 / HEAD_DIM) + EPS) * gain
    outs = []
    for c in range(0, width, LANES):
        yc = y[:, c:c + LANES]
        partner = jnp.where(hi_mask, pltpu.roll(yc, HALF, 1), pltpu.roll(yc, LANES - HALF, 1))
        outs.append(yc * cos + partner * sin)
    return outs[0] if len(outs) == 1 else jnp.concatenate(outs, axis=-1)


def _proj_kernel(x_ref, gain_ref, w_ref, seg_ref, perm_ref, cos_ref, sin_ref,
                 gqa_ref, gka_ref, gqb_ref, gkb_ref,
                 qa_ref, ka_ref, va_ref, ga_ref, qb_ref, kb_ref, vb_ref, gb_ref,
                 qc_ref, kc_ref, vc_ref):
    xf = x_ref[...]
    ms = jnp.mean(xf * xf, axis=-1, keepdims=True)
    h = (xf * lax.rsqrt(ms + EPS) * gain_ref[...]).astype(jnp.bfloat16)
    seg = seg_ref[...]
    perm = perm_ref[...]
    cos = cos_ref[...]
    sin = sin_ref[...]
    lane = lax.broadcasted_iota(jnp.int32, (1, LANES), 1)
    hi_mask = (lane & HALF) != 0

    def proj(lo, width):
        return _dot(h, w_ref[:, lo:lo + width])

    def store_both(nat_ref, cm_ref, y):
        nat_ref[...] = y
        for j in range(y.shape[0] // LOCAL):
            z = _dot(perm, y[j * LOCAL:(j + 1) * LOCAL, :]).astype(y.dtype)
            for c in range(CLASSES):
                cm_ref[c, j * BF16_ROWS:(j + 1) * BF16_ROWS, :] = z[c * BF16_ROWS:(c + 1) * BF16_ROWS, :]

    bf = jnp.bfloat16
    o = 0
    qa_ref[...] = _head_norm_rope(proj(o, A_WIDTH), seg, gqa_ref[...], cos, sin, hi_mask).astype(bf)
    o += A_WIDTH
    ka_ref[...] = _head_norm_rope(proj(o, A_KV_PAIR_WIDTH), seg, gka_ref[...], cos, sin, hi_mask).astype(bf)
    o += A_KV_PAIR_WIDTH
    va_ref[...] = proj(o, A_KV_PAIR_WIDTH).astype(bf)
    o += A_KV_PAIR_WIDTH
    ga_ref[...] = proj(o, A_WIDTH)
    o += A_WIDTH
    store_both(qb_ref, qc_ref, _head_norm_rope(proj(o, B_WIDTH), seg, gqb_ref[...], cos, sin, hi_mask).astype(bf))
    o += B_WIDTH
    store_both(kb_ref, kc_ref, _head_norm_rope(proj(o, B_WIDTH), seg, gkb_ref[...], cos, sin, hi_mask).astype(bf))
    o += B_WIDTH
    store_both(vb_ref, vc_ref, proj(o, B_WIDTH).astype(bf))
    o += B_WIDTH
    gb_ref[...] = proj(o, B_WIDTH)


def _proj_call(x2d, gain, w_bf, seg, perm, cos_t, sin_t, gqa, gka, gqb, gkb, batch, seq):
    tokens, d_model = x2d.shape
    tm = PROJ_ROWS
    steps_per_seq = seq // tm
    row = lambda w: pl.BlockSpec((tm, w), lambda g: (g, 0))
    full = lambda a: pl.BlockSpec(a.shape, lambda g: (0,) * a.ndim)
    table = pl.BlockSpec((tm, LANES), lambda g: (g % steps_per_seq, 0))
    cmaj = pl.BlockSpec((None, CLASSES, tm // CLASSES, B_WIDTH),
                        lambda g: (g // steps_per_seq, 0, g % steps_per_seq, 0))
    bf = jnp.bfloat16
    cm_shape = jax.ShapeDtypeStruct((batch, CLASSES, seq // CLASSES, B_WIDTH), bf)
    out_shape = [
        jax.ShapeDtypeStruct((tokens, A_WIDTH), bf),
        jax.ShapeDtypeStruct((tokens, A_KV_PAIR_WIDTH), bf),
        jax.ShapeDtypeStruct((tokens, A_KV_PAIR_WIDTH), bf),
        jax.ShapeDtypeStruct((tokens, A_WIDTH), jnp.float32),
        jax.ShapeDtypeStruct((tokens, B_WIDTH), bf),
        jax.ShapeDtypeStruct((tokens, B_WIDTH), bf),
        jax.ShapeDtypeStruct((tokens, B_WIDTH), bf),
        jax.ShapeDtypeStruct((tokens, B_WIDTH), jnp.float32),
        cm_shape, cm_shape, cm_shape,
    ]
    return pl.pallas_call(
        _proj_kernel,
        grid=(tokens // tm,),
        in_specs=[row(d_model), full(gain), full(w_bf), full(seg), full(perm), table, table,
                  full(gqa), full(gka), full(gqb), full(gkb)],
        out_specs=[row(s.shape[1]) for s in out_shape[:8]] + [cmaj] * 3,
        out_shape=out_shape,
        compiler_params=pltpu.CompilerParams(
            dimension_semantics=("parallel",), vmem_limit_bytes=VMEM_LIMIT),
        name="proj",
    )(x2d, gain, w_bf, seg, perm, cos_t, sin_t, gqa, gka, gqb, gkb)


ONES_ROWS = BF16_ROWS


def _attn_kernel(*refs, pairs, slabs, nblk, strided, has_sinks, has_gate, want_stats):
    it = iter(refs)
    if strided:
        q_ref, kc_ref, vc_ref, bias_ref = (next(it) for _ in range(4))
        kp_ref = vp_ref = None
    else:
        q_ref, kc_ref, kp_ref, vc_ref, vp_ref, bias_ref = (next(it) for _ in range(6))
    sink_ref = next(it) if has_sinks else None
    gate_ref = next(it) if has_gate else None
    o_ref = next(it)
    st_ref = next(it) if want_stats else None
    vtbuf, sbuf, mbuf = (next(it) for _ in range(3))
    lsebuf = next(it) if want_stats else None

    def block_of(ref, m, cols):
        if strided:
            return ref[:, m, :, cols].reshape(BLOCK, LANES)
        return ref[m * BLOCK:(m + 1) * BLOCK, cols]

    def key_block(cur_ref, prev_ref, kb, cols):
        if kb > 0:
            return block_of(cur_ref, kb - 1, cols)
        return block_of(cur_ref, 0, cols) if strided else prev_ref[:, cols]

    def store_block(ref, m, cols, val):
        if strided:
            ref[:, m, :, cols] = val.reshape(BLOCK // MID_RUN, MID_RUN, LANES)
        else:
            ref[m * BLOCK:(m + 1) * BLOCK, cols] = val

    if strided:
        first_bias = bias_ref[1]
    else:
        first_bias = bias_ref[jnp.where(pl.program_id(1) == 0, 1, 0)]

    ones = jnp.ones((ONES_ROWS, BLOCK), vtbuf.dtype)
    for sl in range(slabs):
        src = slice(sl * LANES, (sl + 1) * LANES)
        for kb in range(nblk + 1):
            vtbuf[sl, kb, 0:LANES, :] = key_block(vc_ref, vp_ref, kb, src).T
            vtbuf[sl, kb, LANES:, :] = ones

    lane = lax.broadcasted_iota(jnp.int32, (1, LANES), 1)
    low = lane < HEAD_DIM
    zero = jnp.zeros((), q_ref.dtype)

    def sink_row(hp):
        return jnp.concatenate([jnp.full((1, LANES), sink_ref[2 * hp] * LOG2E, jnp.float32),
                                jnp.full((1, LANES), sink_ref[2 * hp + 1] * LOG2E, jnp.float32)], axis=1)

    def score_stage(j):
        slot = j % 2
        bias_t = first_bias if j == 0 else bias_ref[0]
        for hp in range(pairs):
            sl = hp * slabs // pairs
            kcols = slice(sl * LANES, (sl + 1) * LANES)
            qp = block_of(q_ref, j, slice(hp * LANES, (hp + 1) * LANES))
            qm = jnp.concatenate([jnp.where(low, qp, zero), jnp.where(low, zero, qp)], axis=0)
            kw = jnp.concatenate([key_block(kc_ref, kp_ref, j, kcols),
                                  key_block(kc_ref, kp_ref, j + 1, kcols)], axis=0)
            st = lax.dot_general(kw, qm, (((1,), (1,)), ((), ())),
                                 preferred_element_type=jnp.float32) + bias_t
            m = jnp.max(st, axis=0, keepdims=True)
            if has_sinks:
                m = jnp.maximum(m, sink_row(hp))
            sbuf[slot, hp] = st
            mbuf[slot, hp] = m

    def value_stage(j):
        slot = j % 2
        for hp in range(pairs):
            sl = hp * slabs // pairs
            cols = slice(hp * LANES, (hp + 1) * LANES)
            m = mbuf[slot, hp]
            pt = jnp.exp2(sbuf[slot, hp] - m).astype(jnp.bfloat16)
            vt = jnp.concatenate([vtbuf[sl, j], vtbuf[sl, j + 1]], axis=1)
            r = _dot(vt, pt)
            l = r[LANES:LANES + 1, :]
            if has_sinks:
                l = l + jnp.exp2(sink_row(hp) - m)
            inv = 1.0 / l
            ot = jnp.concatenate([r[0:HEAD_DIM, 0:LANES] * inv[:, 0:LANES],
                                  r[HEAD_DIM:LANES, LANES:] * inv[:, LANES:]], axis=0)
            o = ot.T
            if has_gate:
                gt = block_of(gate_ref, j, cols)
                o = o * (gt * jax.nn.sigmoid(gt))
            store_block(o_ref, j, cols, o.astype(o_ref.dtype))
            if want_stats:
                lse = m + jnp.log2(l)
                lsebuf[slot, 2 * hp:2 * hp + 1, :] = lse[:, 0:LANES]
                lsebuf[slot, 2 * hp + 1:2 * hp + 2, :] = lse[:, LANES:]
        if want_stats:
            terms = _split3(lsebuf[slot])
            pad = jnp.zeros((LANES - STAT_LANES, BLOCK), jnp.float32)
            tile = jnp.concatenate(list(terms) + [pad], axis=0)
            store_block(st_ref, j, slice(0, LANES), tile.T.astype(st_ref.dtype))

    score_stage(0)
    for j in range(nblk):
        if j + 1 < nblk:
            score_stage(j + 1)
        value_stage(j)


def _band_bias(max_dist, mid):
    r = np.arange(BLOCK)
    pos = MID_DIL * (r % MID_RUN) + r // MID_RUN if mid else r
    qpos = pos[None, :] + BLOCK
    kpos = np.concatenate([pos, pos + BLOCK])[:, None]
    dist = qpos - kpos
    valid = (dist >= 0) & (dist <= max_dist)
    first = valid & (np.arange(2 * BLOCK)[:, None] >= BLOCK)
    both = np.stack([valid, first])
    return jnp.asarray(np.where(np.concatenate([both, both], axis=-1), 0.0, NEG), jnp.float32)


def _attn_scratch(pairs, slabs, nblk, want_stats):
    scratch = [pltpu.VMEM((slabs, nblk + 1, LANES + ONES_ROWS, BLOCK), jnp.bfloat16),
               pltpu.VMEM((2, pairs, 2 * BLOCK, 2 * LANES), jnp.float32),
               pltpu.VMEM((2, pairs, 1, 2 * LANES), jnp.float32)]
    if want_stats:
        scratch.append(pltpu.VMEM((2, 2 * pairs, BLOCK), jnp.float32))
    return scratch


def _attn_call(q, k, v, max_dist, *, sinks=None, gate=None, want_stats=False, name):
    n, seq, qw = q.shape
    kvw = k.shape[-1]
    pairs, slabs = qw // LANES, kvw // LANES
    tq = min(seq, 512)
    sub = tq // BLOCK
    cur = lambda w: pl.BlockSpec((None, tq, w), lambda b, i: (b, i, 0))
    prev = lambda w: pl.BlockSpec((None, BLOCK, w), lambda b, i: (b, jnp.maximum(i * sub - 1, 0), 0))
    bias = _band_bias(max_dist, mid=False)
    args = [q, k, k, v, v, bias]
    in_specs = [cur(qw), cur(kvw), prev(kvw), cur(kvw), prev(kvw),
                pl.BlockSpec(bias.shape, lambda b, i: (0, 0, 0))]
    if sinks is not None:
        args.append(sinks)
        in_specs.append(pl.BlockSpec(memory_space=pltpu.SMEM))
    if gate is not None:
        args.append(gate)
        in_specs.append(cur(qw))
    out_shape = [jax.ShapeDtypeStruct((n, seq, qw), jnp.bfloat16)]
    out_specs = [cur(qw)]
    if want_stats:
        out_shape.append(jax.ShapeDtypeStruct((n, seq, LANES), jnp.bfloat16))
        out_specs.append(cur(LANES))
    kernel = functools.partial(_attn_kernel, pairs=pairs, slabs=slabs, nblk=sub, strided=False,
                               has_sinks=sinks is not None, has_gate=gate is not None,
                               want_stats=want_stats)
    return pl.pallas_call(
        kernel,
        grid=(n, seq // tq),
        in_specs=in_specs,
        out_specs=out_specs,
        out_shape=out_shape,
        scratch_shapes=_attn_scratch(pairs, slabs, sub, want_stats),
        compiler_params=pltpu.CompilerParams(
            dimension_semantics=("parallel", "arbitrary"), vmem_limit_bytes=VMEM_LIMIT),
        name=name,
    )(*args)


def _attn_mid_call(q, k, v, max_dist, *, name):
    b, classes, per_class, w = q.shape
    outer = classes // MID_DIL
    nblk = per_class // MID_RUN
    view = lambda t: t.reshape(b, outer, MID_DIL, nblk, MID_RUN, t.shape[-1])
    spec = lambda width: pl.BlockSpec((None, outer, None, nblk, MID_RUN, width),
                                      lambda i, e: (i, 0, e, 0, 0, 0))
    pairs = w // LANES
    bias = _band_bias(max_dist, mid=True)
    shape6 = (b, outer, MID_DIL, nblk, MID_RUN)
    kernel = functools.partial(_attn_kernel, pairs=pairs, slabs=pairs, nblk=nblk, strided=True,
                               has_sinks=False, has_gate=False, want_stats=True)
    o, st = pl.pallas_call(
        kernel,
        grid=(b, MID_DIL),
        in_specs=[spec(w), spec(w), spec(w), pl.BlockSpec(bias.shape, lambda i, e: (0, 0, 0))],
        out_specs=[spec(w), spec(LANES)],
        out_shape=[jax.ShapeDtypeStruct(shape6 + (w,), jnp.bfloat16),
                   jax.ShapeDtypeStruct(shape6 + (LANES,), jnp.bfloat16)],
        scratch_shapes=_attn_scratch(pairs, pairs, nblk, True),
        compiler_params=pltpu.CompilerParams(
            dimension_semantics=("parallel", "arbitrary"), vmem_limit_bytes=VMEM_LIMIT),
        name=name,
    )(view(q), view(k), view(v), bias)
    return o.reshape(b, classes, per_class, w), st.reshape(b, classes, per_class, LANES)


def _expand_heads(w, expand):
    hi, mid, lo = (t.astype(jnp.bfloat16) for t in _split3(w))
    return _dot(hi, expand) + _dot(mid, expand) + _dot(lo, expand)


def _stat_sum(st):
    return (st + pltpu.roll(st, LANES - B_HEADS, 1) + pltpu.roll(st, LANES - 2 * B_HEADS, 1))


def _out_kernel(x_ref, ma_ref, o1_ref, s1_ref, o4_ref, s4_ref, o16_ref, s16_ref, gb_ref,
                permt_ref, expand_ref, w_ref, out_ref):
    rows = x_ref.shape[0]
    permt = permt_ref[...]

    def to_token_order(o_ref, s_ref):
        outs, stats = [], []
        for j in range(rows // LOCAL):
            sel = slice(j * BF16_ROWS, (j + 1) * BF16_ROWS)
            y = jnp.concatenate([jnp.concatenate([o_ref[c, sel, :] for c in range(CLASSES)], axis=0),
                                 jnp.concatenate([s_ref[c, sel, :] for c in range(CLASSES)], axis=0)],
                                axis=1)
            u = _dot(permt, y)
            outs.append(u[:, :B_WIDTH])
            stats.append(u[:, B_WIDTH:])
        return jnp.concatenate(outs, axis=0), jnp.concatenate(stats, axis=0)

    o1 = o1_ref[...].astype(jnp.float32)
    l1 = _stat_sum(s1_ref[...].astype(jnp.float32))
    o4, s4 = to_token_order(o4_ref, s4_ref)
    o16, s16 = to_token_order(o16_ref, s16_ref)
    l4, l16 = _stat_sum(s4), _stat_sum(s16)

    top = jnp.maximum(jnp.maximum(l1, l4), l16)
    e1, e4, e16 = jnp.exp2(l1 - top), jnp.exp2(l4 - top), jnp.exp2(l16 - top)
    head_lane = lax.broadcasted_iota(jnp.int32, (1, LANES), 1) < B_HEADS
    inv = jnp.where(head_lane, 1.0 / (e1 + e4 + e16), 0.0)
    keep = lambda e: jnp.where(head_lane, e, 0.0) * inv
    expand = expand_ref[...]
    ob = (_expand_heads(keep(e1), expand) * o1 + _expand_heads(keep(e4), expand) * o4
          + _expand_heads(keep(e16), expand) * o16)
    gb = gb_ref[...]
    mb = (ob * (gb * jax.nn.sigmoid(gb))).astype(jnp.bfloat16)
    out_ref[...] = (x_ref[...] + _dot(ma_ref[...], w_ref[0:A_WIDTH, :])
                    + _dot(mb, w_ref[A_WIDTH:, :]))


def _out_call(x2d, mixed_a, o1, s1, o4, s4, o16, s16, gate_b, permt, expand, w_bf, seq):
    tokens, d_model = x2d.shape
    tm = PROJ_ROWS
    steps_per_seq = seq // tm
    row = lambda w: pl.BlockSpec((tm, w), lambda g: (g, 0))
    full = lambda a: pl.BlockSpec(a.shape, lambda g: (0,) * a.ndim)
    cmaj = lambda w: pl.BlockSpec((None, CLASSES, tm // CLASSES, w),
                                  lambda g: (g // steps_per_seq, 0, g % steps_per_seq, 0))
    return pl.pallas_call(
        _out_kernel,
        grid=(tokens // tm,),
        in_specs=[row(d_model), row(A_WIDTH), row(B_WIDTH), row(LANES),
                  cmaj(B_WIDTH), cmaj(LANES), cmaj(B_WIDTH), cmaj(LANES),
                  row(B_WIDTH), full(permt), full(expand), full(w_bf)],
        out_specs=row(d_model),
        out_shape=jax.ShapeDtypeStruct((tokens, d_model), jnp.float32),
        compiler_params=pltpu.CompilerParams(
            dimension_semantics=("parallel",), vmem_limit_bytes=VMEM_LIMIT),
        name="merge_out",
    )(x2d, mixed_a, o1, s1, o4, s4, o16, s16, gate_b, permt, expand, w_bf)


def _rope_tables(seq):
    inv = ROPE_THETA ** (-jnp.arange(HALF, dtype=jnp.float32) / HALF)
    ang = jnp.arange(seq).astype(jnp.float32)[:, None] * inv[None, :]
    reps = LANES // HALF
    cos_t = jnp.tile(jnp.cos(ang), (1, reps))
    sign = jnp.tile(jnp.concatenate([-jnp.ones((HALF,), jnp.float32), jnp.ones((HALF,), jnp.float32)]),
                    LANES // HEAD_DIM)
    sin_t = jnp.tile(jnp.sin(ang), (1, reps)) * sign[None, :]
    return cos_t, sin_t


def _projection_weights(w):
    bounds = np.cumsum([0, A_WIDTH, A_KV_WIDTH, A_KV_WIDTH])
    q_a, k_a, v_a, rest = (w[:, bounds[0]:bounds[1]], w[:, bounds[1]:bounds[2]],
                           w[:, bounds[2]:bounds[3]], w[:, bounds[3]:])
    rep = lambda t: jnp.repeat(t.reshape(t.shape[0], A_KV_HEADS, HEAD_DIM), 2, axis=1).reshape(t.shape[0], -1)
    return jnp.concatenate([q_a, rep(k_a), rep(v_a), rest], axis=1).astype(jnp.bfloat16)


def kernel(x, norm_gain, w_in, q_norm_a, k_norm_a, sinks_a, q_norm_b, k_norm_b, w_out):
    b, seq, d_model = x.shape
    tokens = b * seq
    depth = norm_gain.shape[0]
    assert [d for _, d in B_PATTERNS] == [1, MID_DIL, CLASSES]
    assert seq % PROJ_ROWS == 0 and PROJ_ROWS % LOCAL == 0 and (seq // CLASSES) % BLOCK == 0
    scale = HEAD_DIM ** -0.5 * LOG2E
    cos_t, sin_t = _rope_tables(seq)
    seg_i = np.arange(SEG_TILE) // HEAD_DIM
    seg = jnp.asarray(seg_i[:, None] == seg_i[None, :], jnp.bfloat16)
    r = np.arange(LOCAL)
    perm_np = (CLASSES * (r % BF16_ROWS) + r // BF16_ROWS)[:, None] == r[None, :]
    perm = jnp.asarray(perm_np, jnp.bfloat16)
    permt = jnp.asarray(perm_np.T, jnp.bfloat16)
    expand = jnp.asarray(np.arange(LANES)[:, None] == (np.arange(B_WIDTH) // HEAD_DIM)[None, :], jnp.bfloat16)

    for i in range(depth):
        x2d = x.reshape(tokens, d_model)
        tile = lambda g, n, s=1.0: jnp.tile(g * s, n)[None, :]
        qa, ka, va, ga, qb, kb, vb, gb, qc, kc, vc = _proj_call(
            x2d, norm_gain[i][None, :], _projection_weights(w_in[i]), seg, perm, cos_t, sin_t,
            tile(q_norm_a[i], A_HEADS, scale), tile(k_norm_a[i], 2 * A_KV_HEADS),
            tile(q_norm_b[i], B_HEADS, scale), tile(k_norm_b[i], B_HEADS), b, seq)

        r3 = lambda t: t.reshape(b, seq, t.shape[-1])
        (mixed_a,) = _attn_call(r3(qa), r3(ka), r3(va), A_WINDOW - 1,
                                sinks=sinks_a[i], gate=r3(ga), name="attn_a")

        (w1, d1), (w4, d4), (w16, d16) = B_PATTERNS
        o1, s1 = _attn_call(r3(qb), r3(kb), r3(vb), w1 // d1, want_stats=True, name="attn_b1")
        o4, s4 = _attn_mid_call(qc, kc, vc, w4 // d4, name="attn_b4")
        fold = lambda t: t.reshape(b * CLASSES, seq // CLASSES, t.shape[-1])
        o16, s16 = _attn_call(fold(qc), fold(kc), fold(vc), w16 // d16, want_stats=True, name="attn_b16")
        unfold = lambda t: t.reshape(b, CLASSES, seq // CLASSES, t.shape[-1])

        out = _out_call(x2d, mixed_a.reshape(tokens, A_WIDTH),
                        o1.reshape(tokens, B_WIDTH), s1.reshape(tokens, LANES),
                        o4, s4, unfold(o16), unfold(s16), gb, permt, expand,
                        w_out[i].astype(jnp.bfloat16), seq)
        x = out.reshape(b, seq, d_model)
    return x
```

```python
import functools

import numpy as np
import jax
import jax.numpy as jnp
from jax import lax
from jax.experimental import pallas as pl
from jax.experimental.pallas import tpu as pltpu

HEAD_DIM = 64
HALF = HEAD_DIM // 2
A_HEADS = 8
A_KV_HEADS = 2
A_WINDOW = 128
B_HEADS = 8
B_PATTERNS = ((128, 1), (512, 4), (2048, 16))
BLOCK = 128
ROPE_THETA = 10000.0
EPS = 1e-6
NEG = -1e30
LOG2E = 1.4426950408889634

A_WIDTH = A_HEADS * HEAD_DIM
A_KV_WIDTH = A_KV_HEADS * HEAD_DIM
B_WIDTH = B_HEADS * HEAD_DIM

LANES = 128
BF16_ROWS = 16
SEG_TILE = 256
PROJ_ROWS = 512
VMEM_LIMIT = 56 * 1024 * 1024

A_KV_PAIR_WIDTH = 2 * A_KV_HEADS * HEAD_DIM
assert A_KV_WIDTH == LANES

CLASSES = max(d for _, d in B_PATTERNS)
LOCAL = CLASSES * BF16_ROWS
MID_DIL = 4
MID_RUN = BLOCK // MID_DIL
STAT_LANES = 3 * B_HEADS


def _dot(a, b):
    return jnp.dot(a, b, preferred_element_type=jnp.float32)


def _split3(v):
    hi = v.astype(jnp.bfloat16).astype(jnp.float32)
    r1 = v - hi
    mid = r1.astype(jnp.bfloat16).astype(jnp.float32)
    lo = (r1 - mid).astype(jnp.bfloat16).astype(jnp.float32)
    return hi, mid, lo


def _head_norm_rope(p, seg, gain, cos, sin, hi_mask):
    width = p.shape[-1]
    sq = (p * p).astype(jnp.bfloat16)
    step = min(width, SEG_TILE)
    sums = []
    for c in range(0, width, step):
        sums.append(_dot(sq[:, c:c + step], seg[:step, :step]))
    ss = sums[0] if len(sums) == 1 else jnp.concatenate(sums, axis=-1)
    y = p * lax.rsqrt(ss * (1.0 / HEAD_DIM) + EPS) * gain
    outs = []
    for c in range(0, width, LANES):
        yc = y[:, c:c + LANES]
        partner = jnp.where(hi_mask, pltpu.roll(yc, HALF, 1), pltpu.roll(yc, LANES - HALF, 1))
        outs.append(yc * cos + partner * sin)
    return outs[0] if len(outs) == 1 else jnp.concatenate(outs, axis=-1)


def _proj_kernel(x_ref, gain_ref, w_ref, seg_ref, perm_ref, cos_ref, sin_ref,
                 gqa_ref, gka_ref, gqb_ref, gkb_ref,
                 qa_ref, ka_ref, va_ref, ga_ref, qb_ref, kb_ref, vb_ref, gb_ref,
                 qc_ref, kc_ref, vc_ref):
    xf = x_ref[...]
    ms = jnp.mean(xf * xf, axis=-1, keepdims=True)
    h = (xf * lax.rsqrt(ms + EPS) * gain_ref[...]).astype(jnp.bfloat16)
    seg = seg_ref[...]
    perm = perm_ref[...]
    cos = cos_ref[...]
    sin = sin_ref[...]
    lane = lax.broadcasted_iota(jnp.int32, (1, LANES), 1)
    hi_mask = (lane & HALF) != 0

    def proj(lo, width):
        return _dot(h, w_ref[:, lo:lo + width])

    def store_both(nat_ref, cm_ref, y):
        nat_ref[...] = y
        for j in range(y.shape[0] // LOCAL):
            z = _dot(perm, y[j * LOCAL:(j + 1) * LOCAL, :]).astype(y.dtype)
            for c in range(CLASSES):
                cm_ref[c, j * BF16_ROWS:(j + 1) * BF16_ROWS, :] = z[c * BF16_ROWS:(c + 1) * BF16_ROWS, :]

    bf = jnp.bfloat16
    o = 0
    qa_ref[...] = _head_norm_rope(proj(o, A_WIDTH), seg, gqa_ref[...], cos, sin, hi_mask).astype(bf)
    o += A_WIDTH
    low = lane < HEAD_DIM

    def per_pair(y):
        swapped = pltpu.roll(y, HEAD_DIM, 1)
        return jnp.concatenate([jnp.where(low, y, swapped), jnp.where(low, swapped, y)], axis=-1)

    ka_ref[...] = per_pair(_head_norm_rope(proj(o, A_KV_WIDTH), seg, gka_ref[...], cos, sin, hi_mask)).astype(bf)
    o += A_KV_WIDTH
    va_ref[...] = per_pair(proj(o, A_KV_WIDTH)).astype(bf)
    o += A_KV_WIDTH
    ga_ref[...] = proj(o, A_WIDTH)
    o += A_WIDTH
    store_both(qb_ref, qc_ref, _head_norm_rope(proj(o, B_WIDTH), seg, gqb_ref[...], cos, sin, hi_mask).astype(bf))
    o += B_WIDTH
    store_both(kb_ref, kc_ref, _head_norm_rope(proj(o, B_WIDTH), seg, gkb_ref[...], cos, sin, hi_mask).astype(bf))
    o += B_WIDTH
    store_both(vb_ref, vc_ref, proj(o, B_WIDTH).astype(bf))
    o += B_WIDTH
    gb_ref[...] = proj(o, B_WIDTH)


def _proj_call(x2d, gain, w_bf, seg, perm, cos_t, sin_t, gqa, gka, gqb, gkb, batch, seq):
    tokens, d_model = x2d.shape
    tm = PROJ_ROWS
    steps_per_seq = seq // tm
    row = lambda w: pl.BlockSpec((tm, w), lambda g: (g, 0))
    full = lambda a: pl.BlockSpec(a.shape, lambda g: (0,) * a.ndim)
    table = pl.BlockSpec((tm, LANES), lambda g: (g % steps_per_seq, 0))
    cmaj = pl.BlockSpec((None, CLASSES, tm // CLASSES, B_WIDTH),
                        lambda g: (g // steps_per_seq, 0, g % steps_per_seq, 0))
    bf = jnp.bfloat16
    cm_shape = jax.ShapeDtypeStruct((batch, CLASSES, seq // CLASSES, B_WIDTH), bf)
    out_shape = [
        jax.ShapeDtypeStruct((tokens, A_WIDTH), bf),
        jax.ShapeDtypeStruct((tokens, A_KV_PAIR_WIDTH), bf),
        jax.ShapeDtypeStruct((tokens, A_KV_PAIR_WIDTH), bf),
        jax.ShapeDtypeStruct((tokens, A_WIDTH), jnp.float32),
        jax.ShapeDtypeStruct((tokens, B_WIDTH), bf),
        jax.ShapeDtypeStruct((tokens, B_WIDTH), bf),
        jax.ShapeDtypeStruct((tokens, B_WIDTH), bf),
        jax.ShapeDtypeStruct((tokens, B_WIDTH), jnp.float32),
        cm_shape, cm_shape, cm_shape,
    ]
    return pl.pallas_call(
        _proj_kernel,
        grid=(tokens // tm,),
        in_specs=[row(d_model), full(gain), full(w_bf), full(seg), full(perm), table, table,
                  full(gqa), full(gka), full(gqb), full(gkb)],
        out_specs=[row(s.shape[1]) for s in out_shape[:8]] + [cmaj] * 3,
        out_shape=out_shape,
        compiler_params=pltpu.CompilerParams(
            dimension_semantics=("parallel",), vmem_limit_bytes=VMEM_LIMIT),
        name="proj",
    )(x2d, gain, w_bf, seg, perm, cos_t, sin_t, gqa, gka, gqb, gkb)


ONES_ROWS = BF16_ROWS
ATTN_BLOCKS = 8


def _attn_kernel(*refs, pairs, slabs, group, nblk, strided, has_sinks, has_gate, want_stats):
    it = iter(refs)
    if strided:
        q_ref, kc_ref, vc_ref, bias_ref = (next(it) for _ in range(4))
        kp_ref = vp_ref = None
    else:
        q_ref, kc_ref, kp_ref, vc_ref, vp_ref, bias_ref = (next(it) for _ in range(6))
    sink_ref = next(it) if has_sinks else None
    gate_ref = next(it) if has_gate else None
    o_ref = next(it)
    st_ref = next(it) if want_stats else None
    vtbuf, sbuf, mbuf = (next(it) for _ in range(3))
    lsebuf = next(it) if want_stats else None

    def block_of(ref, g, m, cols):
        if strided:
            return ref[:, m, :, cols].reshape(BLOCK, LANES)
        return ref[g, m * BLOCK:(m + 1) * BLOCK, cols]

    def key_block(cur_ref, prev_ref, g, kb, cols):
        if kb > 0:
            return block_of(cur_ref, g, kb - 1, cols)
        return block_of(cur_ref, g, 0, cols) if strided else prev_ref[g, :, cols]

    def store_block(ref, g, m, cols, val):
        if strided:
            ref[:, m, :, cols] = val.reshape(BLOCK // MID_RUN, MID_RUN, LANES)
        else:
            ref[g, m * BLOCK:(m + 1) * BLOCK, cols] = val

    if strided:
        first_bias = bias_ref[1]
    else:
        first_bias = bias_ref[jnp.where(pl.program_id(1) == 0, 1, 0)]

    ones = jnp.ones((ONES_ROWS, BLOCK), vtbuf.dtype)
    for sl in range(slabs):
        src = slice(sl * LANES, (sl + 1) * LANES)
        for g in range(group):
            for kb in range(nblk + 1):
                vtbuf[sl, g * (nblk + 1) + kb, 0:LANES, :] = key_block(vc_ref, vp_ref, g, kb, src).T
                vtbuf[sl, g * (nblk + 1) + kb, LANES:, :] = ones

    lane = lax.broadcasted_iota(jnp.int32, (1, LANES), 1)
    low = lane < HEAD_DIM
    zero = jnp.zeros((), q_ref.dtype)

    def sink_row(hp):
        return jnp.concatenate([jnp.full((1, LANES), sink_ref[2 * hp] * LOG2E, jnp.float32),
                                jnp.full((1, LANES), sink_ref[2 * hp + 1] * LOG2E, jnp.float32)], axis=1)

    def score_stage(t):
        slot, (g, j) = t % 2, divmod(t, nblk)
        bias_t = first_bias if j == 0 else bias_ref[0]
        for hp in range(pairs):
            sl = hp * slabs // pairs
            kcols = slice(sl * LANES, (sl + 1) * LANES)
            qp = block_of(q_ref, g, j, slice(hp * LANES, (hp + 1) * LANES))
            qm = jnp.concatenate([jnp.where(low, qp, zero), jnp.where(low, zero, qp)], axis=0)
            kw = jnp.concatenate([key_block(kc_ref, kp_ref, g, j, kcols),
                                  key_block(kc_ref, kp_ref, g, j + 1, kcols)], axis=0)
            st = lax.dot_general(kw, qm, (((1,), (1,)), ((), ())),
                                 preferred_element_type=jnp.float32) + bias_t
            m = jnp.max(st, axis=0, keepdims=True)
            if has_sinks:
                m = jnp.maximum(m, sink_row(hp))
            sbuf[slot, hp] = st
            mbuf[slot, hp] = m

    def value_stage(t):
        slot, (g, j) = t % 2, divmod(t, nblk)
        kb0 = g * (nblk + 1) + j
        for hp in range(pairs):
            sl = hp * slabs // pairs
            cols = slice(hp * LANES, (hp + 1) * LANES)
            m = mbuf[slot, hp]
            pt = jnp.exp2(sbuf[slot, hp] - m).astype(jnp.bfloat16)
            vt = jnp.concatenate([vtbuf[sl, kb0], vtbuf[sl, kb0 + 1]], axis=1)
            r = _dot(vt, pt)
            l = r[LANES:LANES + 1, :]
            if has_sinks:
                l = l + jnp.exp2(sink_row(hp) - m)
            inv = 1.0 / l
            ot = jnp.concatenate([r[0:HEAD_DIM, 0:LANES] * inv[:, 0:LANES],
                                  r[HEAD_DIM:LANES, LANES:] * inv[:, LANES:]], axis=0)
            o = ot.T
            if has_gate:
                gt = block_of(gate_ref, g, j, cols)
                o = o * (gt * jax.nn.sigmoid(gt))
            store_block(o_ref, g, j, cols, o.astype(o_ref.dtype))
            if want_stats:
                lse = m + jnp.log2(l)
                lsebuf[slot, 2 * hp:2 * hp + 1, :] = lse[:, 0:LANES]
                lsebuf[slot, 2 * hp + 1:2 * hp + 2, :] = lse[:, LANES:]
        if want_stats:
            terms = _split3(lsebuf[slot])
            pad = jnp.zeros((LANES - STAT_LANES, BLOCK), jnp.float32)
            tile = jnp.concatenate(list(terms) + [pad], axis=0)
            store_block(st_ref, g, j, slice(0, LANES), tile.T.astype(st_ref.dtype))

    jobs = group * nblk
    score_stage(0)
    for t in range(jobs):
        if t + 1 < jobs:
            score_stage(t + 1)
        value_stage(t)


def _band_bias(max_dist, mid):
    r = np.arange(BLOCK)
    pos = MID_DIL * (r % MID_RUN) + r // MID_RUN if mid else r
    qpos = pos[None, :] + BLOCK
    kpos = np.concatenate([pos, pos + BLOCK])[:, None]
    dist = qpos - kpos
    valid = (dist >= 0) & (dist <= max_dist)
    first = valid & (np.arange(2 * BLOCK)[:, None] >= BLOCK)
    both = np.stack([valid, first])
    return jnp.asarray(np.where(np.concatenate([both, both], axis=-1), 0.0, NEG), jnp.float32)


def _attn_scratch(pairs, slabs, key_blocks, want_stats):
    scratch = [pltpu.VMEM((slabs, key_blocks, LANES + ONES_ROWS, BLOCK), jnp.bfloat16),
               pltpu.VMEM((2, pairs, 2 * BLOCK, 2 * LANES), jnp.float32),
               pltpu.VMEM((2, pairs, 1, 2 * LANES), jnp.float32)]
    if want_stats:
        scratch.append(pltpu.VMEM((2, 2 * pairs, BLOCK), jnp.float32))
    return scratch


def _attn_call(q, k, v, max_dist, *, sinks=None, gate=None, want_stats=False, name):
    n, seq, qw = q.shape
    kvw = k.shape[-1]
    pairs, slabs = qw // LANES, kvw // LANES
    tq = min(seq, ATTN_BLOCKS * BLOCK)
    sub = tq // BLOCK
    group = ATTN_BLOCKS // sub
    cur = lambda w: pl.BlockSpec((group, tq, w), lambda b, i: (b, i, 0))
    prev = lambda w: pl.BlockSpec((group, BLOCK, w), lambda b, i: (b, jnp.maximum(i * sub - 1, 0), 0))
    bias = _band_bias(max_dist, mid=False)
    args = [q, k, k, v, v, bias]
    in_specs = [cur(qw), cur(kvw), prev(kvw), cur(kvw), prev(kvw),
                pl.BlockSpec(bias.shape, lambda b, i: (0, 0, 0))]
    if sinks is not None:
        args.append(sinks)
        in_specs.append(pl.BlockSpec(memory_space=pltpu.SMEM))
    if gate is not None:
        args.append(gate)
        in_specs.append(cur(qw))
    out_shape = [jax.ShapeDtypeStruct((n, seq, qw), jnp.bfloat16)]
    out_specs = [cur(qw)]
    if want_stats:
        out_shape.append(jax.ShapeDtypeStruct((n, seq, LANES), jnp.bfloat16))
        out_specs.append(cur(LANES))
    kernel = functools.partial(_attn_kernel, pairs=pairs, slabs=slabs, group=group, nblk=sub,
                               strided=False, has_sinks=sinks is not None, has_gate=gate is not None,
                               want_stats=want_stats)
    return pl.pallas_call(
        kernel,
        grid=(n // group, seq // tq),
        in_specs=in_specs,
        out_specs=out_specs,
        out_shape=out_shape,
        scratch_shapes=_attn_scratch(pairs, slabs, group * (sub + 1), want_stats),
        compiler_params=pltpu.CompilerParams(
            dimension_semantics=("parallel", "arbitrary"), vmem_limit_bytes=VMEM_LIMIT),
        name=name,
    )(*args)


def _attn_mid_call(q, k, v, max_dist, *, name):
    b, classes, per_class, w = q.shape
    outer = classes // MID_DIL
    nblk = per_class // MID_RUN
    view = lambda t: t.reshape(b, outer, MID_DIL, nblk, MID_RUN, t.shape[-1])
    spec = lambda width: pl.BlockSpec((None, outer, None, nblk, MID_RUN, width),
                                      lambda i, e: (i, 0, e, 0, 0, 0))
    pairs = w // LANES
    bias = _band_bias(max_dist, mid=True)
    shape6 = (b, outer, MID_DIL, nblk, MID_RUN)
    kernel = functools.partial(_attn_kernel, pairs=pairs, slabs=pairs, group=1, nblk=nblk, strided=True,
                               has_sinks=False, has_gate=False, want_stats=True)
    o, st = pl.pallas_call(
        kernel,
        grid=(b, MID_DIL),
        in_specs=[spec(w), spec(w), spec(w), pl.BlockSpec(bias.shape, lambda i, e: (0, 0, 0))],
        out_specs=[spec(w), spec(LANES)],
        out_shape=[jax.ShapeDtypeStruct(shape6 + (w,), jnp.bfloat16),
                   jax.ShapeDtypeStruct(shape6 + (LANES,), jnp.bfloat16)],
        scratch_shapes=_attn_scratch(pairs, pairs, nblk + 1, True),
        compiler_params=pltpu.CompilerParams(
            dimension_semantics=("parallel", "arbitrary"), vmem_limit_bytes=VMEM_LIMIT),
        name=name,
    )(view(q), view(k), view(v), bias)
    return o.reshape(b, classes, per_class, w), st.reshape(b, classes, per_class, LANES)


def _expand_heads(w, expand):
    hi, mid, lo = _split3(w)
    packed = hi + pltpu.roll(mid, B_HEADS, 1) + pltpu.roll(lo, 2 * B_HEADS, 1)
    return _dot(packed.astype(jnp.bfloat16), expand)


def _stat_sum(st):
    return (st + pltpu.roll(st, LANES - B_HEADS, 1) + pltpu.roll(st, LANES - 2 * B_HEADS, 1))


def _out_kernel(x_ref, ma_ref, o1_ref, s1_ref, o4_ref, s4_ref, o16_ref, s16_ref, gb_ref,
                permt_ref, expand_ref, w_ref, out_ref):
    rows = x_ref.shape[0]
    permt = permt_ref[...]

    def to_token_order(o_ref, s_ref):
        outs, stats = [], []
        for j in range(rows // LOCAL):
            sel = slice(j * BF16_ROWS, (j + 1) * BF16_ROWS)
            y = jnp.concatenate([jnp.concatenate([o_ref[c, sel, :] for c in range(CLASSES)], axis=0),
                                 jnp.concatenate([s_ref[c, sel, :] for c in range(CLASSES)], axis=0)],
                                axis=1)
            u = _dot(permt, y)
            outs.append(u[:, :B_WIDTH])
            stats.append(u[:, B_WIDTH:])
        return jnp.concatenate(outs, axis=0), jnp.concatenate(stats, axis=0)

    o1 = o1_ref[...].astype(jnp.float32)
    l1 = _stat_sum(s1_ref[...].astype(jnp.float32))
    o4, s4 = to_token_order(o4_ref, s4_ref)
    o16, s16 = to_token_order(o16_ref, s16_ref)
    l4, l16 = _stat_sum(s4), _stat_sum(s16)

    top = jnp.maximum(jnp.maximum(l1, l4), l16)
    e1, e4, e16 = jnp.exp2(l1 - top), jnp.exp2(l4 - top), jnp.exp2(l16 - top)
    head_lane = lax.broadcasted_iota(jnp.int32, (1, LANES), 1) < B_HEADS
    inv = jnp.where(head_lane, 1.0 / (e1 + e4 + e16), 0.0)
    keep = lambda e: jnp.where(head_lane, e, 0.0) * inv
    expand = expand_ref[...]
    ob = o16 + _expand_heads(keep(e1), expand) * (o1 - o16) + _expand_heads(keep(e4), expand) * (o4 - o16)
    gb = gb_ref[...]
    mb = (ob * (gb * jax.nn.sigmoid(gb))).astype(jnp.bfloat16)
    out_ref[...] = (x_ref[...] + _dot(ma_ref[...], w_ref[0:A_WIDTH, :])
                    + _dot(mb, w_ref[A_WIDTH:, :]))


def _out_call(x2d, mixed_a, o1, s1, o4, s4, o16, s16, gate_b, permt, expand, w_bf, seq):
    tokens, d_model = x2d.shape
    tm = PROJ_ROWS
    steps_per_seq = seq // tm
    row = lambda w: pl.BlockSpec((tm, w), lambda g: (g, 0))
    full = lambda a: pl.BlockSpec(a.shape, lambda g: (0,) * a.ndim)
    cmaj = lambda w: pl.BlockSpec((None, CLASSES, tm // CLASSES, w),
                                  lambda g: (g // steps_per_seq, 0, g % steps_per_seq, 0))
    return pl.pallas_call(
        _out_kernel,
        grid=(tokens // tm,),
        in_specs=[row(d_model), row(A_WIDTH), row(B_WIDTH), row(LANES),
                  cmaj(B_WIDTH), cmaj(LANES), cmaj(B_WIDTH), cmaj(LANES),
                  row(B_WIDTH), full(permt), full(expand), full(w_bf)],
        out_specs=row(d_model),
        out_shape=jax.ShapeDtypeStruct((tokens, d_model), jnp.float32),
        compiler_params=pltpu.CompilerParams(
            dimension_semantics=("parallel",), vmem_limit_bytes=VMEM_LIMIT),
        name="merge_out",
    )(x2d, mixed_a, o1, s1, o4, s4, o16, s16, gate_b, permt, expand, w_bf)


def _rope_tables(seq):
    inv = ROPE_THETA ** (-jnp.arange(HALF, dtype=jnp.float32) / HALF)
    ang = jnp.arange(seq).astype(jnp.float32)[:, None] * inv[None, :]
    reps = LANES // HALF
    cos_t = jnp.tile(jnp.cos(ang), (1, reps))
    sign = jnp.tile(jnp.concatenate([-jnp.ones((HALF,), jnp.float32), jnp.ones((HALF,), jnp.float32)]),
                    LANES // HEAD_DIM)
    sin_t = jnp.tile(jnp.sin(ang), (1, reps)) * sign[None, :]
    return cos_t, sin_t


def kernel(x, norm_gain, w_in, q_norm_a, k_norm_a, sinks_a, q_norm_b, k_norm_b, w_out):
    b, seq, d_model = x.shape
    tokens = b * seq
    depth = norm_gain.shape[0]
    assert [d for _, d in B_PATTERNS] == [1, MID_DIL, CLASSES]
    assert seq % PROJ_ROWS == 0 and PROJ_ROWS % LOCAL == 0 and (seq // CLASSES) % BLOCK == 0
    scale = HEAD_DIM ** -0.5 * LOG2E
    cos_t, sin_t = _rope_tables(seq)
    seg_i = np.arange(SEG_TILE) // HEAD_DIM
    seg = jnp.asarray(seg_i[:, None] == seg_i[None, :], jnp.bfloat16)
    r = np.arange(LOCAL)
    perm_np = (CLASSES * (r % BF16_ROWS) + r // BF16_ROWS)[:, None] == r[None, :]
    perm = jnp.asarray(perm_np, jnp.bfloat16)
    permt = jnp.asarray(perm_np.T, jnp.bfloat16)
    term_head = np.where(np.arange(LANES) < STAT_LANES, np.arange(LANES) % B_HEADS, -1)
    expand = jnp.asarray(term_head[:, None] == (np.arange(B_WIDTH) // HEAD_DIM)[None, :], jnp.bfloat16)

    for i in range(depth):
        x2d = x.reshape(tokens, d_model)
        tile = lambda g, n, s=1.0: jnp.tile(g * s, n)[None, :]
        qa, ka, va, ga, qb, kb, vb, gb, qc, kc, vc = _proj_call(
            x2d, norm_gain[i][None, :], w_in[i].astype(jnp.bfloat16), seg, perm, cos_t, sin_t,
            tile(q_norm_a[i], A_HEADS, scale), tile(k_norm_a[i], A_KV_HEADS),
            tile(q_norm_b[i], B_HEADS, scale), tile(k_norm_b[i], B_HEADS), b, seq)

        r3 = lambda t: t.reshape(b, seq, t.shape[-1])
        (mixed_a,) = _attn_call(r3(qa), r3(ka), r3(va), A_WINDOW - 1,
                                sinks=sinks_a[i], gate=r3(ga), name="attn_a")

        (w1, d1), (w4, d4), (w16, d16) = B_PATTERNS
        o1, s1 = _attn_call(r3(qb), r3(kb), r3(vb), w1 // d1, want_stats=True, name="attn_b1")
        o4, s4 = _attn_mid_call(qc, kc, vc, w4 // d4, name="attn_b4")
        fold = lambda t: t.reshape(b * CLASSES, seq // CLASSES, t.shape[-1])
        o16, s16 = _attn_call(fold(qc), fold(kc), fold(vc), w16 // d16, want_stats=True, name="attn_b16")
        unfold = lambda t: t.reshape(b, CLASSES, seq // CLASSES, t.shape[-1])

        out = _out_call(x2d, mixed_a.reshape(tokens, A_WIDTH),
                        o1.reshape(tokens, B_WIDTH), s1.reshape(tokens, LANES),
                        o4, s4, unfold(o16), unfold(s16), gb, permt, expand,
                        w_out[i].astype(jnp.bfloat16), seq)
        x = out.reshape(b, seq, d_model)
    return x
```

```python
import functools

import numpy as np
import jax
import jax.numpy as jnp
from jax import lax
from jax.experimental import pallas as pl
from jax.experimental.pallas import tpu as pltpu

HEAD_DIM = 64
HALF = HEAD_DIM // 2
A_HEADS = 8
A_KV_HEADS = 2
A_WINDOW = 128
B_HEADS = 8
B_PATTERNS = ((128, 1), (512, 4), (2048, 16))
BLOCK = 128
ROPE_THETA = 10000.0
EPS = 1e-6
NEG = -1e30
LOG2E = 1.4426950408889634

A_WIDTH = A_HEADS * HEAD_DIM
A_KV_WIDTH = A_KV_HEADS * HEAD_DIM
B_WIDTH = B_HEADS * HEAD_DIM

LANES = 128
BF16_ROWS = 16
SEG_TILE = 256
PROJ_ROWS = 512
VMEM_LIMIT = 56 * 1024 * 1024

A_KV_PAIR_WIDTH = 2 * A_KV_HEADS * HEAD_DIM
assert A_KV_WIDTH == LANES

CLASSES = max(d for _, d in B_PATTERNS)
LOCAL = CLASSES * BF16_ROWS
MID_DIL = 4
MID_RUN = BLOCK // MID_DIL
STAT_LANES = 3 * B_HEADS


def _dot(a, b):
    return jnp.dot(a, b, preferred_element_type=jnp.float32)


def _split3(v):
    hi = v.astype(jnp.bfloat16).astype(jnp.float32)
    r1 = v - hi
    mid = r1.astype(jnp.bfloat16).astype(jnp.float32)
    lo = (r1 - mid).astype(jnp.bfloat16).astype(jnp.float32)
    return hi, mid, lo


def _head_norm_rope(p, seg, gain, cos, sin, hi_mask):
    width = p.shape[-1]
    sq = (p * p).astype(jnp.bfloat16)
    step = min(width, SEG_TILE)
    sums = []
    for c in range(0, width, step):
        sums.append(_dot(sq[:, c:c + step], seg[:step, :step]))
    ss = sums[0] if len(sums) == 1 else jnp.concatenate(sums, axis=-1)
    y = p * lax.rsqrt(ss * (1.0 / HEAD_DIM) + EPS) * gain
    outs = []
    for c in range(0, width, LANES):
        yc = y[:, c:c + LANES]
        partner = jnp.where(hi_mask, pltpu.roll(yc, HALF, 1), pltpu.roll(yc, LANES - HALF, 1))
        outs.append(yc * cos + partner * sin)
    return outs[0] if len(outs) == 1 else jnp.concatenate(outs, axis=-1)


def _proj_kernel(x_ref, gain_ref, w_ref, seg_ref, perm_ref, cos_ref, sin_ref,
                 gqa_ref, gka_ref, gqb_ref, gkb_ref,
                 qa_ref, ka_ref, va_ref, ga_ref, qb_ref, kb_ref, vb_ref, gb_ref,
                 qc_ref, kc_ref, vc_ref):
    xf = x_ref[...]
    ms = jnp.mean(xf * xf, axis=-1, keepdims=True)
    h = (xf * lax.rsqrt(ms + EPS) * gain_ref[...]).astype(jnp.bfloat16)
    seg = seg_ref[...]
    perm = perm_ref[...]
    cos = cos_ref[...]
    sin = sin_ref[...]
    lane = lax.broadcasted_iota(jnp.int32, (1, LANES), 1)
    hi_mask = (lane & HALF) != 0

    def proj(lo, width):
        return _dot(h, w_ref[:, lo:lo + width])

    def store_both(nat_ref, cm_ref, y):
        nat_ref[...] = y
        for j in range(y.shape[0] // LOCAL):
            z = _dot(perm, y[j * LOCAL:(j + 1) * LOCAL, :]).astype(y.dtype)
            for c in range(CLASSES):
                cm_ref[c, j * BF16_ROWS:(j + 1) * BF16_ROWS, :] = z[c * BF16_ROWS:(c + 1) * BF16_ROWS, :]

    bf = jnp.bfloat16
    o = 0
    qa_ref[...] = _head_norm_rope(proj(o, A_WIDTH), seg, gqa_ref[...], cos, sin, hi_mask).astype(bf)
    o += A_WIDTH
    low = lane < HEAD_DIM

    def per_pair(y):
        swapped = pltpu.roll(y, HEAD_DIM, 1)
        return jnp.concatenate([jnp.where(low, y, swapped), jnp.where(low, swapped, y)], axis=-1)

    ka_ref[...] = per_pair(_head_norm_rope(proj(o, A_KV_WIDTH), seg, gka_ref[...], cos, sin, hi_mask)).astype(bf)
    o += A_KV_WIDTH
    va_ref[...] = per_pair(proj(o, A_KV_WIDTH)).astype(bf)
    o += A_KV_WIDTH
    ga_ref[...] = proj(o, A_WIDTH)
    o += A_WIDTH
    store_both(qb_ref, qc_ref, _head_norm_rope(proj(o, B_WIDTH), seg, gqb_ref[...], cos, sin, hi_mask).astype(bf))
    o += B_WIDTH
    store_both(kb_ref, kc_ref, _head_norm_rope(proj(o, B_WIDTH), seg, gkb_ref[...], cos, sin, hi_mask).astype(bf))
    o += B_WIDTH
    store_both(vb_ref, vc_ref, proj(o, B_WIDTH).astype(bf))
    o += B_WIDTH
    gb_ref[...] = proj(o, B_WIDTH)


def _proj_call(x2d, gain, w_bf, seg, perm, cos_t, sin_t, gqa, gka, gqb, gkb, batch, seq):
    tokens, d_model = x2d.shape
    tm = PROJ_ROWS
    steps_per_seq = seq // tm
    row = lambda w: pl.BlockSpec((tm, w), lambda g: (g, 0))
    full = lambda a: pl.BlockSpec(a.shape, lambda g: (0,) * a.ndim)
    table = pl.BlockSpec((tm, LANES), lambda g: (g % steps_per_seq, 0))
    cmaj = pl.BlockSpec((None, CLASSES, tm // CLASSES, B_WIDTH),
                        lambda g: (g // steps_per_seq, 0, g % steps_per_seq, 0))
    bf = jnp.bfloat16
    cm_shape = jax.ShapeDtypeStruct((batch, CLASSES, seq // CLASSES, B_WIDTH), bf)
    out_shape = [
        jax.ShapeDtypeStruct((tokens, A_WIDTH), bf),
        jax.ShapeDtypeStruct((tokens, A_KV_PAIR_WIDTH), bf),
        jax.ShapeDtypeStruct((tokens, A_KV_PAIR_WIDTH), bf),
        jax.ShapeDtypeStruct((tokens, A_WIDTH), jnp.float32),
        jax.ShapeDtypeStruct((tokens, B_WIDTH), bf),
        jax.ShapeDtypeStruct((tokens, B_WIDTH), bf),
        jax.ShapeDtypeStruct((tokens, B_WIDTH), bf),
        jax.ShapeDtypeStruct((tokens, B_WIDTH), jnp.float32),
        cm_shape, cm_shape, cm_shape,
    ]
    return pl.pallas_call(
        _proj_kernel,
        grid=(tokens // tm,),
        in_specs=[row(d_model), full(gain), full(w_bf), full(seg), full(perm), table, table,
                  full(gqa), full(gka), full(gqb), full(gkb)],
        out_specs=[row(s.shape[1]) for s in out_shape[:8]] + [cmaj] * 3,
        out_shape=out_shape,
        compiler_params=pltpu.CompilerParams(
            dimension_semantics=("parallel",), vmem_limit_bytes=VMEM_LIMIT),
        name="proj",
    )(x2d, gain, w_bf, seg, perm, cos_t, sin_t, gqa, gka, gqb, gkb)


ONES_ROWS = BF16_ROWS
ATTN_BLOCKS = 16


def _attn_kernel(*refs, pairs, slabs, group, nblk, strided, has_sinks, has_gate, want_stats):
    it = iter(refs)
    if strided:
        q_ref, kc_ref, vc_ref, bias_ref = (next(it) for _ in range(4))
        kp_ref = vp_ref = None
    else:
        q_ref, kc_ref, kp_ref, vc_ref, vp_ref, bias_ref = (next(it) for _ in range(6))
    sink_ref = next(it) if has_sinks else None
    gate_ref = next(it) if has_gate else None
    o_ref = next(it)
    st_ref = next(it) if want_stats else None
    vtbuf, sbuf, mbuf = (next(it) for _ in range(3))
    lsebuf = next(it) if want_stats else None

    def block_of(ref, g, m, cols):
        if strided:
            return ref[:, g, m, :, cols].reshape(BLOCK, LANES)
        return ref[g, m * BLOCK:(m + 1) * BLOCK, cols]

    def key_block(cur_ref, prev_ref, g, kb, cols):
        if kb > 0:
            return block_of(cur_ref, g, kb - 1, cols)
        return block_of(cur_ref, g, 0, cols) if strided else prev_ref[g, :, cols]

    def store_block(ref, g, m, cols, val):
        if strided:
            ref[:, g, m, :, cols] = val.reshape(BLOCK // MID_RUN, MID_RUN, LANES)
        else:
            ref[g, m * BLOCK:(m + 1) * BLOCK, cols] = val

    if strided:
        first_bias = bias_ref[1]
    else:
        first_bias = bias_ref[jnp.where(pl.program_id(1) == 0, 1, 0)]

    ones = jnp.ones((ONES_ROWS, BLOCK), vtbuf.dtype)

    def transpose_values(g, kb):
        for sl in range(slabs):
            src = slice(sl * LANES, (sl + 1) * LANES)
            vtbuf[sl, g * (nblk + 1) + kb, 0:LANES, :] = key_block(vc_ref, vp_ref, g, kb, src).T
            vtbuf[sl, g * (nblk + 1) + kb, LANES:, :] = ones

    lane = lax.broadcasted_iota(jnp.int32, (1, LANES), 1)
    low = lane < HEAD_DIM
    zero = jnp.zeros((), q_ref.dtype)
    onehot = bias_ref[2]

    def sink_row(hp):
        return jnp.concatenate([jnp.full((1, LANES), sink_ref[2 * hp] * LOG2E, jnp.float32),
                                jnp.full((1, LANES), sink_ref[2 * hp + 1] * LOG2E, jnp.float32)], axis=1)

    def score_stage(t):
        slot, (g, j) = t % 2, divmod(t, nblk)
        bias_t = first_bias if j == 0 else bias_ref[0]
        if j == 0:
            transpose_values(g, 0)
        transpose_values(g, j + 1)
        for hp in range(pairs):
            sl = hp * slabs // pairs
            kcols = slice(sl * LANES, (sl + 1) * LANES)
            qp = block_of(q_ref, g, j, slice(hp * LANES, (hp + 1) * LANES))
            qm = jnp.concatenate([jnp.where(low, qp, zero), jnp.where(low, zero, qp)], axis=0)
            kw = jnp.concatenate([key_block(kc_ref, kp_ref, g, j, kcols),
                                  key_block(kc_ref, kp_ref, g, j + 1, kcols)], axis=0)
            st = lax.dot_general(jnp.concatenate([kw, bias_t], axis=1),
                                 jnp.concatenate([qm, onehot], axis=1), (((1,), (1,)), ((), ())),
                                 preferred_element_type=jnp.float32)
            m = jnp.max(st, axis=0, keepdims=True)
            if has_sinks:
                m = jnp.maximum(m, sink_row(hp))
            sbuf[slot, hp] = st
            mbuf[slot, hp] = m

    def value_stage(t):
        slot, (g, j) = t % 2, divmod(t, nblk)
        kb0 = g * (nblk + 1) + j
        for hp in range(pairs):
            sl = hp * slabs // pairs
            cols = slice(hp * LANES, (hp + 1) * LANES)
            m = mbuf[slot, hp]
            pt = jnp.exp2(sbuf[slot, hp] - m).astype(jnp.bfloat16)
            vt = jnp.concatenate([vtbuf[sl, kb0], vtbuf[sl, kb0 + 1]], axis=1)
            r = _dot(vt, pt)
            l = r[LANES:LANES + 1, :]
            if has_sinks:
                l = l + jnp.exp2(sink_row(hp) - m)
            inv = 1.0 / l
            ot = jnp.concatenate([r[0:HEAD_DIM, 0:LANES] * inv[:, 0:LANES],
                                  r[HEAD_DIM:LANES, LANES:] * inv[:, LANES:]], axis=0)
            o = ot.T
            if has_gate:
                gt = block_of(gate_ref, g, j, cols)
                o = o * (gt * jax.nn.sigmoid(gt))
            store_block(o_ref, g, j, cols, o.astype(o_ref.dtype))
            if want_stats:
                lse = m + jnp.log2(l)
                lsebuf[slot, 2 * hp:2 * hp + 1, :] = lse[:, 0:LANES]
                lsebuf[slot, 2 * hp + 1:2 * hp + 2, :] = lse[:, LANES:]
        if want_stats:
            terms = _split3(lsebuf[slot])
            pad = jnp.zeros((LANES - STAT_LANES, BLOCK), jnp.float32)
            tile = jnp.concatenate(list(terms) + [pad], axis=0)
            store_block(st_ref, g, j, slice(0, LANES), tile.T.astype(st_ref.dtype))

    jobs = group * nblk
    score_stage(0)
    for t in range(jobs):
        if t + 1 < jobs:
            score_stage(t + 1)
        value_stage(t)


def _band_bias(max_dist, mid):
    r = np.arange(BLOCK)
    pos = MID_DIL * (r % MID_RUN) + r // MID_RUN if mid else r
    qpos = pos[None, :] + BLOCK
    kpos = np.concatenate([pos, pos + BLOCK])[:, None]
    dist = qpos - kpos
    valid = (dist >= 0) & (dist <= max_dist)
    first = valid & (np.arange(2 * BLOCK)[:, None] >= BLOCK)
    masks = np.where(np.stack([valid, first]), 0.0, NEG)
    onehot = np.tile(np.eye(BLOCK), (2, 1))
    return jnp.asarray(np.concatenate([masks, onehot[None]]), jnp.bfloat16)


def _attn_scratch(pairs, slabs, key_blocks, want_stats):
    scratch = [pltpu.VMEM((slabs, key_blocks, LANES + ONES_ROWS, BLOCK), jnp.bfloat16),
               pltpu.VMEM((2, pairs, 2 * BLOCK, 2 * LANES), jnp.float32),
               pltpu.VMEM((2, pairs, 1, 2 * LANES), jnp.float32)]
    if want_stats:
        scratch.append(pltpu.VMEM((2, 2 * pairs, BLOCK), jnp.float32))
    return scratch


def _attn_call(q, k, v, max_dist, *, sinks=None, gate=None, want_stats=False, name):
    n, seq, qw = q.shape
    kvw = k.shape[-1]
    pairs, slabs = qw // LANES, kvw // LANES
    tq = min(seq, ATTN_BLOCKS * BLOCK)
    sub = tq // BLOCK
    group = ATTN_BLOCKS // sub
    cur = lambda w: pl.BlockSpec((group, tq, w), lambda b, i: (b, i, 0))
    prev = lambda w: pl.BlockSpec((group, BLOCK, w), lambda b, i: (b, jnp.maximum(i * sub - 1, 0), 0))
    bias = _band_bias(max_dist, mid=False)
    args = [q, k, k, v, v, bias]
    in_specs = [cur(qw), cur(kvw), prev(kvw), cur(kvw), prev(kvw),
                pl.BlockSpec(bias.shape, lambda b, i: (0, 0, 0))]
    if sinks is not None:
        args.append(sinks)
        in_specs.append(pl.BlockSpec(memory_space=pltpu.SMEM))
    if gate is not None:
        args.append(gate)
        in_specs.append(cur(qw))
    out_shape = [jax.ShapeDtypeStruct((n, seq, qw), jnp.bfloat16)]
    out_specs = [cur(qw)]
    if want_stats:
        out_shape.append(jax.ShapeDtypeStruct((n, seq, LANES), jnp.bfloat16))
        out_specs.append(cur(LANES))
    kernel = functools.partial(_attn_kernel, pairs=pairs, slabs=slabs, group=group, nblk=sub,
                               strided=False, has_sinks=sinks is not None, has_gate=gate is not None,
                               want_stats=want_stats)
    return pl.pallas_call(
        kernel,
        grid=(n // group, seq // tq),
        in_specs=in_specs,
        out_specs=out_specs,
        out_shape=out_shape,
        scratch_shapes=_attn_scratch(pairs, slabs, group * (sub + 1), want_stats),
        compiler_params=pltpu.CompilerParams(
            dimension_semantics=("parallel", "arbitrary"), vmem_limit_bytes=VMEM_LIMIT),
        name=name,
    )(*args)


def _attn_mid_call(q, k, v, max_dist, *, name):
    b, classes, per_class, w = q.shape
    outer = classes // MID_DIL
    nblk = per_class // MID_RUN
    group = max(1, min(MID_DIL, ATTN_BLOCKS // nblk))
    view = lambda t: t.reshape(b, outer, MID_DIL, nblk, MID_RUN, t.shape[-1])
    spec = lambda width: pl.BlockSpec((None, outer, group, nblk, MID_RUN, width),
                                      lambda i, e: (i, 0, e, 0, 0, 0))
    pairs = w // LANES
    bias = _band_bias(max_dist, mid=True)
    shape6 = (b, outer, MID_DIL, nblk, MID_RUN)
    kernel = functools.partial(_attn_kernel, pairs=pairs, slabs=pairs, group=group, nblk=nblk,
                               strided=True, has_sinks=False, has_gate=False, want_stats=True)
    o, st = pl.pallas_call(
        kernel,
        grid=(b, MID_DIL // group),
        in_specs=[spec(w), spec(w), spec(w), pl.BlockSpec(bias.shape, lambda i, e: (0, 0, 0))],
        out_specs=[spec(w), spec(LANES)],
        out_shape=[jax.ShapeDtypeStruct(shape6 + (w,), jnp.bfloat16),
                   jax.ShapeDtypeStruct(shape6 + (LANES,), jnp.bfloat16)],
        scratch_shapes=_attn_scratch(pairs, pairs, group * (nblk + 1), True),
        compiler_params=pltpu.CompilerParams(
            dimension_semantics=("parallel", "arbitrary"), vmem_limit_bytes=VMEM_LIMIT),
        name=name,
    )(view(q), view(k), view(v), bias)
    return o.reshape(b, classes, per_class, w), st.reshape(b, classes, per_class, LANES)


def _expand_heads(w, expand):
    hi, mid, lo = _split3(w)
    packed = hi + pltpu.roll(mid, B_HEADS, 1) + pltpu.roll(lo, 2 * B_HEADS, 1)
    return _dot(packed.astype(jnp.bfloat16), expand)


def _stat_sum(st):
    return (st + pltpu.roll(st, LANES - B_HEADS, 1) + pltpu.roll(st, LANES - 2 * B_HEADS, 1))


def _out_kernel(x_ref, ma_ref, o1_ref, s1_ref, o4_ref, s4_ref, o16_ref, s16_ref, gb_ref,
                permt_ref, expand_ref, w_ref, out_ref):
    rows = x_ref.shape[0]
    permt = permt_ref[...]

    def to_token_order(o_ref, s_ref):
        outs, stats = [], []
        for j in range(rows // LOCAL):
            sel = slice(j * BF16_ROWS, (j + 1) * BF16_ROWS)
            y = jnp.concatenate([jnp.concatenate([o_ref[c, sel, :] for c in range(CLASSES)], axis=0),
                                 jnp.concatenate([s_ref[c, sel, :] for c in range(CLASSES)], axis=0)],
                                axis=1)
            u = _dot(permt, y)
            outs.append(u[:, :B_WIDTH])
            stats.append(u[:, B_WIDTH:])
        return jnp.concatenate(outs, axis=0), jnp.concatenate(stats, axis=0)

    o1 = o1_ref[...].astype(jnp.float32)
    l1 = _stat_sum(s1_ref[...].astype(jnp.float32))
    o4, s4 = to_token_order(o4_ref, s4_ref)
    o16, s16 = to_token_order(o16_ref, s16_ref)
    l4, l16 = _stat_sum(s4), _stat_sum(s16)

    top = jnp.maximum(jnp.maximum(l1, l4), l16)
    e1, e4, e16 = jnp.exp2(l1 - top), jnp.exp2(l4 - top), jnp.exp2(l16 - top)
    head_lane = lax.broadcasted_iota(jnp.int32, (1, LANES), 1) < B_HEADS
    inv = jnp.where(head_lane, 1.0 / (e1 + e4 + e16), 0.0)
    keep = lambda e: jnp.where(head_lane, e, 0.0) * inv
    expand = expand_ref[...]
    ob = o16 + _expand_heads(keep(e1), expand) * (o1 - o16) + _expand_heads(keep(e4), expand) * (o4 - o16)
    gb = gb_ref[...]
    mb = (ob * (gb * jax.nn.sigmoid(gb))).astype(jnp.bfloat16)
    out_ref[...] = (x_ref[...] + _dot(ma_ref[...], w_ref[0:A_WIDTH, :])
                    + _dot(mb, w_ref[A_WIDTH:, :]))


def _out_call(x2d, mixed_a, o1, s1, o4, s4, o16, s16, gate_b, permt, expand, w_bf, seq):
    tokens, d_model = x2d.shape
    tm = PROJ_ROWS
    steps_per_seq = seq // tm
    row = lambda w: pl.BlockSpec((tm, w), lambda g: (g, 0))
    full = lambda a: pl.BlockSpec(a.shape, lambda g: (0,) * a.ndim)
    cmaj = lambda w: pl.BlockSpec((None, CLASSES, tm // CLASSES, w),
                                  lambda g: (g // steps_per_seq, 0, g % steps_per_seq, 0))
    return pl.pallas_call(
        _out_kernel,
        grid=(tokens // tm,),
        in_specs=[row(d_model), row(A_WIDTH), row(B_WIDTH), row(LANES),
                  cmaj(B_WIDTH), cmaj(LANES), cmaj(B_WIDTH), cmaj(LANES),
                  row(B_WIDTH), full(permt), full(expand), full(w_bf)],
        out_specs=row(d_model),
        out_shape=jax.ShapeDtypeStruct((tokens, d_model), jnp.float32),
        compiler_params=pltpu.CompilerParams(
            dimension_semantics=("parallel",), vmem_limit_bytes=VMEM_LIMIT),
        name="merge_out",
    )(x2d, mixed_a, o1, s1, o4, s4, o16, s16, gate_b, permt, expand, w_bf)


def _rope_tables(seq):
    inv = ROPE_THETA ** (-jnp.arange(HALF, dtype=jnp.float32) / HALF)
    ang = jnp.arange(seq).astype(jnp.float32)[:, None] * inv[None, :]
    reps = LANES // HALF
    cos_t = jnp.tile(jnp.cos(ang), (1, reps))
    sign = jnp.tile(jnp.concatenate([-jnp.ones((HALF,), jnp.float32), jnp.ones((HALF,), jnp.float32)]),
                    LANES // HEAD_DIM)
    sin_t = jnp.tile(jnp.sin(ang), (1, reps)) * sign[None, :]
    return cos_t, sin_t


def kernel(x, norm_gain, w_in, q_norm_a, k_norm_a, sinks_a, q_norm_b, k_norm_b, w_out):
    b, seq, d_model = x.shape
    tokens = b * seq
    depth = norm_gain.shape[0]
    assert [d for _, d in B_PATTERNS] == [1, MID_DIL, CLASSES]
    assert seq % PROJ_ROWS == 0 and PROJ_ROWS % LOCAL == 0 and (seq // CLASSES) % BLOCK == 0
    scale = HEAD_DIM ** -0.5 * LOG2E
    cos_t, sin_t = _rope_tables(seq)
    seg_i = np.arange(SEG_TILE) // HEAD_DIM
    seg = jnp.asarray(seg_i[:, None] == seg_i[None, :], jnp.bfloat16)
    r = np.arange(LOCAL)
    perm_np = (CLASSES * (r % BF16_ROWS) + r // BF16_ROWS)[:, None] == r[None, :]
    perm = jnp.asarray(perm_np, jnp.bfloat16)
    permt = jnp.asarray(perm_np.T, jnp.bfloat16)
    term_head = np.where(np.arange(LANES) < STAT_LANES, np.arange(LANES) % B_HEADS, -1)
    expand = jnp.asarray(term_head[:, None] == (np.arange(B_WIDTH) // HEAD_DIM)[None, :], jnp.bfloat16)

    for i in range(depth):
        x2d = x.reshape(tokens, d_model)
        tile = lambda g, n, s=1.0: jnp.tile(g * s, n)[None, :]
        qa, ka, va, ga, qb, kb, vb, gb, qc, kc, vc = _proj_call(
            x2d, norm_gain[i][None, :], w_in[i].astype(jnp.bfloat16), seg, perm, cos_t, sin_t,
            tile(q_norm_a[i], A_HEADS, scale), tile(k_norm_a[i], A_KV_HEADS),
            tile(q_norm_b[i], B_HEADS, scale), tile(k_norm_b[i], B_HEADS), b, seq)

        r3 = lambda t: t.reshape(b, seq, t.shape[-1])
        (mixed_a,) = _attn_call(r3(qa), r3(ka), r3(va), A_WINDOW - 1,
                                sinks=sinks_a[i], gate=r3(ga), name="attn_a")

        (w1, d1), (w4, d4), (w16, d16) = B_PATTERNS
        o1, s1 = _attn_call(r3(qb), r3(kb), r3(vb), w1 // d1, want_stats=True, name="attn_b1")
        o4, s4 = _attn_mid_call(qc, kc, vc, w4 // d4, name="attn_b4")
        fold = lambda t: t.reshape(b * CLASSES, seq // CLASSES, t.shape[-1])
        o16, s16 = _attn_call(fold(qc), fold(kc), fold(vc), w16 // d16, want_stats=True, name="attn_b16")
        unfold = lambda t: t.reshape(b, CLASSES, seq // CLASSES, t.shape[-1])

        out = _out_call(x2d, mixed_a.reshape(tokens, A_WIDTH),
                        o1.reshape(tokens, B_WIDTH), s1.reshape(tokens, LANES),
                        o4, s4, unfold(o16), unfold(s16), gb, permt, expand,
                        w_out[i].astype(jnp.bfloat16), seq)
        x = out.reshape(b, seq, d_model)
    return x
```

```python
import functools

import numpy as np
import jax
import jax.numpy as jnp
from jax import lax
from jax.experimental import pallas as pl
from jax.experimental.pallas import tpu as pltpu

HEAD_DIM = 64
HALF = HEAD_DIM // 2
A_HEADS = 8
A_KV_HEADS = 2
A_WINDOW = 128
B_HEADS = 8
B_PATTERNS = ((128, 1), (512, 4), (2048, 16))
BLOCK = 128
ROPE_THETA = 10000.0
EPS = 1e-6
NEG = -1e30
LOG2E = 1.4426950408889634

A_WIDTH = A_HEADS * HEAD_DIM
A_KV_WIDTH = A_KV_HEADS * HEAD_DIM
B_WIDTH = B_HEADS * HEAD_DIM

LANES = 128
BF16_ROWS = 16
SEG_TILE = 256
PROJ_ROWS = 512
VMEM_LIMIT = 56 * 1024 * 1024

A_KV_PAIR_WIDTH = 2 * A_KV_HEADS * HEAD_DIM
assert A_KV_WIDTH == LANES

CLASSES = max(d for _, d in B_PATTERNS)
LOCAL = CLASSES * BF16_ROWS
MID_DIL = 4
MID_RUN = BLOCK // MID_DIL
STAT_LANES = 3 * B_HEADS


def _dot(a, b):
    return jnp.dot(a, b, preferred_element_type=jnp.float32)


def _split3(v):
    hi = v.astype(jnp.bfloat16).astype(jnp.float32)
    r1 = v - hi
    mid = r1.astype(jnp.bfloat16).astype(jnp.float32)
    lo = (r1 - mid).astype(jnp.bfloat16).astype(jnp.float32)
    return hi, mid, lo


def _head_norm_rope(p, seg, gain, cos, sin, hi_mask):
    width = p.shape[-1]
    sq = (p * p).astype(jnp.bfloat16)
    step = min(width, SEG_TILE)
    sums = []
    for c in range(0, width, step):
        sums.append(_dot(sq[:, c:c + step], seg[:step, :step]))
    ss = sums[0] if len(sums) == 1 else jnp.concatenate(sums, axis=-1)
    y = p * lax.rsqrt(ss * (1.0 / HEAD_DIM) + EPS) * gain
    outs = []
    for c in range(0, width, LANES):
        yc = y[:, c:c + LANES]
        partner = jnp.where(hi_mask, pltpu.roll(yc, HALF, 1), pltpu.roll(yc, LANES - HALF, 1))
        outs.append(yc * cos + partner * sin)
    return outs[0] if len(outs) == 1 else jnp.concatenate(outs, axis=-1)


def _proj_kernel(x_ref, gain_ref, w_ref, seg_ref, perm_ref, cos_ref, sin_ref,
                 gqa_ref, gka_ref, gqb_ref, gkb_ref,
                 qa_ref, ka_ref, va_ref, ga_ref, qb_ref, kb_ref, vb_ref, gb_ref,
                 qc_ref, kc_ref, vc_ref, pbuf):
    xf = x_ref[...]
    ms = jnp.mean(xf * xf, axis=-1, keepdims=True)
    h = (xf * lax.rsqrt(ms + EPS) * gain_ref[...]).astype(jnp.bfloat16)
    seg = seg_ref[...]
    perm = perm_ref[...]
    cos = cos_ref[...]
    sin = sin_ref[...]
    lane = lax.broadcasted_iota(jnp.int32, (1, LANES), 1)
    hi_mask = (lane & HALF) != 0
    low = lane < HEAD_DIM
    bf = jnp.bfloat16

    def per_pair(y):
        swapped = pltpu.roll(y, HEAD_DIM, 1)
        return jnp.concatenate([jnp.where(low, y, swapped), jnp.where(low, swapped, y)], axis=-1)

    def normed(gain_ref_):
        return lambda p: _head_norm_rope(p, seg, gain_ref_[...], cos, sin, hi_mask)

    plain = lambda p: p
    groups = [
        [(A_WIDTH, normed(gqa_ref), qa_ref, None)],
        [(A_KV_WIDTH, lambda p: per_pair(normed(gka_ref)(p)), ka_ref, None),
         (A_KV_WIDTH, per_pair, va_ref, None)],
        [(A_WIDTH, plain, ga_ref, None)],
        [(B_WIDTH, normed(gqb_ref), qb_ref, qc_ref)],
        [(B_WIDTH, normed(gkb_ref), kb_ref, kc_ref)],
        [(B_WIDTH, plain, vb_ref, vc_ref)],
        [(B_WIDTH, plain, gb_ref, None)],
    ]
    widths = [sum(part[0] for part in g) for g in groups]
    starts = np.cumsum([0] + widths)

    def project(i):
        pbuf[i % 2, :, 0:widths[i]] = _dot(h, w_ref[:, starts[i]:starts[i] + widths[i]])

    def finish(i):
        lo = 0
        for width, epilogue, nat_ref, _ in groups[i]:
            nat_ref[...] = epilogue(pbuf[i % 2, :, lo:lo + width]).astype(nat_ref.dtype)
            lo += width

    def to_class_major(i):
        _, _, nat_ref, cm_ref = groups[i][0]
        if cm_ref is None:
            return
        for j in range(nat_ref.shape[0] // LOCAL):
            z = _dot(perm, nat_ref[j * LOCAL:(j + 1) * LOCAL, :]).astype(bf)
            for c in range(CLASSES):
                cm_ref[c, j * BF16_ROWS:(j + 1) * BF16_ROWS, :] = z[c * BF16_ROWS:(c + 1) * BF16_ROWS, :]

    n = len(groups)
    for step in range(n + 2):
        if step < n:
            project(step)
        if 0 <= step - 1 < n:
            finish(step - 1)
        if 0 <= step - 2 < n:
            to_class_major(step - 2)


def _proj_call(x2d, gain, w_bf, seg, perm, cos_t, sin_t, gqa, gka, gqb, gkb, batch, seq):
    tokens, d_model = x2d.shape
    tm = PROJ_ROWS
    steps_per_seq = seq // tm
    row = lambda w: pl.BlockSpec((tm, w), lambda g: (g, 0))
    full = lambda a: pl.BlockSpec(a.shape, lambda g: (0,) * a.ndim)
    table = pl.BlockSpec((tm, LANES), lambda g: (g % steps_per_seq, 0))
    cmaj = pl.BlockSpec((None, CLASSES, tm // CLASSES, B_WIDTH),
                        lambda g: (g // steps_per_seq, 0, g % steps_per_seq, 0))
    bf = jnp.bfloat16
    cm_shape = jax.ShapeDtypeStruct((batch, CLASSES, seq // CLASSES, B_WIDTH), bf)
    out_shape = [
        jax.ShapeDtypeStruct((tokens, A_WIDTH), bf),
        jax.ShapeDtypeStruct((tokens, A_KV_PAIR_WIDTH), bf),
        jax.ShapeDtypeStruct((tokens, A_KV_PAIR_WIDTH), bf),
        jax.ShapeDtypeStruct((tokens, A_WIDTH), jnp.float32),
        jax.ShapeDtypeStruct((tokens, B_WIDTH), bf),
        jax.ShapeDtypeStruct((tokens, B_WIDTH), bf),
        jax.ShapeDtypeStruct((tokens, B_WIDTH), bf),
        jax.ShapeDtypeStruct((tokens, B_WIDTH), jnp.float32),
        cm_shape, cm_shape, cm_shape,
    ]
    return pl.pallas_call(
        _proj_kernel,
        grid=(tokens // tm,),
        in_specs=[row(d_model), full(gain), full(w_bf), full(seg), full(perm), table, table,
                  full(gqa), full(gka), full(gqb), full(gkb)],
        out_specs=[row(s.shape[1]) for s in out_shape[:8]] + [cmaj] * 3,
        out_shape=out_shape,
        scratch_shapes=[pltpu.VMEM((2, tm, max(A_WIDTH, B_WIDTH)), jnp.float32)],
        compiler_params=pltpu.CompilerParams(
            dimension_semantics=("parallel",), vmem_limit_bytes=VMEM_LIMIT),
        name="proj",
    )(x2d, gain, w_bf, seg, perm, cos_t, sin_t, gqa, gka, gqb, gkb)


ONES_ROWS = BF16_ROWS
ATTN_BLOCKS = 16


def _attn_kernel(*refs, pairs, slabs, group, nblk, strided, has_sinks, has_gate, want_stats):
    it = iter(refs)
    if strided:
        q_ref, kc_ref, vc_ref, bias_ref = (next(it) for _ in range(4))
        kp_ref = vp_ref = None
    else:
        q_ref, kc_ref, kp_ref, vc_ref, vp_ref, bias_ref = (next(it) for _ in range(6))
    sink_ref = next(it) if has_sinks else None
    gate_ref = next(it) if has_gate else None
    o_ref = next(it)
    st_ref = next(it) if want_stats else None
    vtbuf, sbuf, mbuf = (next(it) for _ in range(3))
    lsebuf = next(it) if want_stats else None

    def block_of(ref, g, m, cols):
        if strided:
            return ref[:, g, m, :, cols].reshape(BLOCK, LANES)
        return ref[g, m * BLOCK:(m + 1) * BLOCK, cols]

    def key_block(cur_ref, prev_ref, g, kb, cols):
        if kb > 0:
            return block_of(cur_ref, g, kb - 1, cols)
        return block_of(cur_ref, g, 0, cols) if strided else prev_ref[g, :, cols]

    def store_block(ref, g, m, cols, val):
        if strided:
            ref[:, g, m, :, cols] = val.reshape(BLOCK // MID_RUN, MID_RUN, LANES)
        else:
            ref[g, m * BLOCK:(m + 1) * BLOCK, cols] = val

    if strided:
        first_bias = bias_ref[1]
    else:
        first_bias = bias_ref[jnp.where(pl.program_id(1) == 0, 1, 0)]

    ones = jnp.ones((ONES_ROWS, BLOCK), vtbuf.dtype)

    def transpose_values(g, kb):
        for sl in range(slabs):
            src = slice(sl * LANES, (sl + 1) * LANES)
            vtbuf[sl, g * (nblk + 1) + kb, 0:LANES, :] = key_block(vc_ref, vp_ref, g, kb, src).T
            vtbuf[sl, g * (nblk + 1) + kb, LANES:, :] = ones

    lane = lax.broadcasted_iota(jnp.int32, (1, LANES), 1)
    low = lane < HEAD_DIM
    zero = jnp.zeros((), q_ref.dtype)
    onehot = bias_ref[2]

    def sink_row(hp):
        return jnp.concatenate([jnp.full((1, LANES), sink_ref[2 * hp] * LOG2E, jnp.float32),
                                jnp.full((1, LANES), sink_ref[2 * hp + 1] * LOG2E, jnp.float32)], axis=1)

    def score_stage(t):
        slot, (g, j) = t % 2, divmod(t, nblk)
        bias_t = first_bias if j == 0 else bias_ref[0]
        if j == 0:
            transpose_values(g, 0)
        transpose_values(g, j + 1)
        for hp in range(pairs):
            sl = hp * slabs // pairs
            kcols = slice(sl * LANES, (sl + 1) * LANES)
            qp = block_of(q_ref, g, j, slice(hp * LANES, (hp + 1) * LANES))
            qm = jnp.concatenate([jnp.where(low, qp, zero), jnp.where(low, zero, qp)], axis=0)
            kw = jnp.concatenate([key_block(kc_ref, kp_ref, g, j, kcols),
                                  key_block(kc_ref, kp_ref, g, j + 1, kcols)], axis=0)
            st = lax.dot_general(jnp.concatenate([kw, bias_t], axis=1),
                                 jnp.concatenate([qm, onehot], axis=1), (((1,), (1,)), ((), ())),
                                 preferred_element_type=jnp.float32)
            m = jnp.max(st, axis=0, keepdims=True)
            if has_sinks:
                m = jnp.maximum(m, sink_row(hp))
            sbuf[slot, hp] = st
            mbuf[slot, hp] = m

    def value_stage(t):
        slot, (g, j) = t % 2, divmod(t, nblk)
        kb0 = g * (nblk + 1) + j
        for hp in range(pairs):
            sl = hp * slabs // pairs
            cols = slice(hp * LANES, (hp + 1) * LANES)
            m = mbuf[slot, hp]
            pt = jnp.exp2(sbuf[slot, hp] - m).astype(jnp.bfloat16)
            vt = jnp.concatenate([vtbuf[sl, kb0], vtbuf[sl, kb0 + 1]], axis=1)
            r = _dot(vt, pt)
            l = r[LANES:LANES + 1, :]
            if has_sinks:
                l = l + jnp.exp2(sink_row(hp) - m)
            inv = 1.0 / l
            ot = jnp.concatenate([r[0:HEAD_DIM, 0:LANES] * inv[:, 0:LANES],
                                  r[HEAD_DIM:LANES, LANES:] * inv[:, LANES:]], axis=0)
            o = ot.T
            if has_gate:
                gt = block_of(gate_ref, g, j, cols)
                o = o * (gt * jax.nn.sigmoid(gt))
            store_block(o_ref, g, j, cols, o.astype(o_ref.dtype))
            if want_stats:
                lse = m + jnp.log2(l)
                lsebuf[slot, 2 * hp:2 * hp + 1, :] = lse[:, 0:LANES]
                lsebuf[slot, 2 * hp + 1:2 * hp + 2, :] = lse[:, LANES:]
        if want_stats:
            terms = _split3(lsebuf[slot])
            pad = jnp.zeros((LANES - STAT_LANES, BLOCK), jnp.float32)
            tile = jnp.concatenate(list(terms) + [pad], axis=0)
            store_block(st_ref, g, j, slice(0, LANES), tile.T.astype(st_ref.dtype))

    jobs = group * nblk
    score_stage(0)
    for t in range(jobs):
        if t + 1 < jobs:
            score_stage(t + 1)
        value_stage(t)


def _band_bias(max_dist, mid):
    r = np.arange(BLOCK)
    pos = MID_DIL * (r % MID_RUN) + r // MID_RUN if mid else r
    qpos = pos[None, :] + BLOCK
    kpos = np.concatenate([pos, pos + BLOCK])[:, None]
    dist = qpos - kpos
    valid = (dist >= 0) & (dist <= max_dist)
    first = valid & (np.arange(2 * BLOCK)[:, None] >= BLOCK)
    masks = np.where(np.stack([valid, first]), 0.0, NEG)
    onehot = np.tile(np.eye(BLOCK), (2, 1))
    return jnp.asarray(np.concatenate([masks, onehot[None]]), jnp.bfloat16)


def _attn_scratch(pairs, slabs, key_blocks, want_stats):
    scratch = [pltpu.VMEM((slabs, key_blocks, LANES + ONES_ROWS, BLOCK), jnp.bfloat16),
               pltpu.VMEM((2, pairs, 2 * BLOCK, 2 * LANES), jnp.float32),
               pltpu.VMEM((2, pairs, 1, 2 * LANES), jnp.float32)]
    if want_stats:
        scratch.append(pltpu.VMEM((2, 2 * pairs, BLOCK), jnp.float32))
    return scratch


def _attn_call(q, k, v, max_dist, *, sinks=None, gate=None, want_stats=False, name):
    n, seq, qw = q.shape
    kvw = k.shape[-1]
    pairs, slabs = qw // LANES, kvw // LANES
    tq = min(seq, ATTN_BLOCKS * BLOCK)
    sub = tq // BLOCK
    group = ATTN_BLOCKS // sub
    cur = lambda w: pl.BlockSpec((group, tq, w), lambda b, i: (b, i, 0))
    prev = lambda w: pl.BlockSpec((group, BLOCK, w), lambda b, i: (b, jnp.maximum(i * sub - 1, 0), 0))
    bias = _band_bias(max_dist, mid=False)
    args = [q, k, k, v, v, bias]
    in_specs = [cur(qw), cur(kvw), prev(kvw), cur(kvw), prev(kvw),
                pl.BlockSpec(bias.shape, lambda b, i: (0, 0, 0))]
    if sinks is not None:
        args.append(sinks)
        in_specs.append(pl.BlockSpec(memory_space=pltpu.SMEM))
    if gate is not None:
        args.append(gate)
        in_specs.append(cur(qw))
    out_shape = [jax.ShapeDtypeStruct((n, seq, qw), jnp.bfloat16)]
    out_specs = [cur(qw)]
    if want_stats:
        out_shape.append(jax.ShapeDtypeStruct((n, seq, LANES), jnp.bfloat16))
        out_specs.append(cur(LANES))
    kernel = functools.partial(_attn_kernel, pairs=pairs, slabs=slabs, group=group, nblk=sub,
                               strided=False, has_sinks=sinks is not None, has_gate=gate is not None,
                               want_stats=want_stats)
    return pl.pallas_call(
        kernel,
        grid=(n // group, seq // tq),
        in_specs=in_specs,
        out_specs=out_specs,
        out_shape=out_shape,
        scratch_shapes=_attn_scratch(pairs, slabs, group * (sub + 1), want_stats),
        compiler_params=pltpu.CompilerParams(
            dimension_semantics=("parallel", "arbitrary"), vmem_limit_bytes=VMEM_LIMIT),
        name=name,
    )(*args)


def _attn_mid_call(q, k, v, max_dist, *, name):
    b, classes, per_class, w = q.shape
    outer = classes // MID_DIL
    nblk = per_class // MID_RUN
    group = max(1, min(MID_DIL, ATTN_BLOCKS // nblk))
    view = lambda t: t.reshape(b, outer, MID_DIL, nblk, MID_RUN, t.shape[-1])
    spec = lambda width: pl.BlockSpec((None, outer, group, nblk, MID_RUN, width),
                                      lambda i, e: (i, 0, e, 0, 0, 0))
    pairs = w // LANES
    bias = _band_bias(max_dist, mid=True)
    shape6 = (b, outer, MID_DIL, nblk, MID_RUN)
    kernel = functools.partial(_attn_kernel, pairs=pairs, slabs=pairs, group=group, nblk=nblk,
                               strided=True, has_sinks=False, has_gate=False, want_stats=True)
    o, st = pl.pallas_call(
        kernel,
        grid=(b, MID_DIL // group),
        in_specs=[spec(w), spec(w), spec(w), pl.BlockSpec(bias.shape, lambda i, e: (0, 0, 0))],
        out_specs=[spec(w), spec(LANES)],
        out_shape=[jax.ShapeDtypeStruct(shape6 + (w,), jnp.bfloat16),
                   jax.ShapeDtypeStruct(shape6 + (LANES,), jnp.bfloat16)],
        scratch_shapes=_attn_scratch(pairs, pairs, group * (nblk + 1), True),
        compiler_params=pltpu.CompilerParams(
            dimension_semantics=("parallel", "arbitrary"), vmem_limit_bytes=VMEM_LIMIT),
        name=name,
    )(view(q), view(k), view(v), bias)
    return o.reshape(b, classes, per_class, w), st.reshape(b, classes, per_class, LANES)


def _expand_heads(w, expand):
    hi, mid, lo = _split3(w)
    packed = hi + pltpu.roll(mid, B_HEADS, 1) + pltpu.roll(lo, 2 * B_HEADS, 1)
    return _dot(packed.astype(jnp.bfloat16), expand)


def _stat_sum(st):
    return (st + pltpu.roll(st, LANES - B_HEADS, 1) + pltpu.roll(st, LANES - 2 * B_HEADS, 1))


def _out_kernel(x_ref, ma_ref, o1_ref, s1_ref, o4_ref, s4_ref, o16_ref, s16_ref, gb_ref,
                permt_ref, expand_ref, w_ref, out_ref):
    rows = x_ref.shape[0]
    permt = permt_ref[...]

    def to_token_order(o_ref, s_ref):
        outs, stats = [], []
        for j in range(rows // LOCAL):
            sel = slice(j * BF16_ROWS, (j + 1) * BF16_ROWS)
            y = jnp.concatenate([jnp.concatenate([o_ref[c, sel, :] for c in range(CLASSES)], axis=0),
                                 jnp.concatenate([s_ref[c, sel, :] for c in range(CLASSES)], axis=0)],
                                axis=1)
            u = _dot(permt, y)
            outs.append(u[:, :B_WIDTH])
            stats.append(u[:, B_WIDTH:])
        return jnp.concatenate(outs, axis=0), jnp.concatenate(stats, axis=0)

    o1 = o1_ref[...].astype(jnp.float32)
    l1 = _stat_sum(s1_ref[...].astype(jnp.float32))
    o4, s4 = to_token_order(o4_ref, s4_ref)
    o16, s16 = to_token_order(o16_ref, s16_ref)
    l4, l16 = _stat_sum(s4), _stat_sum(s16)

    top = jnp.maximum(jnp.maximum(l1, l4), l16)
    e1, e4, e16 = jnp.exp2(l1 - top), jnp.exp2(l4 - top), jnp.exp2(l16 - top)
    head_lane = lax.broadcasted_iota(jnp.int32, (1, LANES), 1) < B_HEADS
    inv = jnp.where(head_lane, 1.0 / (e1 + e4 + e16), 0.0)
    keep = lambda e: jnp.where(head_lane, e, 0.0) * inv
    expand = expand_ref[...]
    ob = o16 + _expand_heads(keep(e1), expand) * (o1 - o16) + _expand_heads(keep(e4), expand) * (o4 - o16)
    gb = gb_ref[...]
    mb = (ob * (gb * jax.nn.sigmoid(gb))).astype(jnp.bfloat16)
    out_ref[...] = (x_ref[...] + _dot(ma_ref[...], w_ref[0:A_WIDTH, :])
                    + _dot(mb, w_ref[A_WIDTH:, :]))


def _out_call(x2d, mixed_a, o1, s1, o4, s4, o16, s16, gate_b, permt, expand, w_bf, seq):
    tokens, d_model = x2d.shape
    tm = PROJ_ROWS
    steps_per_seq = seq // tm
    row = lambda w: pl.BlockSpec((tm, w), lambda g: (g, 0))
    full = lambda a: pl.BlockSpec(a.shape, lambda g: (0,) * a.ndim)
    cmaj = lambda w: pl.BlockSpec((None, CLASSES, tm // CLASSES, w),
                                  lambda g: (g // steps_per_seq, 0, g % steps_per_seq, 0))
    return pl.pallas_call(
        _out_kernel,
        grid=(tokens // tm,),
        in_specs=[row(d_model), row(A_WIDTH), row(B_WIDTH), row(LANES),
                  cmaj(B_WIDTH), cmaj(LANES), cmaj(B_WIDTH), cmaj(LANES),
                  row(B_WIDTH), full(permt), full(expand), full(w_bf)],
        out_specs=row(d_model),
        out_shape=jax.ShapeDtypeStruct((tokens, d_model), jnp.float32),
        compiler_params=pltpu.CompilerParams(
            dimension_semantics=("parallel",), vmem_limit_bytes=VMEM_LIMIT),
        name="merge_out",
    )(x2d, mixed_a, o1, s1, o4, s4, o16, s16, gate_b, permt, expand, w_bf)


def _rope_tables(seq):
    inv = ROPE_THETA ** (-jnp.arange(HALF, dtype=jnp.float32) / HALF)
    ang = jnp.arange(seq).astype(jnp.float32)[:, None] * inv[None, :]
    reps = LANES // HALF
    cos_t = jnp.tile(jnp.cos(ang), (1, reps))
    sign = jnp.tile(jnp.concatenate([-jnp.ones((HALF,), jnp.float32), jnp.ones((HALF,), jnp.float32)]),
                    LANES // HEAD_DIM)
    sin_t = jnp.tile(jnp.sin(ang), (1, reps)) * sign[None, :]
    return cos_t, sin_t


def kernel(x, norm_gain, w_in, q_norm_a, k_norm_a, sinks_a, q_norm_b, k_norm_b, w_out):
    b, seq, d_model = x.shape
    tokens = b * seq
    depth = norm_gain.shape[0]
    assert [d for _, d in B_PATTERNS] == [1, MID_DIL, CLASSES]
    assert seq % PROJ_ROWS == 0 and PROJ_ROWS % LOCAL == 0 and (seq // CLASSES) % BLOCK == 0
    scale = HEAD_DIM ** -0.5 * LOG2E
    cos_t, sin_t = _rope_tables(seq)
    seg_i = np.arange(SEG_TILE) // HEAD_DIM
    seg = jnp.asarray(seg_i[:, None] == seg_i[None, :], jnp.bfloat16)
    r = np.arange(LOCAL)
    perm_np = (CLASSES * (r % BF16_ROWS) + r // BF16_ROWS)[:, None] == r[None, :]
    perm = jnp.asarray(perm_np, jnp.bfloat16)
    permt = jnp.asarray(perm_np.T, jnp.bfloat16)
    term_head = np.where(np.arange(LANES) < STAT_LANES, np.arange(LANES) % B_HEADS, -1)
    expand = jnp.asarray(term_head[:, None] == (np.arange(B_WIDTH) // HEAD_DIM)[None, :], jnp.bfloat16)

    for i in range(depth):
        x2d = x.reshape(tokens, d_model)
        tile = lambda g, n, s=1.0: jnp.tile(g * s, n)[None, :]
        qa, ka, va, ga, qb, kb, vb, gb, qc, kc, vc = _proj_call(
            x2d, norm_gain[i][None, :], w_in[i].astype(jnp.bfloat16), seg, perm, cos_t, sin_t,
            tile(q_norm_a[i], A_HEADS, scale), tile(k_norm_a[i], A_KV_HEADS),
            tile(q_norm_b[i], B_HEADS, scale), tile(k_norm_b[i], B_HEADS), b, seq)

        r3 = lambda t: t.reshape(b, seq, t.shape[-1])
        (mixed_a,) = _attn_call(r3(qa), r3(ka), r3(va), A_WINDOW - 1,
                                sinks=sinks_a[i], gate=r3(ga), name="attn_a")

        (w1, d1), (w4, d4), (w16, d16) = B_PATTERNS
        o1, s1 = _attn_call(r3(qb), r3(kb), r3(vb), w1 // d1, want_stats=True, name="attn_b1")
        o4, s4 = _attn_mid_call(qc, kc, vc, w4 // d4, name="attn_b4")
        fold = lambda t: t.reshape(b * CLASSES, seq // CLASSES, t.shape[-1])
        o16, s16 = _attn_call(fold(qc), fold(kc), fold(vc), w16 // d16, want_stats=True, name="attn_b16")
        unfold = lambda t: t.reshape(b, CLASSES, seq // CLASSES, t.shape[-1])

        out = _out_call(x2d, mixed_a.reshape(tokens, A_WIDTH),
                        o1.reshape(tokens, B_WIDTH), s1.reshape(tokens, LANES),
                        o4, s4, unfold(o16), unfold(s16), gb, permt, expand,
                        w_out[i].astype(jnp.bfloat16), seq)
        x = out.reshape(b, seq, d_model)
    return x
```

```python
import functools

import numpy as np
import jax
import jax.numpy as jnp
from jax import lax
from jax.experimental import pallas as pl
from jax.experimental.pallas import tpu as pltpu

HEAD_DIM = 64
HALF = HEAD_DIM // 2
A_HEADS = 8
A_KV_HEADS = 2
A_WINDOW = 128
B_HEADS = 8
B_PATTERNS = ((128, 1), (512, 4), (2048, 16))
BLOCK = 128
ROPE_THETA = 10000.0
EPS = 1e-6
NEG = -1e30
LOG2E = 1.4426950408889634

A_WIDTH = A_HEADS * HEAD_DIM
A_KV_WIDTH = A_KV_HEADS * HEAD_DIM
B_WIDTH = B_HEADS * HEAD_DIM

LANES = 128
BF16_ROWS = 16
SEG_TILE = 256
PROJ_ROWS = 512
VMEM_LIMIT = 56 * 1024 * 1024

A_KV_PAIR_WIDTH = 2 * A_KV_HEADS * HEAD_DIM
assert A_KV_WIDTH == LANES

CLASSES = max(d for _, d in B_PATTERNS)
LOCAL = CLASSES * BF16_ROWS
MID_DIL = 4
MID_RUN = BLOCK // MID_DIL
STAT_LANES = 3 * B_HEADS


def _dot(a, b):
    return jnp.dot(a, b, preferred_element_type=jnp.float32)


def _split3(v):
    hi = v.astype(jnp.bfloat16).astype(jnp.float32)
    r1 = v - hi
    mid = r1.astype(jnp.bfloat16).astype(jnp.float32)
    lo = (r1 - mid).astype(jnp.bfloat16).astype(jnp.float32)
    return hi, mid, lo


def _head_norm_rope(p, seg, gain, cos, sin, hi_mask):
    width = p.shape[-1]
    sq = (p * p).astype(jnp.bfloat16)
    step = min(width, SEG_TILE)
    sums = []
    for c in range(0, width, step):
        sums.append(_dot(sq[:, c:c + step], seg[:step, :step]))
    ss = sums[0] if len(sums) == 1 else jnp.concatenate(sums, axis=-1)
    y = p * lax.rsqrt(ss * (1.0 / HEAD_DIM) + EPS) * gain
    outs = []
    for c in range(0, width, LANES):
        yc = y[:, c:c + LANES]
        partner = jnp.where(hi_mask, pltpu.roll(yc, HALF, 1), pltpu.roll(yc, LANES - HALF, 1))
        outs.append(yc * cos + partner * sin)
    return outs[0] if len(outs) == 1 else jnp.concatenate(outs, axis=-1)


def _proj_kernel(x_ref, gain_ref, w_ref, seg_ref, perm_ref, cos_ref, sin_ref,
                 gqa_ref, gka_ref, gqb_ref, gkb_ref,
                 qa_ref, ka_ref, va_ref, ga_ref, qb_ref, kb_ref, vb_ref, gb_ref,
                 qc_ref, kc_ref, vc_ref, pbuf):
    xf = x_ref[...]
    ms = jnp.mean(xf * xf, axis=-1, keepdims=True)
    h = (xf * lax.rsqrt(ms + EPS) * gain_ref[...]).astype(jnp.bfloat16)
    seg = seg_ref[...]
    perm = perm_ref[...]
    cos = cos_ref[...]
    sin = sin_ref[...]
    lane = lax.broadcasted_iota(jnp.int32, (1, LANES), 1)
    hi_mask = (lane & HALF) != 0
    low = lane < HEAD_DIM
    bf = jnp.bfloat16

    def per_pair(y):
        swapped = pltpu.roll(y, HEAD_DIM, 1)
        return jnp.concatenate([jnp.where(low, y, swapped), jnp.where(low, swapped, y)], axis=-1)

    def normed(gain_ref_):
        return lambda p: _head_norm_rope(p, seg, gain_ref_[...], cos, sin, hi_mask)

    plain = lambda p: p
    groups = [
        [(A_WIDTH, normed(gqa_ref), qa_ref, None)],
        [(A_KV_WIDTH, lambda p: per_pair(normed(gka_ref)(p)), ka_ref, None),
         (A_KV_WIDTH, per_pair, va_ref, None)],
        [(A_WIDTH, plain, ga_ref, None)],
        [(B_WIDTH, normed(gqb_ref), qb_ref, qc_ref)],
        [(B_WIDTH, normed(gkb_ref), kb_ref, kc_ref)],
        [(B_WIDTH, plain, vb_ref, vc_ref)],
        [(B_WIDTH, plain, gb_ref, None)],
    ]
    widths = [sum(part[0] for part in g) for g in groups]
    starts = np.cumsum([0] + widths)

    def project(i):
        pbuf[i % 2, :, 0:widths[i]] = _dot(h, w_ref[:, starts[i]:starts[i] + widths[i]])

    def finish(i):
        lo = 0
        for width, epilogue, nat_ref, _ in groups[i]:
            nat_ref[...] = epilogue(pbuf[i % 2, :, lo:lo + width]).astype(nat_ref.dtype)
            lo += width

    def to_class_major(i):
        _, _, nat_ref, cm_ref = groups[i][0]
        if cm_ref is None:
            return
        for j in range(nat_ref.shape[0] // LOCAL):
            z = _dot(perm, nat_ref[j * LOCAL:(j + 1) * LOCAL, :]).astype(bf)
            for c in range(CLASSES):
                cm_ref[c, j * BF16_ROWS:(j + 1) * BF16_ROWS, :] = z[c * BF16_ROWS:(c + 1) * BF16_ROWS, :]

    n = len(groups)
    for step in range(n + 2):
        if step < n:
            project(step)
        if 0 <= step - 1 < n:
            finish(step - 1)
        if 0 <= step - 2 < n:
            to_class_major(step - 2)


def _proj_call(x2d, gain, w_bf, seg, perm, cos_t, sin_t, gqa, gka, gqb, gkb, batch, seq):
    tokens, d_model = x2d.shape
    tm = PROJ_ROWS
    steps_per_seq = seq // tm
    row = lambda w: pl.BlockSpec((tm, w), lambda g: (g, 0))
    full = lambda a: pl.BlockSpec(a.shape, lambda g: (0,) * a.ndim)
    table = pl.BlockSpec((tm, LANES), lambda g: (g % steps_per_seq, 0))
    cmaj = pl.BlockSpec((None, CLASSES, tm // CLASSES, B_WIDTH),
                        lambda g: (g // steps_per_seq, 0, g % steps_per_seq, 0))
    bf = jnp.bfloat16
    cm_shape = jax.ShapeDtypeStruct((batch, CLASSES, seq // CLASSES, B_WIDTH), bf)
    out_shape = [
        jax.ShapeDtypeStruct((tokens, A_WIDTH), bf),
        jax.ShapeDtypeStruct((tokens, A_KV_PAIR_WIDTH), bf),
        jax.ShapeDtypeStruct((tokens, A_KV_PAIR_WIDTH), bf),
        jax.ShapeDtypeStruct((tokens, A_WIDTH), jnp.float32),
        jax.ShapeDtypeStruct((tokens, B_WIDTH), bf),
        jax.ShapeDtypeStruct((tokens, B_WIDTH), bf),
        jax.ShapeDtypeStruct((tokens, B_WIDTH), bf),
        jax.ShapeDtypeStruct((tokens, B_WIDTH), jnp.float32),
        cm_shape, cm_shape, cm_shape,
    ]
    return pl.pallas_call(
        _proj_kernel,
        grid=(tokens // tm,),
        in_specs=[row(d_model), full(gain), full(w_bf), full(seg), full(perm), table, table,
                  full(gqa), full(gka), full(gqb), full(gkb)],
        out_specs=[row(s.shape[1]) for s in out_shape[:8]] + [cmaj] * 3,
        out_shape=out_shape,
        scratch_shapes=[pltpu.VMEM((2, tm, max(A_WIDTH, B_WIDTH)), jnp.float32)],
        compiler_params=pltpu.CompilerParams(
            dimension_semantics=("parallel",), vmem_limit_bytes=VMEM_LIMIT),
        name="proj",
    )(x2d, gain, w_bf, seg, perm, cos_t, sin_t, gqa, gka, gqb, gkb)


ONES_ROWS = BF16_ROWS
ATTN_BLOCKS = 16


def _attn_kernel(*refs, pairs, slabs, group, nblk, strided, has_sinks, has_gate, want_stats):
    it = iter(refs)
    if strided:
        q_ref, kc_ref, vc_ref, bias_ref = (next(it) for _ in range(4))
        kp_ref = vp_ref = None
    else:
        q_ref, kc_ref, kp_ref, vc_ref, vp_ref, bias_ref = (next(it) for _ in range(6))
    sink_ref = next(it) if has_sinks else None
    gate_ref = next(it) if has_gate else None
    o_ref = next(it)
    st_ref = next(it) if want_stats else None
    vtbuf, sbuf, pbuf, mbuf = (next(it) for _ in range(4))
    lsebuf = next(it) if want_stats else None

    def block_of(ref, g, m, cols):
        if strided:
            return ref[:, g, m, :, cols].reshape(BLOCK, LANES)
        return ref[g, m * BLOCK:(m + 1) * BLOCK, cols]

    def key_block(cur_ref, prev_ref, g, kb, cols):
        if kb > 0:
            return block_of(cur_ref, g, kb - 1, cols)
        return block_of(cur_ref, g, 0, cols) if strided else prev_ref[g, :, cols]

    def store_block(ref, g, m, cols, val):
        if strided:
            ref[:, g, m, :, cols] = val.reshape(BLOCK // MID_RUN, MID_RUN, LANES)
        else:
            ref[g, m * BLOCK:(m + 1) * BLOCK, cols] = val

    if strided:
        first_bias = bias_ref[1]
    else:
        first_bias = bias_ref[jnp.where(pl.program_id(1) == 0, 1, 0)]

    ones = jnp.ones((ONES_ROWS, BLOCK), vtbuf.dtype)

    def transpose_values(g, kb):
        for sl in range(slabs):
            src = slice(sl * LANES, (sl + 1) * LANES)
            vtbuf[sl, g * (nblk + 1) + kb, 0:LANES, :] = key_block(vc_ref, vp_ref, g, kb, src).T
            vtbuf[sl, g * (nblk + 1) + kb, LANES:, :] = ones

    lane = lax.broadcasted_iota(jnp.int32, (1, LANES), 1)
    low = lane < HEAD_DIM
    zero = jnp.zeros((), q_ref.dtype)
    onehot = bias_ref[2]

    def sink_row(hp):
        return jnp.concatenate([jnp.full((1, LANES), sink_ref[2 * hp] * LOG2E, jnp.float32),
                                jnp.full((1, LANES), sink_ref[2 * hp + 1] * LOG2E, jnp.float32)], axis=1)

    def score_stage(t):
        slot, (g, j) = t % 2, divmod(t, nblk)
        bias_t = first_bias if j == 0 else bias_ref[0]
        for hp in range(pairs):
            sl = hp * slabs // pairs
            kcols = slice(sl * LANES, (sl + 1) * LANES)
            qp = block_of(q_ref, g, j, slice(hp * LANES, (hp + 1) * LANES))
            qm = jnp.concatenate([jnp.where(low, qp, zero), jnp.where(low, zero, qp)], axis=0)
            kw = jnp.concatenate([key_block(kc_ref, kp_ref, g, j, kcols),
                                  key_block(kc_ref, kp_ref, g, j + 1, kcols)], axis=0)
            sbuf[slot, hp] = lax.dot_general(
                jnp.concatenate([kw, bias_t], axis=1), jnp.concatenate([qm, onehot], axis=1),
                (((1,), (1,)), ((), ())), preferred_element_type=jnp.float32)

    def softmax_stage(t):
        slot, (g, j) = t % 2, divmod(t, nblk)
        if j == 0:
            transpose_values(g, 0)
        transpose_values(g, j + 1)
        for hp in range(pairs):
            m = jnp.max(sbuf[slot, hp], axis=0, keepdims=True)
            if has_sinks:
                m = jnp.maximum(m, sink_row(hp))
            pbuf[slot, hp] = jnp.exp2(sbuf[slot, hp] - m).astype(pbuf.dtype)
            mbuf[slot, hp] = m

    def value_stage(t):
        slot, (g, j) = t % 2, divmod(t, nblk)
        kb0 = g * (nblk + 1) + j
        for hp in range(pairs):
            sl = hp * slabs // pairs
            cols = slice(hp * LANES, (hp + 1) * LANES)
            m = mbuf[slot, hp]
            vt = jnp.concatenate([vtbuf[sl, kb0], vtbuf[sl, kb0 + 1]], axis=1)
            r = _dot(vt, pbuf[slot, hp])
            l = r[LANES:LANES + 1, :]
            if has_sinks:
                l = l + jnp.exp2(sink_row(hp) - m)
            inv = 1.0 / l
            ot = jnp.concatenate([r[0:HEAD_DIM, 0:LANES] * inv[:, 0:LANES],
                                  r[HEAD_DIM:LANES, LANES:] * inv[:, LANES:]], axis=0)
            o = ot.T
            if has_gate:
                gt = block_of(gate_ref, g, j, cols)
                o = o * (gt * jax.nn.sigmoid(gt))
            store_block(o_ref, g, j, cols, o.astype(o_ref.dtype))
            if want_stats:
                lse = m + jnp.log2(l)
                lsebuf[slot, 2 * hp:2 * hp + 1, :] = lse[:, 0:LANES]
                lsebuf[slot, 2 * hp + 1:2 * hp + 2, :] = lse[:, LANES:]
        if want_stats:
            terms = _split3(lsebuf[slot])
            pad = jnp.zeros((LANES - STAT_LANES, BLOCK), jnp.float32)
            tile = jnp.concatenate(list(terms) + [pad], axis=0)
            store_block(st_ref, g, j, slice(0, LANES), tile.T.astype(st_ref.dtype))

    jobs = group * nblk
    for step in range(jobs + 2):
        if step - 2 >= 0:
            value_stage(step - 2)
        if 0 <= step - 1 < jobs:
            softmax_stage(step - 1)
        if step < jobs:
            score_stage(step)


def _band_bias(max_dist, mid):
    r = np.arange(BLOCK)
    pos = MID_DIL * (r % MID_RUN) + r // MID_RUN if mid else r
    qpos = pos[None, :] + BLOCK
    kpos = np.concatenate([pos, pos + BLOCK])[:, None]
    dist = qpos - kpos
    valid = (dist >= 0) & (dist <= max_dist)
    first = valid & (np.arange(2 * BLOCK)[:, None] >= BLOCK)
    masks = np.where(np.stack([valid, first]), 0.0, NEG)
    onehot = np.tile(np.eye(BLOCK), (2, 1))
    return jnp.asarray(np.concatenate([masks, onehot[None]]), jnp.bfloat16)


def _attn_scratch(pairs, slabs, key_blocks, want_stats):
    scratch = [pltpu.VMEM((slabs, key_blocks, LANES + ONES_ROWS, BLOCK), jnp.bfloat16),
               pltpu.VMEM((2, pairs, 2 * BLOCK, 2 * LANES), jnp.float32),
               pltpu.VMEM((2, pairs, 2 * BLOCK, 2 * LANES), jnp.bfloat16),
               pltpu.VMEM((2, pairs, 1, 2 * LANES), jnp.float32)]
    if want_stats:
        scratch.append(pltpu.VMEM((2, 2 * pairs, BLOCK), jnp.float32))
    return scratch


def _attn_call(q, k, v, max_dist, *, sinks=None, gate=None, want_stats=False, name):
    n, seq, qw = q.shape
    kvw = k.shape[-1]
    pairs, slabs = qw // LANES, kvw // LANES
    tq = min(seq, ATTN_BLOCKS * BLOCK)
    sub = tq // BLOCK
    group = ATTN_BLOCKS // sub
    cur = lambda w: pl.BlockSpec((group, tq, w), lambda b, i: (b, i, 0))
    prev = lambda w: pl.BlockSpec((group, BLOCK, w), lambda b, i: (b, jnp.maximum(i * sub - 1, 0), 0))
    bias = _band_bias(max_dist, mid=False)
    args = [q, k, k, v, v, bias]
    in_specs = [cur(qw), cur(kvw), prev(kvw), cur(kvw), prev(kvw),
                pl.BlockSpec(bias.shape, lambda b, i: (0, 0, 0))]
    if sinks is not None:
        args.append(sinks)
        in_specs.append(pl.BlockSpec(memory_space=pltpu.SMEM))
    if gate is not None:
        args.append(gate)
        in_specs.append(cur(qw))
    out_shape = [jax.ShapeDtypeStruct((n, seq, qw), jnp.bfloat16)]
    out_specs = [cur(qw)]
    if want_stats:
        out_shape.append(jax.ShapeDtypeStruct((n, seq, LANES), jnp.bfloat16))
        out_specs.append(cur(LANES))
    kernel = functools.partial(_attn_kernel, pairs=pairs, slabs=slabs, group=group, nblk=sub,
                               strided=False, has_sinks=sinks is not None, has_gate=gate is not None,
                               want_stats=want_stats)
    return pl.pallas_call(
        kernel,
        grid=(n // group, seq // tq),
        in_specs=in_specs,
        out_specs=out_specs,
        out_shape=out_shape,
        scratch_shapes=_attn_scratch(pairs, slabs, group * (sub + 1), want_stats),
        compiler_params=pltpu.CompilerParams(
            dimension_semantics=("parallel", "arbitrary"), vmem_limit_bytes=VMEM_LIMIT),
        name=name,
    )(*args)


def _attn_mid_call(q, k, v, max_dist, *, name):
    b, classes, per_class, w = q.shape
    outer = classes // MID_DIL
    nblk = per_class // MID_RUN
    group = max(1, min(MID_DIL, ATTN_BLOCKS // nblk))
    view = lambda t: t.reshape(b, outer, MID_DIL, nblk, MID_RUN, t.shape[-1])
    spec = lambda width: pl.BlockSpec((None, outer, group, nblk, MID_RUN, width),
                                      lambda i, e: (i, 0, e, 0, 0, 0))
    pairs = w // LANES
    bias = _band_bias(max_dist, mid=True)
    shape6 = (b, outer, MID_DIL, nblk, MID_RUN)
    kernel = functools.partial(_attn_kernel, pairs=pairs, slabs=pairs, group=group, nblk=nblk,
                               strided=True, has_sinks=False, has_gate=False, want_stats=True)
    o, st = pl.pallas_call(
        kernel,
        grid=(b, MID_DIL // group),
        in_specs=[spec(w), spec(w), spec(w), pl.BlockSpec(bias.shape, lambda i, e: (0, 0, 0))],
        out_specs=[spec(w), spec(LANES)],
        out_shape=[jax.ShapeDtypeStruct(shape6 + (w,), jnp.bfloat16),
                   jax.ShapeDtypeStruct(shape6 + (LANES,), jnp.bfloat16)],
        scratch_shapes=_attn_scratch(pairs, pairs, group * (nblk + 1), True),
        compiler_params=pltpu.CompilerParams(
            dimension_semantics=("parallel", "arbitrary"), vmem_limit_bytes=VMEM_LIMIT),
        name=name,
    )(view(q), view(k), view(v), bias)
    return o.reshape(b, classes, per_class, w), st.reshape(b, classes, per_class, LANES)


def _expand_heads(w, expand):
    hi, mid, lo = _split3(w)
    packed = hi + pltpu.roll(mid, B_HEADS, 1) + pltpu.roll(lo, 2 * B_HEADS, 1)
    return _dot(packed.astype(jnp.bfloat16), expand)


def _stat_sum(st):
    return (st + pltpu.roll(st, LANES - B_HEADS, 1) + pltpu.roll(st, LANES - 2 * B_HEADS, 1))


def _out_kernel(x_ref, ma_ref, o1_ref, s1_ref, o4_ref, s4_ref, o16_ref, s16_ref, gb_ref,
                permt_ref, expand_ref, w_ref, out_ref):
    rows = x_ref.shape[0]
    permt = permt_ref[...]

    def to_token_order(o_ref, s_ref):
        outs, stats = [], []
        for j in range(rows // LOCAL):
            sel = slice(j * BF16_ROWS, (j + 1) * BF16_ROWS)
            y = jnp.concatenate([jnp.concatenate([o_ref[c, sel, :] for c in range(CLASSES)], axis=0),
                                 jnp.concatenate([s_ref[c, sel, :] for c in range(CLASSES)], axis=0)],
                                axis=1)
            u = _dot(permt, y)
            outs.append(u[:, :B_WIDTH])
            stats.append(u[:, B_WIDTH:])
        return jnp.concatenate(outs, axis=0), jnp.concatenate(stats, axis=0)

    o1 = o1_ref[...].astype(jnp.float32)
    l1 = _stat_sum(s1_ref[...].astype(jnp.float32))
    o4, s4 = to_token_order(o4_ref, s4_ref)
    o16, s16 = to_token_order(o16_ref, s16_ref)
    l4, l16 = _stat_sum(s4), _stat_sum(s16)

    top = jnp.maximum(jnp.maximum(l1, l4), l16)
    e1, e4, e16 = jnp.exp2(l1 - top), jnp.exp2(l4 - top), jnp.exp2(l16 - top)
    head_lane = lax.broadcasted_iota(jnp.int32, (1, LANES), 1) < B_HEADS
    inv = jnp.where(head_lane, 1.0 / (e1 + e4 + e16), 0.0)
    keep = lambda e: jnp.where(head_lane, e, 0.0) * inv
    expand = expand_ref[...]
    ob = o16 + _expand_heads(keep(e1), expand) * (o1 - o16) + _expand_heads(keep(e4), expand) * (o4 - o16)
    gb = gb_ref[...]
    mb = (ob * (gb * jax.nn.sigmoid(gb))).astype(jnp.bfloat16)
    out_ref[...] = (x_ref[...] + _dot(ma_ref[...], w_ref[0:A_WIDTH, :])
                    + _dot(mb, w_ref[A_WIDTH:, :]))


def _out_call(x2d, mixed_a, o1, s1, o4, s4, o16, s16, gate_b, permt, expand, w_bf, seq):
    tokens, d_model = x2d.shape
    tm = PROJ_ROWS
    steps_per_seq = seq // tm
    row = lambda w: pl.BlockSpec((tm, w), lambda g: (g, 0))
    full = lambda a: pl.BlockSpec(a.shape, lambda g: (0,) * a.ndim)
    cmaj = lambda w: pl.BlockSpec((None, CLASSES, tm // CLASSES, w),
                                  lambda g: (g // steps_per_seq, 0, g % steps_per_seq, 0))
    return pl.pallas_call(
        _out_kernel,
        grid=(tokens // tm,),
        in_specs=[row(d_model), row(A_WIDTH), row(B_WIDTH), row(LANES),
                  cmaj(B_WIDTH), cmaj(LANES), cmaj(B_WIDTH), cmaj(LANES),
                  row(B_WIDTH), full(permt), full(expand), full(w_bf)],
        out_specs=row(d_model),
        out_shape=jax.ShapeDtypeStruct((tokens, d_model), jnp.float32),
        compiler_params=pltpu.CompilerParams(
            dimension_semantics=("parallel",), vmem_limit_bytes=VMEM_LIMIT),
        name="merge_out",
    )(x2d, mixed_a, o1, s1, o4, s4, o16, s16, gate_b, permt, expand, w_bf)


def _rope_tables(seq):
    inv = ROPE_THETA ** (-jnp.arange(HALF, dtype=jnp.float32) / HALF)
    ang = jnp.arange(seq).astype(jnp.float32)[:, None] * inv[None, :]
    reps = LANES // HALF
    cos_t = jnp.tile(jnp.cos(ang), (1, reps))
    sign = jnp.tile(jnp.concatenate([-jnp.ones((HALF,), jnp.float32), jnp.ones((HALF,), jnp.float32)]),
                    LANES // HEAD_DIM)
    sin_t = jnp.tile(jnp.sin(ang), (1, reps)) * sign[None, :]
    return cos_t, sin_t


def kernel(x, norm_gain, w_in, q_norm_a, k_norm_a, sinks_a, q_norm_b, k_norm_b, w_out):
    b, seq, d_model = x.shape
    tokens = b * seq
    depth = norm_gain.shape[0]
    assert [d for _, d in B_PATTERNS] == [1, MID_DIL, CLASSES]
    assert seq % PROJ_ROWS == 0 and PROJ_ROWS % LOCAL == 0 and (seq // CLASSES) % BLOCK == 0
    scale = HEAD_DIM ** -0.5 * LOG2E
    cos_t, sin_t = _rope_tables(seq)
    seg_i = np.arange(SEG_TILE) // HEAD_DIM
    seg = jnp.asarray(seg_i[:, None] == seg_i[None, :], jnp.bfloat16)
    r = np.arange(LOCAL)
    perm_np = (CLASSES * (r % BF16_ROWS) + r // BF16_ROWS)[:, None] == r[None, :]
    perm = jnp.asarray(perm_np, jnp.bfloat16)
    permt = jnp.asarray(perm_np.T, jnp.bfloat16)
    term_head = np.where(np.arange(LANES) < STAT_LANES, np.arange(LANES) % B_HEADS, -1)
    expand = jnp.asarray(term_head[:, None] == (np.arange(B_WIDTH) // HEAD_DIM)[None, :], jnp.bfloat16)

    for i in range(depth):
        x2d = x.reshape(tokens, d_model)
        tile = lambda g, n, s=1.0: jnp.tile(g * s, n)[None, :]
        qa, ka, va, ga, qb, kb, vb, gb, qc, kc, vc = _proj_call(
            x2d, norm_gain[i][None, :], w_in[i].astype(jnp.bfloat16), seg, perm, cos_t, sin_t,
            tile(q_norm_a[i], A_HEADS, scale), tile(k_norm_a[i], A_KV_HEADS),
            tile(q_norm_b[i], B_HEADS, scale), tile(k_norm_b[i], B_HEADS), b, seq)

        r3 = lambda t: t.reshape(b, seq, t.shape[-1])
        (mixed_a,) = _attn_call(r3(qa), r3(ka), r3(va), A_WINDOW - 1,
                                sinks=sinks_a[i], gate=r3(ga), name="attn_a")

        (w1, d1), (w4, d4), (w16, d16) = B_PATTERNS
        o1, s1 = _attn_call(r3(qb), r3(kb), r3(vb), w1 // d1, want_stats=True, name="attn_b1")
        o4, s4 = _attn_mid_call(qc, kc, vc, w4 // d4, name="attn_b4")
        fold = lambda t: t.reshape(b * CLASSES, seq // CLASSES, t.shape[-1])
        o16, s16 = _attn_call(fold(qc), fold(kc), fold(vc), w16 // d16, want_stats=True, name="attn_b16")
        unfold = lambda t: t.reshape(b, CLASSES, seq // CLASSES, t.shape[-1])

        out = _out_call(x2d, mixed_a.reshape(tokens, A_WIDTH),
                        o1.reshape(tokens, B_WIDTH), s1.reshape(tokens, LANES),
                        o4, s4, unfold(o16), unfold(s16), gb, permt, expand,
                        w_out[i].astype(jnp.bfloat16), seq)
        x = out.reshape(b, seq, d_model)
    return x
```

```python
import functools

import numpy as np
import jax
import jax.numpy as jnp
from jax import lax
from jax.experimental import pallas as pl
from jax.experimental.pallas import tpu as pltpu

HEAD_DIM = 64
HALF = HEAD_DIM // 2
A_HEADS = 8
A_KV_HEADS = 2
A_WINDOW = 128
B_HEADS = 8
B_PATTERNS = ((128, 1), (512, 4), (2048, 16))
BLOCK = 128
ROPE_THETA = 10000.0
EPS = 1e-6
NEG = -1e30
LOG2E = 1.4426950408889634

A_WIDTH = A_HEADS * HEAD_DIM
A_KV_WIDTH = A_KV_HEADS * HEAD_DIM
B_WIDTH = B_HEADS * HEAD_DIM

LANES = 128
BF16_ROWS = 16
SEG_TILE = 256
PROJ_ROWS = 512
VMEM_LIMIT = 56 * 1024 * 1024

A_KV_PAIR_WIDTH = 2 * A_KV_HEADS * HEAD_DIM
assert A_KV_WIDTH == LANES

CLASSES = max(d for _, d in B_PATTERNS)
LOCAL = CLASSES * BF16_ROWS
MID_DIL = 4
MID_RUN = BLOCK // MID_DIL
STAT_LANES = 3 * B_HEADS


def _dot(a, b):
    return jnp.dot(a, b, preferred_element_type=jnp.float32)


def _split3(v):
    hi = v.astype(jnp.bfloat16).astype(jnp.float32)
    r1 = v - hi
    mid = r1.astype(jnp.bfloat16).astype(jnp.float32)
    lo = (r1 - mid).astype(jnp.bfloat16).astype(jnp.float32)
    return hi, mid, lo


def _head_norm_rope(p, seg, gain, cos, sin, hi_mask):
    width = p.shape[-1]
    sq = (p * p).astype(jnp.bfloat16)
    step = min(width, SEG_TILE)
    sums = []
    for c in range(0, width, step):
        sums.append(_dot(sq[:, c:c + step], seg[:step, :step]))
    ss = sums[0] if len(sums) == 1 else jnp.concatenate(sums, axis=-1)
    y = p * lax.rsqrt(ss * (1.0 / HEAD_DIM) + EPS) * gain
    outs = []
    for c in range(0, width, LANES):
        yc = y[:, c:c + LANES]
        partner = jnp.where(hi_mask, pltpu.roll(yc, HALF, 1), pltpu.roll(yc, LANES - HALF, 1))
        outs.append(yc * cos + partner * sin)
    return outs[0] if len(outs) == 1 else jnp.concatenate(outs, axis=-1)


def _proj_kernel(x_ref, gain_ref, w_ref, seg_ref, perm_ref, cos_ref, sin_ref,
                 gqa_ref, gka_ref, gqb_ref, gkb_ref,
                 qa_ref, ka_ref, va_ref, ga_ref, qb_ref, kb_ref, vb_ref, gb_ref,
                 qc_ref, kc_ref, vc_ref, pbuf):
    xf = x_ref[...]
    ms = jnp.mean(xf * xf, axis=-1, keepdims=True)
    h = (xf * lax.rsqrt(ms + EPS) * gain_ref[...]).astype(jnp.bfloat16)
    seg = seg_ref[...]
    perm = perm_ref[...]
    cos = cos_ref[...]
    sin = sin_ref[...]
    lane = lax.broadcasted_iota(jnp.int32, (1, LANES), 1)
    hi_mask = (lane & HALF) != 0
    low = lane < HEAD_DIM
    bf = jnp.bfloat16

    def per_pair(y):
        swapped = pltpu.roll(y, HEAD_DIM, 1)
        return jnp.concatenate([jnp.where(low, y, swapped), jnp.where(low, swapped, y)], axis=-1)

    def normed(gain_ref_):
        return lambda p: _head_norm_rope(p, seg, gain_ref_[...], cos, sin, hi_mask)

    plain = lambda p: p
    groups = [
        [(A_WIDTH, normed(gqa_ref), qa_ref, None)],
        [(A_KV_WIDTH, lambda p: per_pair(normed(gka_ref)(p)), ka_ref, None),
         (A_KV_WIDTH, per_pair, va_ref, None)],
        [(A_WIDTH, plain, ga_ref, None)],
        [(B_WIDTH, normed(gqb_ref), qb_ref, qc_ref)],
        [(B_WIDTH, normed(gkb_ref), kb_ref, kc_ref)],
        [(B_WIDTH, plain, vb_ref, vc_ref)],
        [(B_WIDTH, plain, gb_ref, None)],
    ]
    widths = [sum(part[0] for part in g) for g in groups]
    starts = np.cumsum([0] + widths)

    def project(i):
        pbuf[i % 2, :, 0:widths[i]] = _dot(h, w_ref[:, starts[i]:starts[i] + widths[i]])

    def finish(i):
        lo = 0
        for width, epilogue, nat_ref, _ in groups[i]:
            nat_ref[...] = epilogue(pbuf[i % 2, :, lo:lo + width]).astype(nat_ref.dtype)
            lo += width

    def to_class_major(i):
        _, _, nat_ref, cm_ref = groups[i][0]
        if cm_ref is None:
            return
        for j in range(nat_ref.shape[0] // LOCAL):
            z = _dot(perm, nat_ref[j * LOCAL:(j + 1) * LOCAL, :]).astype(bf)
            for c in range(CLASSES):
                cm_ref[c, j * BF16_ROWS:(j + 1) * BF16_ROWS, :] = z[c * BF16_ROWS:(c + 1) * BF16_ROWS, :]

    n = len(groups)
    for step in range(n + 2):
        if step < n:
            project(step)
        if 0 <= step - 1 < n:
            finish(step - 1)
        if 0 <= step - 2 < n:
            to_class_major(step - 2)


def _proj_call(x2d, gain, w_bf, seg, perm, cos_t, sin_t, gqa, gka, gqb, gkb, batch, seq):
    tokens, d_model = x2d.shape
    tm = PROJ_ROWS
    steps_per_seq = seq // tm
    row = lambda w: pl.BlockSpec((tm, w), lambda g: (g, 0))
    full = lambda a: pl.BlockSpec(a.shape, lambda g: (0,) * a.ndim)
    table = pl.BlockSpec((tm, LANES), lambda g: (g % steps_per_seq, 0))
    cmaj = pl.BlockSpec((None, CLASSES, tm // CLASSES, B_WIDTH),
                        lambda g: (g // steps_per_seq, 0, g % steps_per_seq, 0))
    bf = jnp.bfloat16
    cm_shape = jax.ShapeDtypeStruct((batch, CLASSES, seq // CLASSES, B_WIDTH), bf)
    out_shape = [
        jax.ShapeDtypeStruct((tokens, A_WIDTH), bf),
        jax.ShapeDtypeStruct((tokens, A_KV_PAIR_WIDTH), bf),
        jax.ShapeDtypeStruct((tokens, A_KV_PAIR_WIDTH), bf),
        jax.ShapeDtypeStruct((tokens, A_WIDTH), jnp.float32),
        jax.ShapeDtypeStruct((tokens, B_WIDTH), bf),
        jax.ShapeDtypeStruct((tokens, B_WIDTH), bf),
        jax.ShapeDtypeStruct((tokens, B_WIDTH), bf),
        jax.ShapeDtypeStruct((tokens, B_WIDTH), jnp.float32),
        cm_shape, cm_shape, cm_shape,
    ]
    return pl.pallas_call(
        _proj_kernel,
        grid=(tokens // tm,),
        in_specs=[row(d_model), full(gain), full(w_bf), full(seg), full(perm), table, table,
                  full(gqa), full(gka), full(gqb), full(gkb)],
        out_specs=[row(s.shape[1]) for s in out_shape[:8]] + [cmaj] * 3,
        out_shape=out_shape,
        scratch_shapes=[pltpu.VMEM((2, tm, max(A_WIDTH, B_WIDTH)), jnp.float32)],
        compiler_params=pltpu.CompilerParams(
            dimension_semantics=("parallel",), vmem_limit_bytes=VMEM_LIMIT),
        name="proj",
    )(x2d, gain, w_bf, seg, perm, cos_t, sin_t, gqa, gka, gqb, gkb)


ONES_ROWS = BF16_ROWS
ATTN_BLOCKS = 16


def _attn_kernel(*refs, pairs, slabs, group, nblk, strided, has_sinks, has_gate, has_merge, want_stats):
    it = iter(refs)
    if strided:
        q_ref, kc_ref, vc_ref, bias_ref = (next(it) for _ in range(4))
        kp_ref = vp_ref = None
    else:
        q_ref, kc_ref, kp_ref, vc_ref, vp_ref, bias_ref = (next(it) for _ in range(6))
    sink_ref = next(it) if has_sinks else None
    gate_ref = next(it) if has_gate else None
    oin_ref, sin_ref = (next(it), next(it)) if has_merge else (None, None)
    o_ref = next(it)
    st_ref = next(it) if want_stats else None
    vtbuf, sbuf, pbuf, mbuf = (next(it) for _ in range(4))
    lsebuf = next(it) if want_stats else None

    def block_of(ref, g, m, cols):
        if strided:
            return ref[:, g, m, :, cols].reshape(BLOCK, LANES)
        return ref[g, m * BLOCK:(m + 1) * BLOCK, cols]

    def key_block(cur_ref, prev_ref, g, kb, cols):
        if kb > 0:
            return block_of(cur_ref, g, kb - 1, cols)
        return block_of(cur_ref, g, 0, cols) if strided else prev_ref[g, :, cols]

    def store_block(ref, g, m, cols, val):
        if strided:
            ref[:, g, m, :, cols] = val.reshape(BLOCK // MID_RUN, MID_RUN, LANES)
        else:
            ref[g, m * BLOCK:(m + 1) * BLOCK, cols] = val

    if strided:
        first_bias = bias_ref[1]
    else:
        first_bias = bias_ref[jnp.where(pl.program_id(1) == 0, 1, 0)]

    ones = jnp.ones((ONES_ROWS, BLOCK), vtbuf.dtype)

    def transpose_values(g, kb):
        for sl in range(slabs):
            src = slice(sl * LANES, (sl + 1) * LANES)
            vtbuf[sl, g * (nblk + 1) + kb, 0:LANES, :] = key_block(vc_ref, vp_ref, g, kb, src).T
            vtbuf[sl, g * (nblk + 1) + kb, LANES:, :] = ones

    lane = lax.broadcasted_iota(jnp.int32, (1, LANES), 1)
    low = lane < HEAD_DIM
    zero = jnp.zeros((), q_ref.dtype)
    onehot = bias_ref[2]

    def sink_row(hp):
        return jnp.concatenate([jnp.full((1, LANES), sink_ref[2 * hp] * LOG2E, jnp.float32),
                                jnp.full((1, LANES), sink_ref[2 * hp + 1] * LOG2E, jnp.float32)], axis=1)

    def score_stage(t):
        slot, (g, j) = t % 2, divmod(t, nblk)
        bias_t = first_bias if j == 0 else bias_ref[0]
        for hp in range(pairs):
            sl = hp * slabs // pairs
            kcols = slice(sl * LANES, (sl + 1) * LANES)
            qp = block_of(q_ref, g, j, slice(hp * LANES, (hp + 1) * LANES))
            qm = jnp.concatenate([jnp.where(low, qp, zero), jnp.where(low, zero, qp)], axis=0)
            kw = jnp.concatenate([key_block(kc_ref, kp_ref, g, j, kcols),
                                  key_block(kc_ref, kp_ref, g, j + 1, kcols)], axis=0)
            sbuf[slot, hp] = lax.dot_general(
                jnp.concatenate([kw, bias_t], axis=1), jnp.concatenate([qm, onehot], axis=1),
                (((1,), (1,)), ((), ())), preferred_element_type=jnp.float32)

    def softmax_stage(t):
        slot, (g, j) = t % 2, divmod(t, nblk)
        if j == 0:
            transpose_values(g, 0)
        transpose_values(g, j + 1)
        for hp in range(pairs):
            m = jnp.max(sbuf[slot, hp], axis=0, keepdims=True)
            if has_sinks:
                m = jnp.maximum(m, sink_row(hp))
            pbuf[slot, hp] = jnp.exp2(sbuf[slot, hp] - m).astype(pbuf.dtype)
            mbuf[slot, hp] = m

    def value_stage(t):
        slot, (g, j) = t % 2, divmod(t, nblk)
        kb0 = g * (nblk + 1) + j
        if has_merge:
            terms = block_of(sin_ref, g, j, slice(0, LANES)).astype(jnp.float32).T
            lse_in = (terms[0:B_HEADS] + terms[B_HEADS:2 * B_HEADS] + terms[2 * B_HEADS:3 * B_HEADS])
        for hp in range(pairs):
            sl = hp * slabs // pairs
            cols = slice(hp * LANES, (hp + 1) * LANES)
            m = mbuf[slot, hp]
            vt = jnp.concatenate([vtbuf[sl, kb0], vtbuf[sl, kb0 + 1]], axis=1)
            r = _dot(vt, pbuf[slot, hp])
            l = r[LANES:LANES + 1, :]
            if has_sinks:
                l = l + jnp.exp2(sink_row(hp) - m)
            inv = 1.0 / l
            lse = m + jnp.log2(l)
            if has_merge:
                other = jnp.concatenate([lse_in[2 * hp:2 * hp + 1], lse_in[2 * hp + 1:2 * hp + 2]], axis=1)
                top = jnp.maximum(lse, other)
                mine, theirs = jnp.exp2(lse - top), jnp.exp2(other - top)
                total = mine + theirs
                inv, frac = inv * (mine / total), theirs / total
                lse = top + jnp.log2(total)
            ot = jnp.concatenate([r[0:HEAD_DIM, 0:LANES] * inv[:, 0:LANES],
                                  r[HEAD_DIM:LANES, LANES:] * inv[:, LANES:]], axis=0)
            if has_merge:
                oin_t = block_of(oin_ref, g, j, cols).T.astype(jnp.float32)
                ot = ot + oin_t * jnp.concatenate(
                    [jnp.broadcast_to(frac[:, 0:LANES], (HEAD_DIM, LANES)),
                     jnp.broadcast_to(frac[:, LANES:], (HEAD_DIM, LANES))], axis=0)
            o = ot.T
            if has_gate:
                gt = block_of(gate_ref, g, j, cols)
                o = o * (gt * jax.nn.sigmoid(gt))
            store_block(o_ref, g, j, cols, o.astype(o_ref.dtype))
            if want_stats:
                lsebuf[slot, 2 * hp:2 * hp + 1, :] = lse[:, 0:LANES]
                lsebuf[slot, 2 * hp + 1:2 * hp + 2, :] = lse[:, LANES:]
        if want_stats:
            terms = _split3(lsebuf[slot])
            pad = jnp.zeros((LANES - STAT_LANES, BLOCK), jnp.float32)
            tile = jnp.concatenate(list(terms) + [pad], axis=0)
            store_block(st_ref, g, j, slice(0, LANES), tile.T.astype(st_ref.dtype))

    jobs = group * nblk
    for step in range(jobs + 2):
        if step - 2 >= 0:
            value_stage(step - 2)
        if 0 <= step - 1 < jobs:
            softmax_stage(step - 1)
        if step < jobs:
            score_stage(step)


def _band_bias(max_dist, mid):
    r = np.arange(BLOCK)
    pos = MID_DIL * (r % MID_RUN) + r // MID_RUN if mid else r
    qpos = pos[None, :] + BLOCK
    kpos = np.concatenate([pos, pos + BLOCK])[:, None]
    dist = qpos - kpos
    valid = (dist >= 0) & (dist <= max_dist)
    first = valid & (np.arange(2 * BLOCK)[:, None] >= BLOCK)
    masks = np.where(np.stack([valid, first]), 0.0, NEG)
    onehot = np.tile(np.eye(BLOCK), (2, 1))
    return jnp.asarray(np.concatenate([masks, onehot[None]]), jnp.bfloat16)


def _attn_scratch(pairs, slabs, key_blocks, want_stats):
    scratch = [pltpu.VMEM((slabs, key_blocks, LANES + ONES_ROWS, BLOCK), jnp.bfloat16),
               pltpu.VMEM((2, pairs, 2 * BLOCK, 2 * LANES), jnp.float32),
               pltpu.VMEM((2, pairs, 2 * BLOCK, 2 * LANES), jnp.bfloat16),
               pltpu.VMEM((2, pairs, 1, 2 * LANES), jnp.float32)]
    if want_stats:
        scratch.append(pltpu.VMEM((2, 2 * pairs, BLOCK), jnp.float32))
    return scratch


def _attn_call(q, k, v, max_dist, *, sinks=None, gate=None, merge=None, want_stats=False, name):
    n, seq, qw = q.shape
    kvw = k.shape[-1]
    pairs, slabs = qw // LANES, kvw // LANES
    tq = min(seq, ATTN_BLOCKS * BLOCK)
    sub = tq // BLOCK
    group = ATTN_BLOCKS // sub
    cur = lambda w: pl.BlockSpec((group, tq, w), lambda b, i: (b, i, 0))
    prev = lambda w: pl.BlockSpec((group, BLOCK, w), lambda b, i: (b, jnp.maximum(i * sub - 1, 0), 0))
    bias = _band_bias(max_dist, mid=False)
    args = [q, k, k, v, v, bias]
    in_specs = [cur(qw), cur(kvw), prev(kvw), cur(kvw), prev(kvw),
                pl.BlockSpec(bias.shape, lambda b, i: (0, 0, 0))]
    if sinks is not None:
        args.append(sinks)
        in_specs.append(pl.BlockSpec(memory_space=pltpu.SMEM))
    if gate is not None:
        args.append(gate)
        in_specs.append(cur(qw))
    if merge is not None:
        args.extend(merge)
        in_specs.extend([cur(qw), cur(LANES)])
    out_shape = [jax.ShapeDtypeStruct((n, seq, qw), jnp.bfloat16)]
    out_specs = [cur(qw)]
    if want_stats:
        out_shape.append(jax.ShapeDtypeStruct((n, seq, LANES), jnp.bfloat16))
        out_specs.append(cur(LANES))
    kernel = functools.partial(_attn_kernel, pairs=pairs, slabs=slabs, group=group, nblk=sub,
                               strided=False, has_sinks=sinks is not None, has_gate=gate is not None,
                               has_merge=merge is not None, want_stats=want_stats)
    return pl.pallas_call(
        kernel,
        grid=(n // group, seq // tq),
        in_specs=in_specs,
        out_specs=out_specs,
        out_shape=out_shape,
        scratch_shapes=_attn_scratch(pairs, slabs, group * (sub + 1), want_stats),
        compiler_params=pltpu.CompilerParams(
            dimension_semantics=("parallel", "arbitrary"), vmem_limit_bytes=VMEM_LIMIT),
        name=name,
    )(*args)


def _attn_mid_call(q, k, v, max_dist, *, name):
    b, classes, per_class, w = q.shape
    outer = classes // MID_DIL
    nblk = per_class // MID_RUN
    group = max(1, min(MID_DIL, ATTN_BLOCKS // nblk))
    view = lambda t: t.reshape(b, outer, MID_DIL, nblk, MID_RUN, t.shape[-1])
    spec = lambda width: pl.BlockSpec((None, outer, group, nblk, MID_RUN, width),
                                      lambda i, e: (i, 0, e, 0, 0, 0))
    pairs = w // LANES
    bias = _band_bias(max_dist, mid=True)
    shape6 = (b, outer, MID_DIL, nblk, MID_RUN)
    kernel = functools.partial(_attn_kernel, pairs=pairs, slabs=pairs, group=group, nblk=nblk,
                               strided=True, has_sinks=False, has_gate=False, has_merge=False,
                               want_stats=True)
    o, st = pl.pallas_call(
        kernel,
        grid=(b, MID_DIL // group),
        in_specs=[spec(w), spec(w), spec(w), pl.BlockSpec(bias.shape, lambda i, e: (0, 0, 0))],
        out_specs=[spec(w), spec(LANES)],
        out_shape=[jax.ShapeDtypeStruct(shape6 + (w,), jnp.bfloat16),
                   jax.ShapeDtypeStruct(shape6 + (LANES,), jnp.bfloat16)],
        scratch_shapes=_attn_scratch(pairs, pairs, group * (nblk + 1), True),
        compiler_params=pltpu.CompilerParams(
            dimension_semantics=("parallel", "arbitrary"), vmem_limit_bytes=VMEM_LIMIT),
        name=name,
    )(view(q), view(k), view(v), bias)
    return o.reshape(b, classes, per_class, w), st.reshape(b, classes, per_class, LANES)


def _expand_heads(w, expand):
    hi, mid, lo = _split3(w)
    packed = hi + pltpu.roll(mid, B_HEADS, 1) + pltpu.roll(lo, 2 * B_HEADS, 1)
    return _dot(packed.astype(jnp.bfloat16), expand)


def _stat_sum(st):
    return (st + pltpu.roll(st, LANES - B_HEADS, 1) + pltpu.roll(st, LANES - 2 * B_HEADS, 1))


def _out_kernel(x_ref, ma_ref, o1_ref, s1_ref, oc_ref, sc_ref, gb_ref,
                permt_ref, expand_ref, w_ref, out_ref):
    rows = x_ref.shape[0]
    permt = permt_ref[...]

    def to_token_order(o_ref, s_ref):
        outs, stats = [], []
        for j in range(rows // LOCAL):
            sel = slice(j * BF16_ROWS, (j + 1) * BF16_ROWS)
            y = jnp.concatenate([jnp.concatenate([o_ref[c, sel, :] for c in range(CLASSES)], axis=0),
                                 jnp.concatenate([s_ref[c, sel, :] for c in range(CLASSES)], axis=0)],
                                axis=1)
            u = _dot(permt, y)
            outs.append(u[:, :B_WIDTH])
            stats.append(u[:, B_WIDTH:])
        return jnp.concatenate(outs, axis=0), jnp.concatenate(stats, axis=0)

    o1 = o1_ref[...].astype(jnp.float32)
    l1 = _stat_sum(s1_ref[...].astype(jnp.float32))
    oc, sc = to_token_order(oc_ref, sc_ref)
    lc = _stat_sum(sc)

    top = jnp.maximum(l1, lc)
    e1, ec = jnp.exp2(l1 - top), jnp.exp2(lc - top)
    head_lane = lax.broadcasted_iota(jnp.int32, (1, LANES), 1) < B_HEADS
    w1 = jnp.where(head_lane, e1 / (e1 + ec), 0.0)
    ob = oc + _expand_heads(w1, expand_ref[...]) * (o1 - oc)
    gb = gb_ref[...]
    mb = (ob * (gb * jax.nn.sigmoid(gb))).astype(jnp.bfloat16)
    out_ref[...] = (x_ref[...] + _dot(ma_ref[...], w_ref[0:A_WIDTH, :])
                    + _dot(mb, w_ref[A_WIDTH:, :]))


def _out_call(x2d, mixed_a, o1, s1, oc, sc, gate_b, permt, expand, w_bf, seq):
    tokens, d_model = x2d.shape
    tm = PROJ_ROWS
    steps_per_seq = seq // tm
    row = lambda w: pl.BlockSpec((tm, w), lambda g: (g, 0))
    full = lambda a: pl.BlockSpec(a.shape, lambda g: (0,) * a.ndim)
    cmaj = lambda w: pl.BlockSpec((None, CLASSES, tm // CLASSES, w),
                                  lambda g: (g // steps_per_seq, 0, g % steps_per_seq, 0))
    return pl.pallas_call(
        _out_kernel,
        grid=(tokens // tm,),
        in_specs=[row(d_model), row(A_WIDTH), row(B_WIDTH), row(LANES),
                  cmaj(B_WIDTH), cmaj(LANES),
                  row(B_WIDTH), full(permt), full(expand), full(w_bf)],
        out_specs=row(d_model),
        out_shape=jax.ShapeDtypeStruct((tokens, d_model), jnp.float32),
        compiler_params=pltpu.CompilerParams(
            dimension_semantics=("parallel",), vmem_limit_bytes=VMEM_LIMIT),
        name="merge_out",
    )(x2d, mixed_a, o1, s1, oc, sc, gate_b, permt, expand, w_bf)


def _rope_tables(seq):
    inv = ROPE_THETA ** (-jnp.arange(HALF, dtype=jnp.float32) / HALF)
    ang = jnp.arange(seq).astype(jnp.float32)[:, None] * inv[None, :]
    reps = LANES // HALF
    cos_t = jnp.tile(jnp.cos(ang), (1, reps))
    sign = jnp.tile(jnp.concatenate([-jnp.ones((HALF,), jnp.float32), jnp.ones((HALF,), jnp.float32)]),
                    LANES // HEAD_DIM)
    sin_t = jnp.tile(jnp.sin(ang), (1, reps)) * sign[None, :]
    return cos_t, sin_t


def kernel(x, norm_gain, w_in, q_norm_a, k_norm_a, sinks_a, q_norm_b, k_norm_b, w_out):
    b, seq, d_model = x.shape
    tokens = b * seq
    depth = norm_gain.shape[0]
    assert [d for _, d in B_PATTERNS] == [1, MID_DIL, CLASSES]
    assert seq % PROJ_ROWS == 0 and PROJ_ROWS % LOCAL == 0 and (seq // CLASSES) % BLOCK == 0
    scale = HEAD_DIM ** -0.5 * LOG2E
    cos_t, sin_t = _rope_tables(seq)
    seg_i = np.arange(SEG_TILE) // HEAD_DIM
    seg = jnp.asarray(seg_i[:, None] == seg_i[None, :], jnp.bfloat16)
    r = np.arange(LOCAL)
    perm_np = (CLASSES * (r % BF16_ROWS) + r // BF16_ROWS)[:, None] == r[None, :]
    perm = jnp.asarray(perm_np, jnp.bfloat16)
    permt = jnp.asarray(perm_np.T, jnp.bfloat16)
    term_head = np.where(np.arange(LANES) < STAT_LANES, np.arange(LANES) % B_HEADS, -1)
    expand = jnp.asarray(term_head[:, None] == (np.arange(B_WIDTH) // HEAD_DIM)[None, :], jnp.bfloat16)

    for i in range(depth):
        x2d = x.reshape(tokens, d_model)
        tile = lambda g, n, s=1.0: jnp.tile(g * s, n)[None, :]
        qa, ka, va, ga, qb, kb, vb, gb, qc, kc, vc = _proj_call(
            x2d, norm_gain[i][None, :], w_in[i].astype(jnp.bfloat16), seg, perm, cos_t, sin_t,
            tile(q_norm_a[i], A_HEADS, scale), tile(k_norm_a[i], A_KV_HEADS),
            tile(q_norm_b[i], B_HEADS, scale), tile(k_norm_b[i], B_HEADS), b, seq)

        r3 = lambda t: t.reshape(b, seq, t.shape[-1])
        (mixed_a,) = _attn_call(r3(qa), r3(ka), r3(va), A_WINDOW - 1,
                                sinks=sinks_a[i], gate=r3(ga), name="attn_a")

        (w1, d1), (w4, d4), (w16, d16) = B_PATTERNS
        o1, s1 = _attn_call(r3(qb), r3(kb), r3(vb), w1 // d1, want_stats=True, name="attn_b1")
        o4, s4 = _attn_mid_call(qc, kc, vc, w4 // d4, name="attn_b4")
        fold = lambda t: t.reshape(b * CLASSES, seq // CLASSES, t.shape[-1])
        oc, sc = _attn_call(fold(qc), fold(kc), fold(vc), w16 // d16, merge=(fold(o4), fold(s4)),
                            want_stats=True, name="attn_b16")
        unfold = lambda t: t.reshape(b, CLASSES, seq // CLASSES, t.shape[-1])

        out = _out_call(x2d, mixed_a.reshape(tokens, A_WIDTH),
                        o1.reshape(tokens, B_WIDTH), s1.reshape(tokens, LANES),
                        unfold(oc), unfold(sc), gb, permt, expand,
                        w_out[i].astype(jnp.bfloat16), seq)
        x = out.reshape(b, seq, d_model)
    return x
```

```python
import functools

import numpy as np
import jax
import jax.numpy as jnp
from jax import lax
from jax.experimental import pallas as pl
from jax.experimental.pallas import tpu as pltpu

HEAD_DIM = 64
HALF = HEAD_DIM // 2
A_HEADS = 8
A_KV_HEADS = 2
A_WINDOW = 128
B_HEADS = 8
B_PATTERNS = ((128, 1), (512, 4), (2048, 16))
BLOCK = 128
ROPE_THETA = 10000.0
EPS = 1e-6
NEG = -1e30
LOG2E = 1.4426950408889634

A_WIDTH = A_HEADS * HEAD_DIM
A_KV_WIDTH = A_KV_HEADS * HEAD_DIM
B_WIDTH = B_HEADS * HEAD_DIM

LANES = 128
BF16_ROWS = 16
SEG_TILE = 256
PROJ_ROWS = 512
VMEM_LIMIT = 56 * 1024 * 1024

A_KV_PAIR_WIDTH = 2 * A_KV_HEADS * HEAD_DIM
assert A_KV_WIDTH == LANES

CLASSES = max(d for _, d in B_PATTERNS)
LOCAL = CLASSES * BF16_ROWS
MID_DIL = 4
MID_RUN = BLOCK // MID_DIL
STAT_LANES = 3 * B_HEADS


def _dot(a, b):
    return jnp.dot(a, b, preferred_element_type=jnp.float32)


def _split3(v):
    hi = v.astype(jnp.bfloat16).astype(jnp.float32)
    r1 = v - hi
    mid = r1.astype(jnp.bfloat16).astype(jnp.float32)
    lo = (r1 - mid).astype(jnp.bfloat16).astype(jnp.float32)
    return hi, mid, lo


def _head_norm_rope(p, seg, gain, cos, sin, hi_mask):
    width = p.shape[-1]
    sq = (p * p).astype(jnp.bfloat16)
    step = min(width, SEG_TILE)
    sums = []
    for c in range(0, width, step):
        sums.append(_dot(sq[:, c:c + step], seg[:step, :step]))
    ss = sums[0] if len(sums) == 1 else jnp.concatenate(sums, axis=-1)
    y = p * lax.rsqrt(ss * (1.0 / HEAD_DIM) + EPS) * gain
    outs = []
    for c in range(0, width, LANES):
        yc = y[:, c:c + LANES]
        partner = jnp.where(hi_mask, pltpu.roll(yc, HALF, 1), pltpu.roll(yc, LANES - HALF, 1))
        outs.append(yc * cos + partner * sin)
    return outs[0] if len(outs) == 1 else jnp.concatenate(outs, axis=-1)


def _proj_kernel(x_ref, gain_ref, w_ref, seg_ref, perm_ref, cos_ref, sin_ref,
                 gqa_ref, gka_ref, gqb_ref, gkb_ref,
                 qa_ref, ka_ref, va_ref, ga_ref, qb_ref, kb_ref, vb_ref, gb_ref,
                 qc_ref, kc_ref, vc_ref, pbuf):
    xf = x_ref[...]
    ms = jnp.mean(xf * xf, axis=-1, keepdims=True)
    h = (xf * lax.rsqrt(ms + EPS) * gain_ref[...]).astype(jnp.bfloat16)
    seg = seg_ref[...]
    perm = perm_ref[...]
    cos = cos_ref[...]
    sin = sin_ref[...]
    lane = lax.broadcasted_iota(jnp.int32, (1, LANES), 1)
    hi_mask = (lane & HALF) != 0
    low = lane < HEAD_DIM
    bf = jnp.bfloat16

    def per_pair(y):
        swapped = pltpu.roll(y, HEAD_DIM, 1)
        return jnp.concatenate([jnp.where(low, y, swapped), jnp.where(low, swapped, y)], axis=-1)

    def normed(gain_ref_):
        return lambda p: _head_norm_rope(p, seg, gain_ref_[...], cos, sin, hi_mask)

    plain = lambda p: p
    groups = [
        [(A_WIDTH, normed(gqa_ref), qa_ref, None)],
        [(A_KV_WIDTH, lambda p: per_pair(normed(gka_ref)(p)), ka_ref, None),
         (A_KV_WIDTH, per_pair, va_ref, None)],
        [(A_WIDTH, plain, ga_ref, None)],
        [(B_WIDTH, normed(gqb_ref), qb_ref, qc_ref)],
        [(B_WIDTH, normed(gkb_ref), kb_ref, kc_ref)],
        [(B_WIDTH, plain, vb_ref, vc_ref)],
        [(B_WIDTH, plain, gb_ref, None)],
    ]
    widths = [sum(part[0] for part in g) for g in groups]
    starts = np.cumsum([0] + widths)

    def project(i):
        pbuf[i % 2, :, 0:widths[i]] = _dot(h, w_ref[:, starts[i]:starts[i] + widths[i]])

    def finish(i):
        lo = 0
        for width, epilogue, nat_ref, _ in groups[i]:
            nat_ref[...] = epilogue(pbuf[i % 2, :, lo:lo + width]).astype(nat_ref.dtype)
            lo += width

    def to_class_major(i):
        _, _, nat_ref, cm_ref = groups[i][0]
        if cm_ref is None:
            return
        for j in range(nat_ref.shape[0] // LOCAL):
            z = _dot(perm, nat_ref[j * LOCAL:(j + 1) * LOCAL, :]).astype(bf)
            for c in range(CLASSES):
                cm_ref[c, j * BF16_ROWS:(j + 1) * BF16_ROWS, :] = z[c * BF16_ROWS:(c + 1) * BF16_ROWS, :]

    n = len(groups)
    for step in range(n + 2):
        if step < n:
            project(step)
        if 0 <= step - 1 < n:
            finish(step - 1)
        if 0 <= step - 2 < n:
            to_class_major(step - 2)


def _proj_call(x2d, gain, w_bf, seg, perm, cos_t, sin_t, gqa, gka, gqb, gkb, batch, seq):
    tokens, d_model = x2d.shape
    tm = PROJ_ROWS
    steps_per_seq = seq // tm
    row = lambda w: pl.BlockSpec((tm, w), lambda g: (g, 0))
    full = lambda a: pl.BlockSpec(a.shape, lambda g: (0,) * a.ndim)
    table = pl.BlockSpec((tm, LANES), lambda g: (g % steps_per_seq, 0))
    cmaj = pl.BlockSpec((None, CLASSES, tm // CLASSES, B_WIDTH),
                        lambda g: (g // steps_per_seq, 0, g % steps_per_seq, 0))
    bf = jnp.bfloat16
    cm_shape = jax.ShapeDtypeStruct((batch, CLASSES, seq // CLASSES, B_WIDTH), bf)
    out_shape = [
        jax.ShapeDtypeStruct((tokens, A_WIDTH), bf),
        jax.ShapeDtypeStruct((tokens, A_KV_PAIR_WIDTH), bf),
        jax.ShapeDtypeStruct((tokens, A_KV_PAIR_WIDTH), bf),
        jax.ShapeDtypeStruct((tokens, A_WIDTH), jnp.float32),
        jax.ShapeDtypeStruct((tokens, B_WIDTH), bf),
        jax.ShapeDtypeStruct((tokens, B_WIDTH), bf),
        jax.ShapeDtypeStruct((tokens, B_WIDTH), bf),
        jax.ShapeDtypeStruct((tokens, B_WIDTH), jnp.float32),
        cm_shape, cm_shape, cm_shape,
    ]
    return pl.pallas_call(
        _proj_kernel,
        grid=(tokens // tm,),
        in_specs=[row(d_model), full(gain), full(w_bf), full(seg), full(perm), table, table,
                  full(gqa), full(gka), full(gqb), full(gkb)],
        out_specs=[row(s.shape[1]) for s in out_shape[:8]] + [cmaj] * 3,
        out_shape=out_shape,
        scratch_shapes=[pltpu.VMEM((2, tm, max(A_WIDTH, B_WIDTH)), jnp.float32)],
        compiler_params=pltpu.CompilerParams(
            dimension_semantics=("parallel",), vmem_limit_bytes=VMEM_LIMIT),
        name="proj",
    )(x2d, gain, w_bf, seg, perm, cos_t, sin_t, gqa, gka, gqb, gkb)


ONES_ROWS = BF16_ROWS
ATTN_BLOCKS = 16


def _attn_kernel(*refs, pairs, slabs, group, nblk, strided, has_sinks, has_gate, has_merge, want_stats):
    it = iter(refs)
    if strided:
        q_ref, kc_ref, vc_ref, bias_ref = (next(it) for _ in range(4))
        kp_ref = vp_ref = None
    else:
        q_ref, kc_ref, kp_ref, vc_ref, vp_ref, bias_ref = (next(it) for _ in range(6))
    sink_ref = next(it) if has_sinks else None
    gate_ref = next(it) if has_gate else None
    oin_ref, sin_ref = (next(it), next(it)) if has_merge else (None, None)
    o_ref = next(it)
    st_ref = next(it) if want_stats else None
    vtbuf, sbuf, pbuf, mbuf = (next(it) for _ in range(4))
    lsebuf = next(it) if want_stats else None

    def block_of(ref, g, m, cols):
        if strided:
            return ref[:, g, m, :, cols].reshape(BLOCK, LANES)
        return ref[g, m * BLOCK:(m + 1) * BLOCK, cols]

    def key_block(cur_ref, prev_ref, g, kb, cols):
        if kb > 0:
            return block_of(cur_ref, g, kb - 1, cols)
        return block_of(cur_ref, g, 0, cols) if strided else prev_ref[g, :, cols]

    def store_block(ref, g, m, cols, val):
        if strided:
            ref[:, g, m, :, cols] = val.reshape(BLOCK // MID_RUN, MID_RUN, LANES)
        else:
            ref[g, m * BLOCK:(m + 1) * BLOCK, cols] = val

    if strided:
        first_bias = bias_ref[1]
    else:
        first_bias = bias_ref[jnp.where(pl.program_id(1) == 0, 1, 0)]

    ones = jnp.ones((ONES_ROWS, BLOCK), vtbuf.dtype)

    def transpose_values(g, kb):
        for sl in range(slabs):
            src = slice(sl * LANES, (sl + 1) * LANES)
            vtbuf[sl, g * (nblk + 1) + kb, 0:LANES, :] = key_block(vc_ref, vp_ref, g, kb, src).T
            vtbuf[sl, g * (nblk + 1) + kb, LANES:, :] = ones

    transpose_q = not has_merge
    low = lax.broadcasted_iota(jnp.int32, (1, LANES), 1) < HEAD_DIM
    zero = jnp.zeros((), q_ref.dtype)
    zeros = jnp.zeros((HEAD_DIM, BLOCK), q_ref.dtype)
    onehot_rows = bias_ref[2]
    onehot = jnp.concatenate([bias_ref[2, 0:BLOCK], bias_ref[2, 0:BLOCK]], axis=1)

    def sink_row(hp):
        return jnp.concatenate([jnp.full((1, LANES), sink_ref[2 * hp] * LOG2E, jnp.float32),
                                jnp.full((1, LANES), sink_ref[2 * hp + 1] * LOG2E, jnp.float32)], axis=1)

    def score_stage(t):
        slot, (g, j) = t % 2, divmod(t, nblk)
        bias_t = first_bias if j == 0 else bias_ref[0]
        for hp in range(pairs):
            sl = hp * slabs // pairs
            kcols = slice(sl * LANES, (sl + 1) * LANES)
            qp = block_of(q_ref, g, j, slice(hp * LANES, (hp + 1) * LANES))
            kw = jnp.concatenate([key_block(kc_ref, kp_ref, g, j, kcols),
                                  key_block(kc_ref, kp_ref, g, j + 1, kcols)], axis=0)
            lhs = jnp.concatenate([kw, bias_t], axis=1)
            if transpose_q:
                qt = qp.T
                qm = jnp.concatenate([jnp.concatenate([qt[0:HEAD_DIM], zeros], axis=0),
                                      jnp.concatenate([zeros, qt[HEAD_DIM:]], axis=0)], axis=1)
                sbuf[slot, hp] = _dot(lhs, jnp.concatenate([qm, onehot], axis=0))
            else:
                qm = jnp.concatenate([jnp.where(low, qp, zero), jnp.where(low, zero, qp)], axis=0)
                sbuf[slot, hp] = lax.dot_general(
                    lhs, jnp.concatenate([qm, onehot_rows], axis=1),
                    (((1,), (1,)), ((), ())), preferred_element_type=jnp.float32)

    def softmax_stage(t):
        slot, (g, j) = t % 2, divmod(t, nblk)
        if j == 0:
            transpose_values(g, 0)
        transpose_values(g, j + 1)
        for hp in range(pairs):
            m = jnp.max(sbuf[slot, hp], axis=0, keepdims=True)
            if has_sinks:
                m = jnp.maximum(m, sink_row(hp))
            pbuf[slot, hp] = jnp.exp2(sbuf[slot, hp] - m).astype(pbuf.dtype)
            mbuf[slot, hp] = m

    def value_stage(t):
        slot, (g, j) = t % 2, divmod(t, nblk)
        kb0 = g * (nblk + 1) + j
        if has_merge:
            terms = block_of(sin_ref, g, j, slice(0, LANES)).astype(jnp.float32).T
            lse_in = (terms[0:B_HEADS] + terms[B_HEADS:2 * B_HEADS] + terms[2 * B_HEADS:3 * B_HEADS])
        for hp in range(pairs):
            sl = hp * slabs // pairs
            cols = slice(hp * LANES, (hp + 1) * LANES)
            m = mbuf[slot, hp]
            vt = jnp.concatenate([vtbuf[sl, kb0], vtbuf[sl, kb0 + 1]], axis=1)
            r = _dot(vt, pbuf[slot, hp])
            l = r[LANES:LANES + 1, :]
            if has_sinks:
                l = l + jnp.exp2(sink_row(hp) - m)
            inv = 1.0 / l
            lse = m + jnp.log2(l)
            if has_merge:
                other = jnp.concatenate([lse_in[2 * hp:2 * hp + 1], lse_in[2 * hp + 1:2 * hp + 2]], axis=1)
                top = jnp.maximum(lse, other)
                mine, theirs = jnp.exp2(lse - top), jnp.exp2(other - top)
                total = mine + theirs
                inv, frac = inv * (mine / total), theirs / total
                lse = top + jnp.log2(total)
            ot = jnp.concatenate([r[0:HEAD_DIM, 0:LANES] * inv[:, 0:LANES],
                                  r[HEAD_DIM:LANES, LANES:] * inv[:, LANES:]], axis=0)
            if has_merge:
                oin_t = block_of(oin_ref, g, j, cols).T.astype(jnp.float32)
                ot = ot + oin_t * jnp.concatenate(
                    [jnp.broadcast_to(frac[:, 0:LANES], (HEAD_DIM, LANES)),
                     jnp.broadcast_to(frac[:, LANES:], (HEAD_DIM, LANES))], axis=0)
            o = ot.T
            if has_gate:
                gt = block_of(gate_ref, g, j, cols)
                o = o * (gt * jax.nn.sigmoid(gt))
            store_block(o_ref, g, j, cols, o.astype(o_ref.dtype))
            if want_stats:
                lsebuf[slot, 2 * hp:2 * hp + 1, :] = lse[:, 0:LANES]
                lsebuf[slot, 2 * hp + 1:2 * hp + 2, :] = lse[:, LANES:]
        if want_stats:
            terms = _split3(lsebuf[slot])
            pad = jnp.zeros((LANES - STAT_LANES, BLOCK), jnp.float32)
            tile = jnp.concatenate(list(terms) + [pad], axis=0)
            store_block(st_ref, g, j, slice(0, LANES), tile.T.astype(st_ref.dtype))

    jobs = group * nblk
    for step in range(jobs + 2):
        if step - 2 >= 0:
            value_stage(step - 2)
        if 0 <= step - 1 < jobs:
            softmax_stage(step - 1)
        if step < jobs:
            score_stage(step)


def _band_bias(max_dist, mid):
    r = np.arange(BLOCK)
    pos = MID_DIL * (r % MID_RUN) + r // MID_RUN if mid else r
    qpos = pos[None, :] + BLOCK
    kpos = np.concatenate([pos, pos + BLOCK])[:, None]
    dist = qpos - kpos
    valid = (dist >= 0) & (dist <= max_dist)
    first = valid & (np.arange(2 * BLOCK)[:, None] >= BLOCK)
    masks = np.where(np.stack([valid, first]), 0.0, NEG)
    onehot = np.tile(np.eye(BLOCK), (2, 1))
    return jnp.asarray(np.concatenate([masks, onehot[None]]), jnp.bfloat16)


def _attn_scratch(pairs, slabs, key_blocks, want_stats):
    scratch = [pltpu.VMEM((slabs, key_blocks, LANES + ONES_ROWS, BLOCK), jnp.bfloat16),
               pltpu.VMEM((2, pairs, 2 * BLOCK, 2 * LANES), jnp.float32),
               pltpu.VMEM((2, pairs, 2 * BLOCK, 2 * LANES), jnp.bfloat16),
               pltpu.VMEM((2, pairs, 1, 2 * LANES), jnp.float32)]
    if want_stats:
        scratch.append(pltpu.VMEM((2, 2 * pairs, BLOCK), jnp.float32))
    return scratch


def _attn_call(q, k, v, max_dist, *, sinks=None, gate=None, merge=None, want_stats=False, name):
    n, seq, qw = q.shape
    kvw = k.shape[-1]
    pairs, slabs = qw // LANES, kvw // LANES
    tq = min(seq, ATTN_BLOCKS * BLOCK)
    sub = tq // BLOCK
    group = ATTN_BLOCKS // sub
    cur = lambda w: pl.BlockSpec((group, tq, w), lambda b, i: (b, i, 0))
    prev = lambda w: pl.BlockSpec((group, BLOCK, w), lambda b, i: (b, jnp.maximum(i * sub - 1, 0), 0))
    bias = _band_bias(max_dist, mid=False)
    args = [q, k, k, v, v, bias]
    in_specs = [cur(qw), cur(kvw), prev(kvw), cur(kvw), prev(kvw),
                pl.BlockSpec(bias.shape, lambda b, i: (0, 0, 0))]
    if sinks is not None:
        args.append(sinks)
        in_specs.append(pl.BlockSpec(memory_space=pltpu.SMEM))
    if gate is not None:
        args.append(gate)
        in_specs.append(cur(qw))
    if merge is not None:
        args.extend(merge)
        in_specs.extend([cur(qw), cur(LANES)])
    out_shape = [jax.ShapeDtypeStruct((n, seq, qw), jnp.bfloat16)]
    out_specs = [cur(qw)]
    if want_stats:
        out_shape.append(jax.ShapeDtypeStruct((n, seq, LANES), jnp.bfloat16))
        out_specs.append(cur(LANES))
    kernel = functools.partial(_attn_kernel, pairs=pairs, slabs=slabs, group=group, nblk=sub,
                               strided=False, has_sinks=sinks is not None, has_gate=gate is not None,
                               has_merge=merge is not None, want_stats=want_stats)
    return pl.pallas_call(
        kernel,
        grid=(n // group, seq // tq),
        in_specs=in_specs,
        out_specs=out_specs,
        out_shape=out_shape,
        scratch_shapes=_attn_scratch(pairs, slabs, group * (sub + 1), want_stats),
        compiler_params=pltpu.CompilerParams(
            dimension_semantics=("parallel", "arbitrary"), vmem_limit_bytes=VMEM_LIMIT),
        name=name,
    )(*args)


def _attn_mid_call(q, k, v, max_dist, *, name):
    b, classes, per_class, w = q.shape
    outer = classes // MID_DIL
    nblk = per_class // MID_RUN
    group = max(1, min(MID_DIL, ATTN_BLOCKS // nblk))
    view = lambda t: t.reshape(b, outer, MID_DIL, nblk, MID_RUN, t.shape[-1])
    spec = lambda width: pl.BlockSpec((None, outer, group, nblk, MID_RUN, width),
                                      lambda i, e: (i, 0, e, 0, 0, 0))
    pairs = w // LANES
    bias = _band_bias(max_dist, mid=True)
    shape6 = (b, outer, MID_DIL, nblk, MID_RUN)
    kernel = functools.partial(_attn_kernel, pairs=pairs, slabs=pairs, group=group, nblk=nblk,
                               strided=True, has_sinks=False, has_gate=False, has_merge=False,
                               want_stats=True)
    o, st = pl.pallas_call(
        kernel,
        grid=(b, MID_DIL // group),
        in_specs=[spec(w), spec(w), spec(w), pl.BlockSpec(bias.shape, lambda i, e: (0, 0, 0))],
        out_specs=[spec(w), spec(LANES)],
        out_shape=[jax.ShapeDtypeStruct(shape6 + (w,), jnp.bfloat16),
                   jax.ShapeDtypeStruct(shape6 + (LANES,), jnp.bfloat16)],
        scratch_shapes=_attn_scratch(pairs, pairs, group * (nblk + 1), True),
        compiler_params=pltpu.CompilerParams(
            dimension_semantics=("parallel", "arbitrary"), vmem_limit_bytes=VMEM_LIMIT),
        name=name,
    )(view(q), view(k), view(v), bias)
    return o.reshape(b, classes, per_class, w), st.reshape(b, classes, per_class, LANES)


def _expand_heads(w, expand):
    hi, mid, lo = _split3(w)
    packed = hi + pltpu.roll(mid, B_HEADS, 1) + pltpu.roll(lo, 2 * B_HEADS, 1)
    return _dot(packed.astype(jnp.bfloat16), expand)


def _stat_sum(st):
    return (st + pltpu.roll(st, LANES - B_HEADS, 1) + pltpu.roll(st, LANES - 2 * B_HEADS, 1))


def _out_kernel(x_ref, ma_ref, o1_ref, s1_ref, oc_ref, sc_ref, gb_ref,
                permt_ref, expand_ref, w_ref, out_ref):
    rows = x_ref.shape[0]
    permt = permt_ref[...]

    def to_token_order(o_ref, s_ref):
        outs, stats = [], []
        for j in range(rows // LOCAL):
            sel = slice(j * BF16_ROWS, (j + 1) * BF16_ROWS)
            y = jnp.concatenate([jnp.concatenate([o_ref[c, sel, :] for c in range(CLASSES)], axis=0),
                                 jnp.concatenate([s_ref[c, sel, :] for c in range(CLASSES)], axis=0)],
                                axis=1)
            u = _dot(permt, y)
            outs.append(u[:, :B_WIDTH])
            stats.append(u[:, B_WIDTH:])
        return jnp.concatenate(outs, axis=0), jnp.concatenate(stats, axis=0)

    o1 = o1_ref[...].astype(jnp.float32)
    l1 = _stat_sum(s1_ref[...].astype(jnp.float32))
    oc, sc = to_token_order(oc_ref, sc_ref)
    lc = _stat_sum(sc)

    top = jnp.maximum(l1, lc)
    e1, ec = jnp.exp2(l1 - top), jnp.exp2(lc - top)
    head_lane = lax.broadcasted_iota(jnp.int32, (1, LANES), 1) < B_HEADS
    w1 = jnp.where(head_lane, e1 / (e1 + ec), 0.0)
    ob = oc + _expand_heads(w1, expand_ref[...]) * (o1 - oc)
    gb = gb_ref[...]
    mb = (ob * (gb * jax.nn.sigmoid(gb))).astype(jnp.bfloat16)
    out_ref[...] = (x_ref[...] + _dot(ma_ref[...], w_ref[0:A_WIDTH, :])
                    + _dot(mb, w_ref[A_WIDTH:, :]))


def _out_call(x2d, mixed_a, o1, s1, oc, sc, gate_b, permt, expand, w_bf, seq):
    tokens, d_model = x2d.shape
    tm = PROJ_ROWS
    steps_per_seq = seq // tm
    row = lambda w: pl.BlockSpec((tm, w), lambda g: (g, 0))
    full = lambda a: pl.BlockSpec(a.shape, lambda g: (0,) * a.ndim)
    cmaj = lambda w: pl.BlockSpec((None, CLASSES, tm // CLASSES, w),
                                  lambda g: (g // steps_per_seq, 0, g % steps_per_seq, 0))
    return pl.pallas_call(
        _out_kernel,
        grid=(tokens // tm,),
        in_specs=[row(d_model), row(A_WIDTH), row(B_WIDTH), row(LANES),
                  cmaj(B_WIDTH), cmaj(LANES),
                  row(B_WIDTH), full(permt), full(expand), full(w_bf)],
        out_specs=row(d_model),
        out_shape=jax.ShapeDtypeStruct((tokens, d_model), jnp.float32),
        compiler_params=pltpu.CompilerParams(
            dimension_semantics=("parallel",), vmem_limit_bytes=VMEM_LIMIT),
        name="merge_out",
    )(x2d, mixed_a, o1, s1, oc, sc, gate_b, permt, expand, w_bf)


def _rope_tables(seq):
    inv = ROPE_THETA ** (-jnp.arange(HALF, dtype=jnp.float32) / HALF)
    ang = jnp.arange(seq).astype(jnp.float32)[:, None] * inv[None, :]
    reps = LANES // HALF
    cos_t = jnp.tile(jnp.cos(ang), (1, reps))
    sign = jnp.tile(jnp.concatenate([-jnp.ones((HALF,), jnp.float32), jnp.ones((HALF,), jnp.float32)]),
                    LANES // HEAD_DIM)
    sin_t = jnp.tile(jnp.sin(ang), (1, reps)) * sign[None, :]
    return cos_t, sin_t


def kernel(x, norm_gain, w_in, q_norm_a, k_norm_a, sinks_a, q_norm_b, k_norm_b, w_out):
    b, seq, d_model = x.shape
    tokens = b * seq
    depth = norm_gain.shape[0]
    assert [d for _, d in B_PATTERNS] == [1, MID_DIL, CLASSES]
    assert seq % PROJ_ROWS == 0 and PROJ_ROWS % LOCAL == 0 and (seq // CLASSES) % BLOCK == 0
    scale = HEAD_DIM ** -0.5 * LOG2E
    cos_t, sin_t = _rope_tables(seq)
    seg_i = np.arange(SEG_TILE) // HEAD_DIM
    seg = jnp.asarray(seg_i[:, None] == seg_i[None, :], jnp.bfloat16)
    r = np.arange(LOCAL)
    perm_np = (CLASSES * (r % BF16_ROWS) + r // BF16_ROWS)[:, None] == r[None, :]
    perm = jnp.asarray(perm_np, jnp.bfloat16)
    permt = jnp.asarray(perm_np.T, jnp.bfloat16)
    term_head = np.where(np.arange(LANES) < STAT_LANES, np.arange(LANES) % B_HEADS, -1)
    expand = jnp.asarray(term_head[:, None] == (np.arange(B_WIDTH) // HEAD_DIM)[None, :], jnp.bfloat16)

    for i in range(depth):
        x2d = x.reshape(tokens, d_model)
        tile = lambda g, n, s=1.0: jnp.tile(g * s, n)[None, :]
        qa, ka, va, ga, qb, kb, vb, gb, qc, kc, vc = _proj_call(
            x2d, norm_gain[i][None, :], w_in[i].astype(jnp.bfloat16), seg, perm, cos_t, sin_t,
            tile(q_norm_a[i], A_HEADS, scale), tile(k_norm_a[i], A_KV_HEADS),
            tile(q_norm_b[i], B_HEADS, scale), tile(k_norm_b[i], B_HEADS), b, seq)

        r3 = lambda t: t.reshape(b, seq, t.shape[-1])
        (mixed_a,) = _attn_call(r3(qa), r3(ka), r3(va), A_WINDOW - 1,
                                sinks=sinks_a[i], gate=r3(ga), name="attn_a")

        (w1, d1), (w4, d4), (w16, d16) = B_PATTERNS
        o1, s1 = _attn_call(r3(qb), r3(kb), r3(vb), w1 // d1, want_stats=True, name="attn_b1")
        o4, s4 = _attn_mid_call(qc, kc, vc, w4 // d4, name="attn_b4")
        fold = lambda t: t.reshape(b * CLASSES, seq // CLASSES, t.shape[-1])
        oc, sc = _attn_call(fold(qc), fold(kc), fold(vc), w16 // d16, merge=(fold(o4), fold(s4)),
                            want_stats=True, name="attn_b16")
        unfold = lambda t: t.reshape(b, CLASSES, seq // CLASSES, t.shape[-1])

        out = _out_call(x2d, mixed_a.reshape(tokens, A_WIDTH),
                        o1.reshape(tokens, B_WIDTH), s1.reshape(tokens, LANES),
                        unfold(oc), unfold(sc), gb, permt, expand,
                        w_out[i].astype(jnp.bfloat16), seq)
        x = out.reshape(b, seq, d_model)
    return x
```

```python
import functools

import numpy as np
import jax
import jax.numpy as jnp
from jax import lax
from jax.experimental import pallas as pl
from jax.experimental.pallas import tpu as pltpu

HEAD_DIM = 64
HALF = HEAD_DIM // 2
A_HEADS = 8
A_KV_HEADS = 2
A_WINDOW = 128
B_HEADS = 8
B_PATTERNS = ((128, 1), (512, 4), (2048, 16))
BLOCK = 128
ROPE_THETA = 10000.0
EPS = 1e-6
NEG = -1e30
LOG2E = 1.4426950408889634

A_WIDTH = A_HEADS * HEAD_DIM
A_KV_WIDTH = A_KV_HEADS * HEAD_DIM
B_WIDTH = B_HEADS * HEAD_DIM

LANES = 128
BF16_ROWS = 16
SEG_TILE = 256
PROJ_ROWS = 512
VMEM_LIMIT = 56 * 1024 * 1024

A_KV_PAIR_WIDTH = 2 * A_KV_HEADS * HEAD_DIM
assert A_KV_WIDTH == LANES

CLASSES = max(d for _, d in B_PATTERNS)
LOCAL = CLASSES * BF16_ROWS
MID_DIL = 4
MID_RUN = BLOCK // MID_DIL
STAT_LANES = 3 * B_HEADS


def _dot(a, b):
    return jnp.dot(a, b, preferred_element_type=jnp.float32)


def _split3(v):
    hi = v.astype(jnp.bfloat16).astype(jnp.float32)
    r1 = v - hi
    mid = r1.astype(jnp.bfloat16).astype(jnp.float32)
    lo = (r1 - mid).astype(jnp.bfloat16).astype(jnp.float32)
    return hi, mid, lo


def _head_norm_rope(p, seg, gain, cos, sin, hi_mask):
    width = p.shape[-1]
    sq = (p * p).astype(jnp.bfloat16)
    step = min(width, SEG_TILE)
    sums = []
    for c in range(0, width, step):
        sums.append(_dot(sq[:, c:c + step], seg[:step, :step]))
    ss = sums[0] if len(sums) == 1 else jnp.concatenate(sums, axis=-1)
    y = p * lax.rsqrt(ss * (1.0 / HEAD_DIM) + EPS) * gain
    outs = []
    for c in range(0, width, LANES):
        yc = y[:, c:c + LANES]
        partner = jnp.where(hi_mask, pltpu.roll(yc, HALF, 1), pltpu.roll(yc, LANES - HALF, 1))
        outs.append(yc * cos + partner * sin)
    return outs[0] if len(outs) == 1 else jnp.concatenate(outs, axis=-1)


def _proj_kernel(x_ref, gain_ref, w_ref, seg_ref, perm_ref, cos_ref, sin_ref,
                 gqa_ref, gka_ref, gqb_ref, gkb_ref,
                 qa_ref, ka_ref, va_ref, ga_ref, qb_ref, kb_ref, vb_ref, gb_ref,
                 qc_ref, kc_ref, vc_ref, pbuf):
    xf = x_ref[...]
    ms = jnp.mean(xf * xf, axis=-1, keepdims=True)
    h = (xf * lax.rsqrt(ms + EPS) * gain_ref[...]).astype(jnp.bfloat16)
    seg = seg_ref[...]
    perm = perm_ref[...]
    cos = cos_ref[...]
    sin = sin_ref[...]
    lane = lax.broadcasted_iota(jnp.int32, (1, LANES), 1)
    hi_mask = (lane & HALF) != 0
    low = lane < HEAD_DIM
    bf = jnp.bfloat16

    def per_pair(y):
        swapped = pltpu.roll(y, HEAD_DIM, 1)
        return jnp.concatenate([jnp.where(low, y, swapped), jnp.where(low, swapped, y)], axis=-1)

    def normed(gain_ref_):
        return lambda p: _head_norm_rope(p, seg, gain_ref_[...], cos, sin, hi_mask)

    plain = lambda p: p
    groups = [
        [(A_WIDTH, normed(gqa_ref), qa_ref, None)],
        [(A_KV_WIDTH, lambda p: per_pair(normed(gka_ref)(p)), ka_ref, None),
         (A_KV_WIDTH, per_pair, va_ref, None)],
        [(A_WIDTH, plain, ga_ref, None)],
        [(B_WIDTH, normed(gqb_ref), qb_ref, qc_ref)],
        [(B_WIDTH, normed(gkb_ref), kb_ref, kc_ref)],
        [(B_WIDTH, plain, vb_ref, vc_ref)],
        [(B_WIDTH, plain, gb_ref, None)],
    ]
    widths = [sum(part[0] for part in g) for g in groups]
    starts = np.cumsum([0] + widths)

    def project(i):
        pbuf[i % 2, :, 0:widths[i]] = _dot(h, w_ref[:, starts[i]:starts[i] + widths[i]])

    def finish(i):
        lo = 0
        for width, epilogue, nat_ref, _ in groups[i]:
            nat_ref[...] = epilogue(pbuf[i % 2, :, lo:lo + width]).astype(nat_ref.dtype)
            lo += width

    def to_class_major(i):
        _, _, nat_ref, cm_ref = groups[i][0]
        if cm_ref is None:
            return
        for j in range(nat_ref.shape[0] // LOCAL):
            z = _dot(perm, nat_ref[j * LOCAL:(j + 1) * LOCAL, :]).astype(bf)
            for c in range(CLASSES):
                cm_ref[c, j * BF16_ROWS:(j + 1) * BF16_ROWS, :] = z[c * BF16_ROWS:(c + 1) * BF16_ROWS, :]

    n = len(groups)
    for step in range(n + 2):
        if step < n:
            project(step)
        if 0 <= step - 1 < n:
            finish(step - 1)
        if 0 <= step - 2 < n:
            to_class_major(step - 2)


def _proj_call(x2d, gain, w_bf, seg, perm, cos_t, sin_t, gqa, gka, gqb, gkb, batch, seq):
    tokens, d_model = x2d.shape
    tm = PROJ_ROWS
    steps_per_seq = seq // tm
    row = lambda w: pl.BlockSpec((tm, w), lambda g: (g, 0))
    full = lambda a: pl.BlockSpec(a.shape, lambda g: (0,) * a.ndim)
    table = pl.BlockSpec((tm, LANES), lambda g: (g % steps_per_seq, 0))
    cmaj = pl.BlockSpec((None, CLASSES, tm // CLASSES, B_WIDTH),
                        lambda g: (g // steps_per_seq, 0, g % steps_per_seq, 0))
    bf = jnp.bfloat16
    cm_shape = jax.ShapeDtypeStruct((batch, CLASSES, seq // CLASSES, B_WIDTH), bf)
    out_shape = [
        jax.ShapeDtypeStruct((tokens, A_WIDTH), bf),
        jax.ShapeDtypeStruct((tokens, A_KV_PAIR_WIDTH), bf),
        jax.ShapeDtypeStruct((tokens, A_KV_PAIR_WIDTH), bf),
        jax.ShapeDtypeStruct((tokens, A_WIDTH), jnp.float32),
        jax.ShapeDtypeStruct((tokens, B_WIDTH), bf),
        jax.ShapeDtypeStruct((tokens, B_WIDTH), bf),
        jax.ShapeDtypeStruct((tokens, B_WIDTH), bf),
        jax.ShapeDtypeStruct((tokens, B_WIDTH), jnp.float32),
        cm_shape, cm_shape, cm_shape,
    ]
    return pl.pallas_call(
        _proj_kernel,
        grid=(tokens // tm,),
        in_specs=[row(d_model), full(gain), full(w_bf), full(seg), full(perm), table, table,
                  full(gqa), full(gka), full(gqb), full(gkb)],
        out_specs=[row(s.shape[1]) for s in out_shape[:8]] + [cmaj] * 3,
        out_shape=out_shape,
        scratch_shapes=[pltpu.VMEM((2, tm, max(A_WIDTH, B_WIDTH)), jnp.float32)],
        compiler_params=pltpu.CompilerParams(
            dimension_semantics=("parallel",), vmem_limit_bytes=VMEM_LIMIT),
        name="proj",
    )(x2d, gain, w_bf, seg, perm, cos_t, sin_t, gqa, gka, gqb, gkb)


ONES_ROWS = BF16_ROWS
ATTN_BLOCKS = 16


def _attn_kernel(*refs, pairs, slabs, group, nblk, strided, has_sinks, has_gate, has_merge, want_stats):
    it = iter(refs)
    if strided:
        q_ref, kc_ref, vc_ref, bias_ref = (next(it) for _ in range(4))
        kp_ref = vp_ref = None
    else:
        q_ref, kc_ref, kp_ref, vc_ref, vp_ref, bias_ref = (next(it) for _ in range(6))
    sink_ref = next(it) if has_sinks else None
    gate_ref = next(it) if has_gate else None
    oin_ref, sin_ref = (next(it), next(it)) if has_merge else (None, None)
    o_ref = next(it)
    st_ref = next(it) if want_stats else None
    vtbuf, sbuf, pbuf, mbuf = (next(it) for _ in range(4))
    lsebuf = next(it) if want_stats else None

    def block_of(ref, g, m, cols):
        if strided:
            return ref[:, g, m, :, cols].reshape(BLOCK, LANES)
        return ref[g, m * BLOCK:(m + 1) * BLOCK, cols]

    def key_block(cur_ref, prev_ref, g, kb, cols):
        if kb > 0:
            return block_of(cur_ref, g, kb - 1, cols)
        return block_of(cur_ref, g, 0, cols) if strided else prev_ref[g, :, cols]

    def store_block(ref, g, m, cols, val):
        if strided:
            ref[:, g, m, :, cols] = val.reshape(BLOCK // MID_RUN, MID_RUN, LANES)
        else:
            ref[g, m * BLOCK:(m + 1) * BLOCK, cols] = val

    if strided:
        first_bias = bias_ref[1]
    else:
        first_bias = bias_ref[jnp.where(pl.program_id(1) == 0, 1, 0)]

    ones = jnp.ones((ONES_ROWS, BLOCK), vtbuf.dtype)

    def transpose_values(g, kb):
        for sl in range(slabs):
            src = slice(sl * LANES, (sl + 1) * LANES)
            vtbuf[sl, g * (nblk + 1) + kb, 0:LANES, :] = key_block(vc_ref, vp_ref, g, kb, src).T
            vtbuf[sl, g * (nblk + 1) + kb, LANES:, :] = ones

    zeros = jnp.zeros((HEAD_DIM, BLOCK), q_ref.dtype)
    onehot = jnp.concatenate([bias_ref[2, 0:BLOCK], bias_ref[2, 0:BLOCK]], axis=1)

    def sink_row(hp):
        return jnp.concatenate([jnp.full((1, LANES), sink_ref[2 * hp] * LOG2E, jnp.float32),
                                jnp.full((1, LANES), sink_ref[2 * hp + 1] * LOG2E, jnp.float32)], axis=1)

    def score_stage(t):
        slot, (g, j) = t % 2, divmod(t, nblk)
        bias_t = first_bias if j == 0 else bias_ref[0]
        for hp in range(pairs):
            sl = hp * slabs // pairs
            kcols = slice(sl * LANES, (sl + 1) * LANES)
            qt = block_of(q_ref, g, j, slice(hp * LANES, (hp + 1) * LANES)).T
            qm = jnp.concatenate([jnp.concatenate([qt[0:HEAD_DIM], zeros], axis=0),
                                  jnp.concatenate([zeros, qt[HEAD_DIM:]], axis=0)], axis=1)
            kw = jnp.concatenate([key_block(kc_ref, kp_ref, g, j, kcols),
                                  key_block(kc_ref, kp_ref, g, j + 1, kcols)], axis=0)
            sbuf[slot, hp] = _dot(jnp.concatenate([kw, bias_t], axis=1),
                                  jnp.concatenate([qm, onehot], axis=0))

    def softmax_stage(t):
        slot, (g, j) = t % 2, divmod(t, nblk)
        if j == 0:
            transpose_values(g, 0)
        transpose_values(g, j + 1)
        for hp in range(pairs):
            m = jnp.max(sbuf[slot, hp], axis=0, keepdims=True)
            if has_sinks:
                m = jnp.maximum(m, sink_row(hp))
            pbuf[slot, hp] = jnp.exp2(sbuf[slot, hp] - m).astype(pbuf.dtype)
            mbuf[slot, hp] = m

    def value_stage(t):
        slot, (g, j) = t % 2, divmod(t, nblk)
        kb0 = g * (nblk + 1) + j
        if has_merge:
            terms = block_of(sin_ref, g, j, slice(0, LANES)).T.astype(jnp.float32)
            lse_in = (terms[0:B_HEADS] + terms[B_HEADS:2 * B_HEADS] + terms[2 * B_HEADS:3 * B_HEADS])
        for hp in range(pairs):
            sl = hp * slabs // pairs
            cols = slice(hp * LANES, (hp + 1) * LANES)
            m = mbuf[slot, hp]
            vt = jnp.concatenate([vtbuf[sl, kb0], vtbuf[sl, kb0 + 1]], axis=1)
            r = _dot(vt, pbuf[slot, hp])
            l = r[LANES:LANES + 1, :]
            if has_sinks:
                l = l + jnp.exp2(sink_row(hp) - m)
            inv = 1.0 / l
            lse = m + jnp.log2(l)
            if has_merge:
                other = jnp.concatenate([lse_in[2 * hp:2 * hp + 1], lse_in[2 * hp + 1:2 * hp + 2]], axis=1)
                top = jnp.maximum(lse, other)
                mine, theirs = jnp.exp2(lse - top), jnp.exp2(other - top)
                total = mine + theirs
                inv, frac = inv * (mine / total), theirs / total
                lse = top + jnp.log2(total)
            ot = jnp.concatenate([r[0:HEAD_DIM, 0:LANES] * inv[:, 0:LANES],
                                  r[HEAD_DIM:LANES, LANES:] * inv[:, LANES:]], axis=0)
            if has_merge:
                oin_t = block_of(oin_ref, g, j, cols).T.astype(jnp.float32)
                ot = ot + oin_t * jnp.concatenate(
                    [jnp.broadcast_to(frac[:, 0:LANES], (HEAD_DIM, LANES)),
                     jnp.broadcast_to(frac[:, LANES:], (HEAD_DIM, LANES))], axis=0)
            if has_gate:
                gt = block_of(gate_ref, g, j, cols)
                o = (ot.T * (gt * jax.nn.sigmoid(gt))).astype(o_ref.dtype)
            else:
                o = ot.astype(o_ref.dtype).T
            store_block(o_ref, g, j, cols, o)
            if want_stats:
                lsebuf[slot, 2 * hp:2 * hp + 1, :] = lse[:, 0:LANES]
                lsebuf[slot, 2 * hp + 1:2 * hp + 2, :] = lse[:, LANES:]
        if want_stats:
            terms = _split3(lsebuf[slot])
            pad = jnp.zeros((LANES - STAT_LANES, BLOCK), jnp.float32)
            tile = jnp.concatenate(list(terms) + [pad], axis=0)
            store_block(st_ref, g, j, slice(0, LANES), tile.astype(st_ref.dtype).T)

    jobs = group * nblk
    for step in range(jobs + 2):
        if step - 2 >= 0:
            value_stage(step - 2)
        if 0 <= step - 1 < jobs:
            softmax_stage(step - 1)
        if step < jobs:
            score_stage(step)


def _band_bias(max_dist, mid):
    r = np.arange(BLOCK)
    pos = MID_DIL * (r % MID_RUN) + r // MID_RUN if mid else r
    qpos = pos[None, :] + BLOCK
    kpos = np.concatenate([pos, pos + BLOCK])[:, None]
    dist = qpos - kpos
    valid = (dist >= 0) & (dist <= max_dist)
    first = valid & (np.arange(2 * BLOCK)[:, None] >= BLOCK)
    masks = np.where(np.stack([valid, first]), 0.0, NEG)
    onehot = np.tile(np.eye(BLOCK), (2, 1))
    return jnp.asarray(np.concatenate([masks, onehot[None]]), jnp.bfloat16)


def _attn_scratch(pairs, slabs, key_blocks, want_stats):
    scratch = [pltpu.VMEM((slabs, key_blocks, LANES + ONES_ROWS, BLOCK), jnp.bfloat16),
               pltpu.VMEM((2, pairs, 2 * BLOCK, 2 * LANES), jnp.float32),
               pltpu.VMEM((2, pairs, 2 * BLOCK, 2 * LANES), jnp.bfloat16),
               pltpu.VMEM((2, pairs, 1, 2 * LANES), jnp.float32)]
    if want_stats:
        scratch.append(pltpu.VMEM((2, 2 * pairs, BLOCK), jnp.float32))
    return scratch


def _attn_call(q, k, v, max_dist, *, sinks=None, gate=None, merge=None, want_stats=False, name):
    n, seq, qw = q.shape
    kvw = k.shape[-1]
    pairs, slabs = qw // LANES, kvw // LANES
    tq = min(seq, ATTN_BLOCKS * BLOCK)
    sub = tq // BLOCK
    group = ATTN_BLOCKS // sub
    cur = lambda w: pl.BlockSpec((group, tq, w), lambda b, i: (b, i, 0))
    prev = lambda w: pl.BlockSpec((group, BLOCK, w), lambda b, i: (b, jnp.maximum(i * sub - 1, 0), 0))
    bias = _band_bias(max_dist, mid=False)
    args = [q, k, k, v, v, bias]
    in_specs = [cur(qw), cur(kvw), prev(kvw), cur(kvw), prev(kvw),
                pl.BlockSpec(bias.shape, lambda b, i: (0, 0, 0))]
    if sinks is not None:
        args.append(sinks)
        in_specs.append(pl.BlockSpec(memory_space=pltpu.SMEM))
    if gate is not None:
        args.append(gate)
        in_specs.append(cur(qw))
    if merge is not None:
        args.extend(merge)
        in_specs.extend([cur(qw), cur(LANES)])
    out_shape = [jax.ShapeDtypeStruct((n, seq, qw), jnp.bfloat16)]
    out_specs = [cur(qw)]
    if want_stats:
        out_shape.append(jax.ShapeDtypeStruct((n, seq, LANES), jnp.bfloat16))
        out_specs.append(cur(LANES))
    kernel = functools.partial(_attn_kernel, pairs=pairs, slabs=slabs, group=group, nblk=sub,
                               strided=False, has_sinks=sinks is not None, has_gate=gate is not None,
                               has_merge=merge is not None, want_stats=want_stats)
    return pl.pallas_call(
        kernel,
        grid=(n // group, seq // tq),
        in_specs=in_specs,
        out_specs=out_specs,
        out_shape=out_shape,
        scratch_shapes=_attn_scratch(pairs, slabs, group * (sub + 1), want_stats),
        compiler_params=pltpu.CompilerParams(
            dimension_semantics=("parallel", "arbitrary"), vmem_limit_bytes=VMEM_LIMIT),
        name=name,
    )(*args)


def _attn_mid_call(q, k, v, max_dist, *, name):
    b, classes, per_class, w = q.shape
    outer = classes // MID_DIL
    nblk = per_class // MID_RUN
    group = max(1, min(MID_DIL, ATTN_BLOCKS // nblk))
    view = lambda t: t.reshape(b, outer, MID_DIL, nblk, MID_RUN, t.shape[-1])
    spec = lambda width: pl.BlockSpec((None, outer, group, nblk, MID_RUN, width),
                                      lambda i, e: (i, 0, e, 0, 0, 0))
    pairs = w // LANES
    bias = _band_bias(max_dist, mid=True)
    shape6 = (b, outer, MID_DIL, nblk, MID_RUN)
    kernel = functools.partial(_attn_kernel, pairs=pairs, slabs=pairs, group=group, nblk=nblk,
                               strided=True, has_sinks=False, has_gate=False, has_merge=False,
                               want_stats=True)
    o, st = pl.pallas_call(
        kernel,
        grid=(b, MID_DIL // group),
        in_specs=[spec(w), spec(w), spec(w), pl.BlockSpec(bias.shape, lambda i, e: (0, 0, 0))],
        out_specs=[spec(w), spec(LANES)],
        out_shape=[jax.ShapeDtypeStruct(shape6 + (w,), jnp.bfloat16),
                   jax.ShapeDtypeStruct(shape6 + (LANES,), jnp.bfloat16)],
        scratch_shapes=_attn_scratch(pairs, pairs, group * (nblk + 1), True),
        compiler_params=pltpu.CompilerParams(
            dimension_semantics=("parallel", "arbitrary"), vmem_limit_bytes=VMEM_LIMIT),
        name=name,
    )(view(q), view(k), view(v), bias)
    return o.reshape(b, classes, per_class, w), st.reshape(b, classes, per_class, LANES)


def _expand_heads(w, expand):
    hi, mid, lo = _split3(w)
    packed = hi + pltpu.roll(mid, B_HEADS, 1) + pltpu.roll(lo, 2 * B_HEADS, 1)
    return _dot(packed.astype(jnp.bfloat16), expand)


def _stat_sum(st):
    return (st + pltpu.roll(st, LANES - B_HEADS, 1) + pltpu.roll(st, LANES - 2 * B_HEADS, 1))


def _out_kernel(x_ref, ma_ref, o1_ref, s1_ref, oc_ref, sc_ref, gb_ref,
                permt_ref, expand_ref, w_ref, out_ref):
    rows = x_ref.shape[0]
    permt = permt_ref[...]

    def to_token_order(o_ref, s_ref):
        outs, stats = [], []
        for j in range(rows // LOCAL):
            sel = slice(j * BF16_ROWS, (j + 1) * BF16_ROWS)
            y = jnp.concatenate([jnp.concatenate([o_ref[c, sel, :] for c in range(CLASSES)], axis=0),
                                 jnp.concatenate([s_ref[c, sel, :] for c in range(CLASSES)], axis=0)],
                                axis=1)
            u = _dot(permt, y)
            outs.append(u[:, :B_WIDTH])
            stats.append(u[:, B_WIDTH:])
        return jnp.concatenate(outs, axis=0), jnp.concatenate(stats, axis=0)

    o1 = o1_ref[...].astype(jnp.float32)
    l1 = _stat_sum(s1_ref[...].astype(jnp.float32))
    oc, sc = to_token_order(oc_ref, sc_ref)
    lc = _stat_sum(sc)

    top = jnp.maximum(l1, lc)
    e1, ec = jnp.exp2(l1 - top), jnp.exp2(lc - top)
    head_lane = lax.broadcasted_iota(jnp.int32, (1, LANES), 1) < B_HEADS
    w1 = jnp.where(head_lane, e1 / (e1 + ec), 0.0)
    ob = oc + _expand_heads(w1, expand_ref[...]) * (o1 - oc)
    gb = gb_ref[...]
    mb = (ob * (gb * jax.nn.sigmoid(gb))).astype(jnp.bfloat16)
    out_ref[...] = (x_ref[...] + _dot(ma_ref[...], w_ref[0:A_WIDTH, :])
                    + _dot(mb, w_ref[A_WIDTH:, :]))


def _out_call(x2d, mixed_a, o1, s1, oc, sc, gate_b, permt, expand, w_bf, seq):
    tokens, d_model = x2d.shape
    tm = PROJ_ROWS
    steps_per_seq = seq // tm
    row = lambda w: pl.BlockSpec((tm, w), lambda g: (g, 0))
    full = lambda a: pl.BlockSpec(a.shape, lambda g: (0,) * a.ndim)
    cmaj = lambda w: pl.BlockSpec((None, CLASSES, tm // CLASSES, w),
                                  lambda g: (g // steps_per_seq, 0, g % steps_per_seq, 0))
    return pl.pallas_call(
        _out_kernel,
        grid=(tokens // tm,),
        in_specs=[row(d_model), row(A_WIDTH), row(B_WIDTH), row(LANES),
                  cmaj(B_WIDTH), cmaj(LANES),
                  row(B_WIDTH), full(permt), full(expand), full(w_bf)],
        out_specs=row(d_model),
        out_shape=jax.ShapeDtypeStruct((tokens, d_model), jnp.float32),
        compiler_params=pltpu.CompilerParams(
            dimension_semantics=("parallel",), vmem_limit_bytes=VMEM_LIMIT),
        name="merge_out",
    )(x2d, mixed_a, o1, s1, oc, sc, gate_b, permt, expand, w_bf)


def _rope_tables(seq):
    inv = ROPE_THETA ** (-jnp.arange(HALF, dtype=jnp.float32) / HALF)
    ang = jnp.arange(seq).astype(jnp.float32)[:, None] * inv[None, :]
    reps = LANES // HALF
    cos_t = jnp.tile(jnp.cos(ang), (1, reps))
    sign = jnp.tile(jnp.concatenate([-jnp.ones((HALF,), jnp.float32), jnp.ones((HALF,), jnp.float32)]),
                    LANES // HEAD_DIM)
    sin_t = jnp.tile(jnp.sin(ang), (1, reps)) * sign[None, :]
    return cos_t, sin_t


def kernel(x, norm_gain, w_in, q_norm_a, k_norm_a, sinks_a, q_norm_b, k_norm_b, w_out):
    b, seq, d_model = x.shape
    tokens = b * seq
    depth = norm_gain.shape[0]
    assert [d for _, d in B_PATTERNS] == [1, MID_DIL, CLASSES]
    assert seq % PROJ_ROWS == 0 and PROJ_ROWS % LOCAL == 0 and (seq // CLASSES) % BLOCK == 0
    scale = HEAD_DIM ** -0.5 * LOG2E
    cos_t, sin_t = _rope_tables(seq)
    seg_i = np.arange(SEG_TILE) // HEAD_DIM
    seg = jnp.asarray(seg_i[:, None] == seg_i[None, :], jnp.bfloat16)
    r = np.arange(LOCAL)
    perm_np = (CLASSES * (r % BF16_ROWS) + r // BF16_ROWS)[:, None] == r[None, :]
    perm = jnp.asarray(perm_np, jnp.bfloat16)
    permt = jnp.asarray(perm_np.T, jnp.bfloat16)
    term_head = np.where(np.arange(LANES) < STAT_LANES, np.arange(LANES) % B_HEADS, -1)
    expand = jnp.asarray(term_head[:, None] == (np.arange(B_WIDTH) // HEAD_DIM)[None, :], jnp.bfloat16)

    for i in range(depth):
        x2d = x.reshape(tokens, d_model)
        tile = lambda g, n, s=1.0: jnp.tile(g * s, n)[None, :]
        qa, ka, va, ga, qb, kb, vb, gb, qc, kc, vc = _proj_call(
            x2d, norm_gain[i][None, :], w_in[i].astype(jnp.bfloat16), seg, perm, cos_t, sin_t,
            tile(q_norm_a[i], A_HEADS, scale), tile(k_norm_a[i], A_KV_HEADS),
            tile(q_norm_b[i], B_HEADS, scale), tile(k_norm_b[i], B_HEADS), b, seq)

        r3 = lambda t: t.reshape(b, seq, t.shape[-1])
        (mixed_a,) = _attn_call(r3(qa), r3(ka), r3(va), A_WINDOW - 1,
                                sinks=sinks_a[i], gate=r3(ga), name="attn_a")

        (w1, d1), (w4, d4), (w16, d16) = B_PATTERNS
        o1, s1 = _attn_call(r3(qb), r3(kb), r3(vb), w1 // d1, want_stats=True, name="attn_b1")
        o4, s4 = _attn_mid_call(qc, kc, vc, w4 // d4, name="attn_b4")
        fold = lambda t: t.reshape(b * CLASSES, seq // CLASSES, t.shape[-1])
        oc, sc = _attn_call(fold(qc), fold(kc), fold(vc), w16 // d16, merge=(fold(o4), fold(s4)),
                            want_stats=True, name="attn_b16")
        unfold = lambda t: t.reshape(b, CLASSES, seq // CLASSES, t.shape[-1])

        out = _out_call(x2d, mixed_a.reshape(tokens, A_WIDTH),
                        o1.reshape(tokens, B_WIDTH), s1.reshape(tokens, LANES),
                        unfold(oc), unfold(sc), gb, permt, expand,
                        w_out[i].astype(jnp.bfloat16), seq)
        x = out.reshape(b, seq, d_model)
    return x
```

```python
import functools

import numpy as np
import jax
import jax.numpy as jnp
from jax import lax
from jax.experimental import pallas as pl
from jax.experimental.pallas import tpu as pltpu

HEAD_DIM = 64
HALF = HEAD_DIM // 2
A_HEADS = 8
A_KV_HEADS = 2
A_WINDOW = 128
B_HEADS = 8
B_PATTERNS = ((128, 1), (512, 4), (2048, 16))
BLOCK = 128
ROPE_THETA = 10000.0
EPS = 1e-6
NEG = -1e30
LOG2E = 1.4426950408889634

A_WIDTH = A_HEADS * HEAD_DIM
A_KV_WIDTH = A_KV_HEADS * HEAD_DIM
B_WIDTH = B_HEADS * HEAD_DIM

LANES = 128
BF16_ROWS = 16
SEG_TILE = 256
PROJ_ROWS = 512
OUT_ROWS = 1024
VMEM_LIMIT = 56 * 1024 * 1024

A_KV_PAIR_WIDTH = 2 * A_KV_HEADS * HEAD_DIM
assert A_KV_WIDTH == LANES

CLASSES = max(d for _, d in B_PATTERNS)
LOCAL = CLASSES * BF16_ROWS
MID_DIL = 4
MID_RUN = BLOCK // MID_DIL
STAT_LANES = 3 * B_HEADS


def _dot(a, b):
    return jnp.dot(a, b, preferred_element_type=jnp.float32)


def _split3(v):
    hi = v.astype(jnp.bfloat16).astype(jnp.float32)
    r1 = v - hi
    mid = r1.astype(jnp.bfloat16).astype(jnp.float32)
    lo = (r1 - mid).astype(jnp.bfloat16).astype(jnp.float32)
    return hi, mid, lo


def _head_norm_rope(p, seg, gain, cos, sin, hi_mask):
    width = p.shape[-1]
    sq = (p * p).astype(jnp.bfloat16)
    step = min(width, SEG_TILE)
    sums = []
    for c in range(0, width, step):
        sums.append(_dot(sq[:, c:c + step], seg[:step, :step]))
    ss = sums[0] if len(sums) == 1 else jnp.concatenate(sums, axis=-1)
    y = p * lax.rsqrt(ss * (1.0 / HEAD_DIM) + EPS) * gain
    outs = []
    for c in range(0, width, LANES):
        yc = y[:, c:c + LANES]
        partner = jnp.where(hi_mask, pltpu.roll(yc, HALF, 1), pltpu.roll(yc, LANES - HALF, 1))
        outs.append(yc * cos + partner * sin)
    return outs[0] if len(outs) == 1 else jnp.concatenate(outs, axis=-1)


def _proj_kernel(x_ref, gain_ref, w_ref, seg_ref, perm_ref, cos_ref, sin_ref,
                 gqa_ref, gka_ref, gqb_ref, gkb_ref,
                 qa_ref, ka_ref, va_ref, ga_ref, qb_ref, kb_ref, vb_ref, gb_ref,
                 qc_ref, kc_ref, vc_ref, pbuf):
    xf = x_ref[...]
    ms = jnp.mean(xf * xf, axis=-1, keepdims=True)
    h = (xf * lax.rsqrt(ms + EPS) * gain_ref[...]).astype(jnp.bfloat16)
    seg = seg_ref[...]
    perm = perm_ref[...]
    cos = cos_ref[...]
    sin = sin_ref[...]
    lane = lax.broadcasted_iota(jnp.int32, (1, LANES), 1)
    hi_mask = (lane & HALF) != 0
    low = lane < HEAD_DIM
    bf = jnp.bfloat16

    def per_pair(y):
        swapped = pltpu.roll(y, HEAD_DIM, 1)
        return jnp.concatenate([jnp.where(low, y, swapped), jnp.where(low, swapped, y)], axis=-1)

    def normed(gain_ref_):
        return lambda p: _head_norm_rope(p, seg, gain_ref_[...], cos, sin, hi_mask)

    plain = lambda p: p
    groups = [
        [(A_WIDTH, normed(gqa_ref), qa_ref, None)],
        [(A_KV_WIDTH, lambda p: per_pair(normed(gka_ref)(p)), ka_ref, None),
         (A_KV_WIDTH, per_pair, va_ref, None)],
        [(A_WIDTH, plain, ga_ref, None)],
        [(B_WIDTH, normed(gqb_ref), qb_ref, qc_ref)],
        [(B_WIDTH, normed(gkb_ref), kb_ref, kc_ref)],
        [(B_WIDTH, plain, vb_ref, vc_ref)],
        [(B_WIDTH, plain, gb_ref, None)],
    ]
    widths = [sum(part[0] for part in g) for g in groups]
    starts = np.cumsum([0] + widths)

    def project(i):
        pbuf[i % 2, :, 0:widths[i]] = _dot(h, w_ref[:, starts[i]:starts[i] + widths[i]])

    def finish(i):
        lo = 0
        for width, epilogue, nat_ref, _ in groups[i]:
            nat_ref[...] = epilogue(pbuf[i % 2, :, lo:lo + width]).astype(nat_ref.dtype)
            lo += width

    def to_class_major(i):
        _, _, nat_ref, cm_ref = groups[i][0]
        if cm_ref is None:
            return
        for j in range(nat_ref.shape[0] // LOCAL):
            z = _dot(perm, nat_ref[j * LOCAL:(j + 1) * LOCAL, :]).astype(bf)
            for c in range(CLASSES):
                cm_ref[c, j * BF16_ROWS:(j + 1) * BF16_ROWS, :] = z[c * BF16_ROWS:(c + 1) * BF16_ROWS, :]

    n = len(groups)
    for step in range(n + 2):
        if step < n:
            project(step)
        if 0 <= step - 1 < n:
            finish(step - 1)
        if 0 <= step - 2 < n:
            to_class_major(step - 2)


def _proj_call(x2d, gain, w_bf, seg, perm, cos_t, sin_t, gqa, gka, gqb, gkb, batch, seq):
    tokens, d_model = x2d.shape
    tm = PROJ_ROWS
    steps_per_seq = seq // tm
    row = lambda w: pl.BlockSpec((tm, w), lambda g: (g, 0))
    full = lambda a: pl.BlockSpec(a.shape, lambda g: (0,) * a.ndim)
    table = pl.BlockSpec((tm, LANES), lambda g: (g % steps_per_seq, 0))
    cmaj = pl.BlockSpec((None, CLASSES, tm // CLASSES, B_WIDTH),
                        lambda g: (g // steps_per_seq, 0, g % steps_per_seq, 0))
    bf = jnp.bfloat16
    cm_shape = jax.ShapeDtypeStruct((batch, CLASSES, seq // CLASSES, B_WIDTH), bf)
    out_shape = [
        jax.ShapeDtypeStruct((tokens, A_WIDTH), bf),
        jax.ShapeDtypeStruct((tokens, A_KV_PAIR_WIDTH), bf),
        jax.ShapeDtypeStruct((tokens, A_KV_PAIR_WIDTH), bf),
        jax.ShapeDtypeStruct((tokens, A_WIDTH), jnp.float32),
        jax.ShapeDtypeStruct((tokens, B_WIDTH), bf),
        jax.ShapeDtypeStruct((tokens, B_WIDTH), bf),
        jax.ShapeDtypeStruct((tokens, B_WIDTH), bf),
        jax.ShapeDtypeStruct((tokens, B_WIDTH), jnp.float32),
        cm_shape, cm_shape, cm_shape,
    ]
    return pl.pallas_call(
        _proj_kernel,
        grid=(tokens // tm,),
        in_specs=[row(d_model), full(gain), full(w_bf), full(seg), full(perm), table, table,
                  full(gqa), full(gka), full(gqb), full(gkb)],
        out_specs=[row(s.shape[1]) for s in out_shape[:8]] + [cmaj] * 3,
        out_shape=out_shape,
        scratch_shapes=[pltpu.VMEM((2, tm, max(A_WIDTH, B_WIDTH)), jnp.float32)],
        compiler_params=pltpu.CompilerParams(
            dimension_semantics=("parallel",), vmem_limit_bytes=VMEM_LIMIT),
        name="proj",
    )(x2d, gain, w_bf, seg, perm, cos_t, sin_t, gqa, gka, gqb, gkb)


ONES_ROWS = BF16_ROWS
ATTN_BLOCKS = 16


def _attn_kernel(*refs, pairs, slabs, group, nblk, strided, has_sinks, has_gate, has_merge, want_stats):
    it = iter(refs)
    if strided:
        q_ref, kc_ref, vc_ref, bias_ref = (next(it) for _ in range(4))
        kp_ref = vp_ref = None
    else:
        q_ref, kc_ref, kp_ref, vc_ref, vp_ref, bias_ref = (next(it) for _ in range(6))
    sink_ref = next(it) if has_sinks else None
    gate_ref = next(it) if has_gate else None
    oin_ref, sin_ref = (next(it), next(it)) if has_merge else (None, None)
    o_ref = next(it)
    st_ref = next(it) if want_stats else None
    vtbuf, sbuf, pbuf, mbuf = (next(it) for _ in range(4))
    lsebuf = next(it) if want_stats else None

    def block_of(ref, g, m, cols):
        if strided:
            return ref[:, g, m, :, cols].reshape(BLOCK, LANES)
        return ref[g, m * BLOCK:(m + 1) * BLOCK, cols]

    def key_block(cur_ref, prev_ref, g, kb, cols):
        if kb > 0:
            return block_of(cur_ref, g, kb - 1, cols)
        return block_of(cur_ref, g, 0, cols) if strided else prev_ref[g, :, cols]

    def store_block(ref, g, m, cols, val):
        if strided:
            ref[:, g, m, :, cols] = val.reshape(BLOCK // MID_RUN, MID_RUN, LANES)
        else:
            ref[g, m * BLOCK:(m + 1) * BLOCK, cols] = val

    if strided:
        first_bias = bias_ref[1]
    else:
        first_bias = bias_ref[jnp.where(pl.program_id(1) == 0, 1, 0)]

    ones = jnp.ones((ONES_ROWS, BLOCK), vtbuf.dtype)

    def transpose_values(g, kb):
        for sl in range(slabs):
            src = slice(sl * LANES, (sl + 1) * LANES)
            vtbuf[sl, g * (nblk + 1) + kb, 0:LANES, :] = key_block(vc_ref, vp_ref, g, kb, src).T
            vtbuf[sl, g * (nblk + 1) + kb, LANES:, :] = ones

    zeros = jnp.zeros((HEAD_DIM, BLOCK), q_ref.dtype)
    onehot = jnp.concatenate([bias_ref[2, 0:BLOCK], bias_ref[2, 0:BLOCK]], axis=1)

    def sink_row(hp):
        return jnp.concatenate([jnp.full((1, LANES), sink_ref[2 * hp] * LOG2E, jnp.float32),
                                jnp.full((1, LANES), sink_ref[2 * hp + 1] * LOG2E, jnp.float32)], axis=1)

    def score_stage(t):
        slot, (g, j) = t % 2, divmod(t, nblk)
        bias_t = first_bias if j == 0 else bias_ref[0]
        for hp in range(pairs):
            sl = hp * slabs // pairs
            kcols = slice(sl * LANES, (sl + 1) * LANES)
            qt = block_of(q_ref, g, j, slice(hp * LANES, (hp + 1) * LANES)).T
            qm = jnp.concatenate([jnp.concatenate([qt[0:HEAD_DIM], zeros], axis=0),
                                  jnp.concatenate([zeros, qt[HEAD_DIM:]], axis=0)], axis=1)
            kw = jnp.concatenate([key_block(kc_ref, kp_ref, g, j, kcols),
                                  key_block(kc_ref, kp_ref, g, j + 1, kcols)], axis=0)
            sbuf[slot, hp] = _dot(jnp.concatenate([kw, bias_t], axis=1),
                                  jnp.concatenate([qm, onehot], axis=0))

    def softmax_stage(t):
        slot, (g, j) = t % 2, divmod(t, nblk)
        if j == 0:
            transpose_values(g, 0)
        transpose_values(g, j + 1)
        for hp in range(pairs):
            m = jnp.max(sbuf[slot, hp], axis=0, keepdims=True)
            if has_sinks:
                m = jnp.maximum(m, sink_row(hp))
            pbuf[slot, hp] = jnp.exp2(sbuf[slot, hp] - m).astype(pbuf.dtype)
            mbuf[slot, hp] = m

    def value_stage(t):
        slot, (g, j) = t % 2, divmod(t, nblk)
        kb0 = g * (nblk + 1) + j
        if has_merge:
            terms = block_of(sin_ref, g, j, slice(0, LANES)).T.astype(jnp.float32)
            lse_in = (terms[0:B_HEADS] + terms[B_HEADS:2 * B_HEADS] + terms[2 * B_HEADS:3 * B_HEADS])
        for hp in range(pairs):
            sl = hp * slabs // pairs
            cols = slice(hp * LANES, (hp + 1) * LANES)
            m = mbuf[slot, hp]
            vt = jnp.concatenate([vtbuf[sl, kb0], vtbuf[sl, kb0 + 1]], axis=1)
            r = _dot(vt, pbuf[slot, hp])
            l = r[LANES:LANES + 1, :]
            if has_sinks:
                l = l + jnp.exp2(sink_row(hp) - m)
            inv = 1.0 / l
            lse = m + jnp.log2(l)
            if has_merge:
                other = jnp.concatenate([lse_in[2 * hp:2 * hp + 1], lse_in[2 * hp + 1:2 * hp + 2]], axis=1)
                top = jnp.maximum(lse, other)
                mine, theirs = jnp.exp2(lse - top), jnp.exp2(other - top)
                total = mine + theirs
                inv, frac = inv * (mine / total), theirs / total
                lse = top + jnp.log2(total)
            ot = jnp.concatenate([r[0:HEAD_DIM, 0:LANES] * inv[:, 0:LANES],
                                  r[HEAD_DIM:LANES, LANES:] * inv[:, LANES:]], axis=0)
            if has_merge:
                oin_t = block_of(oin_ref, g, j, cols).T.astype(jnp.float32)
                ot = ot + oin_t * jnp.concatenate(
                    [jnp.broadcast_to(frac[:, 0:LANES], (HEAD_DIM, LANES)),
                     jnp.broadcast_to(frac[:, LANES:], (HEAD_DIM, LANES))], axis=0)
            if has_gate:
                gt = block_of(gate_ref, g, j, cols)
                o = (ot.T * (gt * jax.nn.sigmoid(gt))).astype(o_ref.dtype)
            else:
                o = ot.astype(o_ref.dtype).T
            store_block(o_ref, g, j, cols, o)
            if want_stats:
                lsebuf[slot, 2 * hp:2 * hp + 1, :] = lse[:, 0:LANES]
                lsebuf[slot, 2 * hp + 1:2 * hp + 2, :] = lse[:, LANES:]
        if want_stats:
            terms = _split3(lsebuf[slot])
            pad = jnp.zeros((LANES - STAT_LANES, BLOCK), jnp.float32)
            tile = jnp.concatenate(list(terms) + [pad], axis=0)
            store_block(st_ref, g, j, slice(0, LANES), tile.astype(st_ref.dtype).T)

    jobs = group * nblk
    for step in range(jobs + 2):
        if step - 2 >= 0:
            value_stage(step - 2)
        if 0 <= step - 1 < jobs:
            softmax_stage(step - 1)
        if step < jobs:
            score_stage(step)


def _band_bias(max_dist, mid):
    r = np.arange(BLOCK)
    pos = MID_DIL * (r % MID_RUN) + r // MID_RUN if mid else r
    qpos = pos[None, :] + BLOCK
    kpos = np.concatenate([pos, pos + BLOCK])[:, None]
    dist = qpos - kpos
    valid = (dist >= 0) & (dist <= max_dist)
    first = valid & (np.arange(2 * BLOCK)[:, None] >= BLOCK)
    masks = np.where(np.stack([valid, first]), 0.0, NEG)
    onehot = np.tile(np.eye(BLOCK), (2, 1))
    return jnp.asarray(np.concatenate([masks, onehot[None]]), jnp.bfloat16)


def _attn_scratch(pairs, slabs, key_blocks, want_stats):
    scratch = [pltpu.VMEM((slabs, key_blocks, LANES + ONES_ROWS, BLOCK), jnp.bfloat16),
               pltpu.VMEM((2, pairs, 2 * BLOCK, 2 * LANES), jnp.float32),
               pltpu.VMEM((2, pairs, 2 * BLOCK, 2 * LANES), jnp.bfloat16),
               pltpu.VMEM((2, pairs, 1, 2 * LANES), jnp.float32)]
    if want_stats:
        scratch.append(pltpu.VMEM((2, 2 * pairs, BLOCK), jnp.float32))
    return scratch


def _attn_call(q, k, v, max_dist, *, sinks=None, gate=None, merge=None, want_stats=False, name):
    n, seq, qw = q.shape
    kvw = k.shape[-1]
    pairs, slabs = qw // LANES, kvw // LANES
    tq = min(seq, ATTN_BLOCKS * BLOCK)
    sub = tq // BLOCK
    group = ATTN_BLOCKS // sub
    cur = lambda w: pl.BlockSpec((group, tq, w), lambda b, i: (b, i, 0))
    prev = lambda w: pl.BlockSpec((group, BLOCK, w), lambda b, i: (b, jnp.maximum(i * sub - 1, 0), 0))
    bias = _band_bias(max_dist, mid=False)
    args = [q, k, k, v, v, bias]
    in_specs = [cur(qw), cur(kvw), prev(kvw), cur(kvw), prev(kvw),
                pl.BlockSpec(bias.shape, lambda b, i: (0, 0, 0))]
    if sinks is not None:
        args.append(sinks)
        in_specs.append(pl.BlockSpec(memory_space=pltpu.SMEM))
    if gate is not None:
        args.append(gate)
        in_specs.append(cur(qw))
    if merge is not None:
        args.extend(merge)
        in_specs.extend([cur(qw), cur(LANES)])
    out_shape = [jax.ShapeDtypeStruct((n, seq, qw), jnp.bfloat16)]
    out_specs = [cur(qw)]
    if want_stats:
        out_shape.append(jax.ShapeDtypeStruct((n, seq, LANES), jnp.bfloat16))
        out_specs.append(cur(LANES))
    kernel = functools.partial(_attn_kernel, pairs=pairs, slabs=slabs, group=group, nblk=sub,
                               strided=False, has_sinks=sinks is not None, has_gate=gate is not None,
                               has_merge=merge is not None, want_stats=want_stats)
    return pl.pallas_call(
        kernel,
        grid=(n // group, seq // tq),
        in_specs=in_specs,
        out_specs=out_specs,
        out_shape=out_shape,
        scratch_shapes=_attn_scratch(pairs, slabs, group * (sub + 1), want_stats),
        compiler_params=pltpu.CompilerParams(
            dimension_semantics=("parallel", "arbitrary"), vmem_limit_bytes=VMEM_LIMIT),
        name=name,
    )(*args)


def _attn_mid_call(q, k, v, max_dist, *, name):
    b, classes, per_class, w = q.shape
    outer = classes // MID_DIL
    nblk = per_class // MID_RUN
    group = max(1, min(MID_DIL, ATTN_BLOCKS // nblk))
    view = lambda t: t.reshape(b, outer, MID_DIL, nblk, MID_RUN, t.shape[-1])
    spec = lambda width: pl.BlockSpec((None, outer, group, nblk, MID_RUN, width),
                                      lambda i, e: (i, 0, e, 0, 0, 0))
    pairs = w // LANES
    bias = _band_bias(max_dist, mid=True)
    shape6 = (b, outer, MID_DIL, nblk, MID_RUN)
    kernel = functools.partial(_attn_kernel, pairs=pairs, slabs=pairs, group=group, nblk=nblk,
                               strided=True, has_sinks=False, has_gate=False, has_merge=False,
                               want_stats=True)
    o, st = pl.pallas_call(
        kernel,
        grid=(b, MID_DIL // group),
        in_specs=[spec(w), spec(w), spec(w), pl.BlockSpec(bias.shape, lambda i, e: (0, 0, 0))],
        out_specs=[spec(w), spec(LANES)],
        out_shape=[jax.ShapeDtypeStruct(shape6 + (w,), jnp.bfloat16),
                   jax.ShapeDtypeStruct(shape6 + (LANES,), jnp.bfloat16)],
        scratch_shapes=_attn_scratch(pairs, pairs, group * (nblk + 1), True),
        compiler_params=pltpu.CompilerParams(
            dimension_semantics=("parallel", "arbitrary"), vmem_limit_bytes=VMEM_LIMIT),
        name=name,
    )(view(q), view(k), view(v), bias)
    return o.reshape(b, classes, per_class, w), st.reshape(b, classes, per_class, LANES)


def _expand_heads(w, expand):
    hi, mid, lo = _split3(w)
    packed = hi + pltpu.roll(mid, B_HEADS, 1) + pltpu.roll(lo, 2 * B_HEADS, 1)
    return _dot(packed.astype(jnp.bfloat16), expand)


def _stat_sum(st):
    return (st + pltpu.roll(st, LANES - B_HEADS, 1) + pltpu.roll(st, LANES - 2 * B_HEADS, 1))


def _out_kernel(x_ref, ma_ref, o1_ref, s1_ref, oc_ref, sc_ref, gb_ref,
                permt_ref, expand_ref, w_ref, out_ref, ubuf, mbuf):
    permt = permt_ref[...]
    expand = expand_ref[...]
    head_lane = lax.broadcasted_iota(jnp.int32, (1, LANES), 1) < B_HEADS
    nloc = x_ref.shape[0] // LOCAL

    def to_token_order(j):
        sel = slice(j * BF16_ROWS, (j + 1) * BF16_ROWS)
        y = jnp.concatenate([jnp.concatenate([oc_ref[c, sel, :] for c in range(CLASSES)], axis=0),
                             jnp.concatenate([sc_ref[c, sel, :] for c in range(CLASSES)], axis=0)],
                            axis=1)
        ubuf[j % 2] = _dot(permt, y)

    def merge_gate(j):
        rows = slice(j * LOCAL, (j + 1) * LOCAL)
        o1 = o1_ref[rows, :].astype(jnp.float32)
        l1 = _stat_sum(s1_ref[rows, :].astype(jnp.float32))
        oc = ubuf[j % 2, :, 0:B_WIDTH]
        lc = _stat_sum(ubuf[j % 2, :, B_WIDTH:])
        top = jnp.maximum(l1, lc)
        e1, ec = jnp.exp2(l1 - top), jnp.exp2(lc - top)
        w1 = jnp.where(head_lane, e1 / (e1 + ec), 0.0)
        ob = oc + _expand_heads(w1, expand) * (o1 - oc)
        gb = gb_ref[rows, :]
        mbuf[j % 2] = (ob * (gb * jax.nn.sigmoid(gb))).astype(mbuf.dtype)

    def project(j):
        rows = slice(j * LOCAL, (j + 1) * LOCAL)
        out_ref[rows, :] = (x_ref[rows, :] + _dot(ma_ref[rows, :], w_ref[0:A_WIDTH, :])
                            + _dot(mbuf[j % 2], w_ref[A_WIDTH:, :]))

    for step in range(nloc + 2):
        if step - 2 >= 0:
            project(step - 2)
        if 0 <= step - 1 < nloc:
            merge_gate(step - 1)
        if step < nloc:
            to_token_order(step)


def _out_call(x2d, mixed_a, o1, s1, oc, sc, gate_b, permt, expand, w_bf, seq):
    tokens, d_model = x2d.shape
    tm = OUT_ROWS
    steps_per_seq = seq // tm
    row = lambda w: pl.BlockSpec((tm, w), lambda g: (g, 0))
    full = lambda a: pl.BlockSpec(a.shape, lambda g: (0,) * a.ndim)
    cmaj = lambda w: pl.BlockSpec((None, CLASSES, tm // CLASSES, w),
                                  lambda g: (g // steps_per_seq, 0, g % steps_per_seq, 0))
    return pl.pallas_call(
        _out_kernel,
        grid=(tokens // tm,),
        in_specs=[row(d_model), row(A_WIDTH), row(B_WIDTH), row(LANES),
                  cmaj(B_WIDTH), cmaj(LANES),
                  row(B_WIDTH), full(permt), full(expand), full(w_bf)],
        out_specs=row(d_model),
        out_shape=jax.ShapeDtypeStruct((tokens, d_model), jnp.float32),
        scratch_shapes=[pltpu.VMEM((2, LOCAL, B_WIDTH + LANES), jnp.float32),
                        pltpu.VMEM((2, LOCAL, B_WIDTH), jnp.bfloat16)],
        compiler_params=pltpu.CompilerParams(
            dimension_semantics=("parallel",), vmem_limit_bytes=VMEM_LIMIT),
        name="merge_out",
    )(x2d, mixed_a, o1, s1, oc, sc, gate_b, permt, expand, w_bf)


def _rope_tables(seq):
    inv = ROPE_THETA ** (-jnp.arange(HALF, dtype=jnp.float32) / HALF)
    ang = jnp.arange(seq).astype(jnp.float32)[:, None] * inv[None, :]
    reps = LANES // HALF
    cos_t = jnp.tile(jnp.cos(ang), (1, reps))
    sign = jnp.tile(jnp.concatenate([-jnp.ones((HALF,), jnp.float32), jnp.ones((HALF,), jnp.float32)]),
                    LANES // HEAD_DIM)
    sin_t = jnp.tile(jnp.sin(ang), (1, reps)) * sign[None, :]
    return cos_t, sin_t


def kernel(x, norm_gain, w_in, q_norm_a, k_norm_a, sinks_a, q_norm_b, k_norm_b, w_out):
    b, seq, d_model = x.shape
    tokens = b * seq
    depth = norm_gain.shape[0]
    assert [d for _, d in B_PATTERNS] == [1, MID_DIL, CLASSES]
    assert seq % PROJ_ROWS == 0 and PROJ_ROWS % LOCAL == 0 and (seq // CLASSES) % BLOCK == 0
    assert seq % OUT_ROWS == 0 and OUT_ROWS % LOCAL == 0
    scale = HEAD_DIM ** -0.5 * LOG2E
    cos_t, sin_t = _rope_tables(seq)
    seg_i = np.arange(SEG_TILE) // HEAD_DIM
    seg = jnp.asarray(seg_i[:, None] == seg_i[None, :], jnp.bfloat16)
    r = np.arange(LOCAL)
    perm_np = (CLASSES * (r % BF16_ROWS) + r // BF16_ROWS)[:, None] == r[None, :]
    perm = jnp.asarray(perm_np, jnp.bfloat16)
    permt = jnp.asarray(perm_np.T, jnp.bfloat16)
    term_head = np.where(np.arange(LANES) < STAT_LANES, np.arange(LANES) % B_HEADS, -1)
    expand = jnp.asarray(term_head[:, None] == (np.arange(B_WIDTH) // HEAD_DIM)[None, :], jnp.bfloat16)

    for i in range(depth):
        x2d = x.reshape(tokens, d_model)
        tile = lambda g, n, s=1.0: jnp.tile(g * s, n)[None, :]
        qa, ka, va, ga, qb, kb, vb, gb, qc, kc, vc = _proj_call(
            x2d, norm_gain[i][None, :], w_in[i].astype(jnp.bfloat16), seg, perm, cos_t, sin_t,
            tile(q_norm_a[i], A_HEADS, scale), tile(k_norm_a[i], A_KV_HEADS),
            tile(q_norm_b[i], B_HEADS, scale), tile(k_norm_b[i], B_HEADS), b, seq)

        r3 = lambda t: t.reshape(b, seq, t.shape[-1])
        (mixed_a,) = _attn_call(r3(qa), r3(ka), r3(va), A_WINDOW - 1,
                                sinks=sinks_a[i], gate=r3(ga), name="attn_a")

        (w1, d1), (w4, d4), (w16, d16) = B_PATTERNS
        o1, s1 = _attn_call(r3(qb), r3(kb), r3(vb), w1 // d1, want_stats=True, name="attn_b1")
        o4, s4 = _attn_mid_call(qc, kc, vc, w4 // d4, name="attn_b4")
        fold = lambda t: t.reshape(b * CLASSES, seq // CLASSES, t.shape[-1])
        oc, sc = _attn_call(fold(qc), fold(kc), fold(vc), w16 // d16, merge=(fold(o4), fold(s4)),
                            want_stats=True, name="attn_b16")
        unfold = lambda t: t.reshape(b, CLASSES, seq // CLASSES, t.shape[-1])

        out = _out_call(x2d, mixed_a.reshape(tokens, A_WIDTH),
                        o1.reshape(tokens, B_WIDTH), s1.reshape(tokens, LANES),
                        unfold(oc), unfold(sc), gb, permt, expand,
                        w_out[i].astype(jnp.bfloat16), seq)
        x = out.reshape(b, seq, d_model)
    return x
```

```python
import functools

import numpy as np
import jax
import jax.numpy as jnp
from jax import lax
from jax.experimental import pallas as pl
from jax.experimental.pallas import tpu as pltpu

HEAD_DIM = 64
HALF = HEAD_DIM // 2
A_HEADS = 8
A_KV_HEADS = 2
A_WINDOW = 128
B_HEADS = 8
B_PATTERNS = ((128, 1), (512, 4), (2048, 16))
BLOCK = 128
ROPE_THETA = 10000.0
EPS = 1e-6
NEG = -1e30
LOG2E = 1.4426950408889634

A_WIDTH = A_HEADS * HEAD_DIM
A_KV_WIDTH = A_KV_HEADS * HEAD_DIM
B_WIDTH = B_HEADS * HEAD_DIM

LANES = 128
BF16_ROWS = 16
SEG_TILE = 256
PROJ_ROWS = 512
OUT_ROWS = 1024
VMEM_LIMIT = 56 * 1024 * 1024

A_KV_PAIR_WIDTH = 2 * A_KV_HEADS * HEAD_DIM
assert A_KV_WIDTH == LANES

CLASSES = max(d for _, d in B_PATTERNS)
LOCAL = CLASSES * BF16_ROWS
MID_DIL = 4
MID_RUN = BLOCK // MID_DIL
STAT_LANES = 3 * B_HEADS


def _dot(a, b):
    return jnp.dot(a, b, preferred_element_type=jnp.float32)


def _split3(v):
    hi = v.astype(jnp.bfloat16).astype(jnp.float32)
    r1 = v - hi
    mid = r1.astype(jnp.bfloat16).astype(jnp.float32)
    lo = (r1 - mid).astype(jnp.bfloat16).astype(jnp.float32)
    return hi, mid, lo


def _head_norm_rope(p, seg, gain, cos, sin, hi_mask):
    width = p.shape[-1]
    sq = (p * p).astype(jnp.bfloat16)
    step = min(width, SEG_TILE)
    sums = []
    for c in range(0, width, step):
        sums.append(_dot(sq[:, c:c + step], seg[:step, :step]))
    ss = sums[0] if len(sums) == 1 else jnp.concatenate(sums, axis=-1)
    y = p * lax.rsqrt(ss * (1.0 / HEAD_DIM) + EPS) * gain
    outs = []
    for c in range(0, width, LANES):
        yc = y[:, c:c + LANES]
        partner = jnp.where(hi_mask, pltpu.roll(yc, HALF, 1), pltpu.roll(yc, LANES - HALF, 1))
        outs.append(yc * cos + partner * sin)
    return outs[0] if len(outs) == 1 else jnp.concatenate(outs, axis=-1)


def _proj_kernel(x_ref, gain_ref, w_ref, seg_ref, perm_ref, cos_ref, sin_ref,
                 gqa_ref, gka_ref, gqb_ref, gkb_ref,
                 qa_ref, ka_ref, va_ref, ga_ref, qb_ref, kb_ref, vb_ref, gb_ref,
                 qc_ref, kc_ref, vc_ref, pbuf):
    xf = x_ref[...]
    ms = jnp.mean(xf * xf, axis=-1, keepdims=True)
    h = (xf * lax.rsqrt(ms + EPS) * gain_ref[...]).astype(jnp.bfloat16)
    seg = seg_ref[...]
    perm = perm_ref[...]
    cos = cos_ref[...]
    sin = sin_ref[...]
    lane = lax.broadcasted_iota(jnp.int32, (1, LANES), 1)
    hi_mask = (lane & HALF) != 0
    low = lane < HEAD_DIM
    bf = jnp.bfloat16

    def per_pair(y):
        swapped = pltpu.roll(y, HEAD_DIM, 1)
        return jnp.concatenate([jnp.where(low, y, swapped), jnp.where(low, swapped, y)], axis=-1)

    def normed(gain_ref_):
        return lambda p: _head_norm_rope(p, seg, gain_ref_[...], cos, sin, hi_mask)

    plain = lambda p: p
    groups = [
        [(A_WIDTH, normed(gqa_ref), qa_ref, None)],
        [(A_KV_WIDTH, lambda p: per_pair(normed(gka_ref)(p)), ka_ref, None),
         (A_KV_WIDTH, per_pair, va_ref, None)],
        [(A_WIDTH, plain, ga_ref, None)],
        [(B_WIDTH, normed(gqb_ref), qb_ref, qc_ref)],
        [(B_WIDTH, normed(gkb_ref), kb_ref, kc_ref)],
        [(B_WIDTH, plain, vb_ref, vc_ref)],
        [(B_WIDTH, plain, gb_ref, None)],
    ]
    widths = [sum(part[0] for part in g) for g in groups]
    starts = np.cumsum([0] + widths)

    def project(i):
        pbuf[i % 2, :, 0:widths[i]] = _dot(h, w_ref[:, starts[i]:starts[i] + widths[i]])

    def finish(i):
        lo = 0
        for width, epilogue, nat_ref, _ in groups[i]:
            nat_ref[...] = epilogue(pbuf[i % 2, :, lo:lo + width]).astype(nat_ref.dtype)
            lo += width

    def to_class_major(i):
        _, _, nat_ref, cm_ref = groups[i][0]
        if cm_ref is None:
            return
        for j in range(nat_ref.shape[0] // LOCAL):
            z = _dot(perm, nat_ref[j * LOCAL:(j + 1) * LOCAL, :]).astype(bf)
            for c in range(CLASSES):
                cm_ref[c, j * BF16_ROWS:(j + 1) * BF16_ROWS, :] = z[c * BF16_ROWS:(c + 1) * BF16_ROWS, :]

    n = len(groups)
    for step in range(n + 2):
        if step < n:
            project(step)
        if 0 <= step - 1 < n:
            finish(step - 1)
        if 0 <= step - 2 < n:
            to_class_major(step - 2)


def _proj_call(x2d, gain, w_bf, seg, perm, cos_t, sin_t, gqa, gka, gqb, gkb, batch, seq):
    tokens, d_model = x2d.shape
    tm = PROJ_ROWS
    steps_per_seq = seq // tm
    row = lambda w: pl.BlockSpec((tm, w), lambda g: (g, 0))
    full = lambda a: pl.BlockSpec(a.shape, lambda g: (0,) * a.ndim)
    table = pl.BlockSpec((tm, LANES), lambda g: (g % steps_per_seq, 0))
    cmaj = pl.BlockSpec((None, CLASSES, tm // CLASSES, B_WIDTH),
                        lambda g: (g // steps_per_seq, 0, g % steps_per_seq, 0))
    bf = jnp.bfloat16
    cm_shape = jax.ShapeDtypeStruct((batch, CLASSES, seq // CLASSES, B_WIDTH), bf)
    out_shape = [
        jax.ShapeDtypeStruct((tokens, A_WIDTH), bf),
        jax.ShapeDtypeStruct((tokens, A_KV_PAIR_WIDTH), bf),
        jax.ShapeDtypeStruct((tokens, A_KV_PAIR_WIDTH), bf),
        jax.ShapeDtypeStruct((tokens, A_WIDTH), jnp.float32),
        jax.ShapeDtypeStruct((tokens, B_WIDTH), bf),
        jax.ShapeDtypeStruct((tokens, B_WIDTH), bf),
        jax.ShapeDtypeStruct((tokens, B_WIDTH), bf),
        jax.ShapeDtypeStruct((tokens, B_WIDTH), jnp.float32),
        cm_shape, cm_shape, cm_shape,
    ]
    return pl.pallas_call(
        _proj_kernel,
        grid=(tokens // tm,),
        in_specs=[row(d_model), full(gain), full(w_bf), full(seg), full(perm), table, table,
                  full(gqa), full(gka), full(gqb), full(gkb)],
        out_specs=[row(s.shape[1]) for s in out_shape[:8]] + [cmaj] * 3,
        out_shape=out_shape,
        scratch_shapes=[pltpu.VMEM((2, tm, max(A_WIDTH, B_WIDTH)), jnp.float32)],
        compiler_params=pltpu.CompilerParams(
            dimension_semantics=("parallel",), vmem_limit_bytes=VMEM_LIMIT),
        name="proj",
    )(x2d, gain, w_bf, seg, perm, cos_t, sin_t, gqa, gka, gqb, gkb)


ONES_ROWS = BF16_ROWS
ATTN_BLOCKS = 16


def _attn_kernel(*refs, pairs, slabs, group, nblk, strided, whole, has_sinks, has_gate, has_merge,
                 want_stats):
    it = iter(refs)
    if strided:
        q_ref, kc_ref, vc_ref, bias_ref = (next(it) for _ in range(4))
        kp_ref = vp_ref = None
    else:
        q_ref, kc_ref, kp_ref, vc_ref, vp_ref, bias_ref = (next(it) for _ in range(6))
    sink_ref = next(it) if has_sinks else None
    gate_ref = next(it) if has_gate else None
    oin_ref, sin_ref = (next(it), next(it)) if has_merge else (None, None)
    o_ref = next(it)
    st_ref = next(it) if want_stats else None
    vtbuf, sbuf, pbuf, mbuf = (next(it) for _ in range(4))
    lsebuf = next(it) if want_stats else None

    def block_of(ref, g, m, cols):
        if strided:
            return ref[:, g, m, :, cols].reshape(BLOCK, LANES)
        return ref[g, m * BLOCK:(m + 1) * BLOCK, cols]

    def key_block(cur_ref, prev_ref, g, kb, cols):
        if kb > 0:
            return block_of(cur_ref, g, kb - 1, cols)
        return block_of(cur_ref, g, 0, cols) if strided else prev_ref[g, :, cols]

    def store_block(ref, g, m, cols, val):
        if strided:
            ref[:, g, m, :, cols] = val.reshape(BLOCK // MID_RUN, MID_RUN, LANES)
        else:
            ref[g, m * BLOCK:(m + 1) * BLOCK, cols] = val

    if strided:
        first_bias = bias_ref[1]
    else:
        first_bias = bias_ref[jnp.where(pl.program_id(1) == 0, 1, 0)]

    ones = jnp.ones((ONES_ROWS, BLOCK), vtbuf.dtype)

    def transpose_values(g, kb):
        for sl in range(slabs):
            src = slice(sl * LANES, (sl + 1) * LANES)
            vtbuf[sl, g * (nblk + 1) + kb, 0:LANES, :] = key_block(vc_ref, vp_ref, g, kb, src).T
            vtbuf[sl, g * (nblk + 1) + kb, LANES:, :] = ones

    zeros = jnp.zeros((HEAD_DIM, BLOCK), q_ref.dtype)
    onehot = jnp.concatenate([bias_ref[2, 0:BLOCK], bias_ref[2, 0:BLOCK]], axis=1)

    def sink_row(hp):
        return jnp.concatenate([jnp.full((1, LANES), sink_ref[2 * hp] * LOG2E, jnp.float32),
                                jnp.full((1, LANES), sink_ref[2 * hp + 1] * LOG2E, jnp.float32)], axis=1)

    def window(j):
        return slice(BLOCK, 2 * BLOCK) if (whole and j == 0) else slice(0, 2 * BLOCK)

    def score_stage(t):
        slot, (g, j) = t % 2, divmod(t, nblk)
        win = window(j)
        bias_t = (first_bias if j == 0 else bias_ref[0])[win]
        for hp in range(pairs):
            sl = hp * slabs // pairs
            kcols = slice(sl * LANES, (sl + 1) * LANES)
            qt = block_of(q_ref, g, j, slice(hp * LANES, (hp + 1) * LANES)).T
            qm = jnp.concatenate([jnp.concatenate([qt[0:HEAD_DIM], zeros], axis=0),
                                  jnp.concatenate([zeros, qt[HEAD_DIM:]], axis=0)], axis=1)
            kw = key_block(kc_ref, kp_ref, g, j + 1, kcols)
            if win.start == 0:
                kw = jnp.concatenate([key_block(kc_ref, kp_ref, g, j, kcols), kw], axis=0)
            sbuf[slot, hp, win] = _dot(jnp.concatenate([kw, bias_t], axis=1),
                                       jnp.concatenate([qm, onehot], axis=0))

    def softmax_stage(t):
        slot, (g, j) = t % 2, divmod(t, nblk)
        win = window(j)
        if j == 0 and win.start == 0:
            transpose_values(g, 0)
        transpose_values(g, j + 1)
        for hp in range(pairs):
            m = jnp.max(sbuf[slot, hp, win], axis=0, keepdims=True)
            if has_sinks:
                m = jnp.maximum(m, sink_row(hp))
            pbuf[slot, hp, win] = jnp.exp2(sbuf[slot, hp, win] - m).astype(pbuf.dtype)
            mbuf[slot, hp] = m

    def value_stage(t):
        slot, (g, j) = t % 2, divmod(t, nblk)
        win = window(j)
        kb0 = g * (nblk + 1) + j
        if has_merge:
            terms = block_of(sin_ref, g, j, slice(0, LANES)).T.astype(jnp.float32)
            lse_in = (terms[0:B_HEADS] + terms[B_HEADS:2 * B_HEADS] + terms[2 * B_HEADS:3 * B_HEADS])
        for hp in range(pairs):
            sl = hp * slabs // pairs
            cols = slice(hp * LANES, (hp + 1) * LANES)
            m = mbuf[slot, hp]
            vt = vtbuf[sl, kb0 + 1]
            if win.start == 0:
                vt = jnp.concatenate([vtbuf[sl, kb0], vt], axis=1)
            r = _dot(vt, pbuf[slot, hp, win])
            l = r[LANES:LANES + 1, :]
            if has_sinks:
                l = l + jnp.exp2(sink_row(hp) - m)
            inv = 1.0 / l
            lse = m + jnp.log2(l)
            if has_merge:
                other = jnp.concatenate([lse_in[2 * hp:2 * hp + 1], lse_in[2 * hp + 1:2 * hp + 2]], axis=1)
                top = jnp.maximum(lse, other)
                mine, theirs = jnp.exp2(lse - top), jnp.exp2(other - top)
                total = mine + theirs
                inv, frac = inv * (mine / total), theirs / total
                lse = top + jnp.log2(total)
            ot = jnp.concatenate([r[0:HEAD_DIM, 0:LANES] * inv[:, 0:LANES],
                                  r[HEAD_DIM:LANES, LANES:] * inv[:, LANES:]], axis=0)
            if has_merge:
                oin_t = block_of(oin_ref, g, j, cols).T.astype(jnp.float32)
                ot = ot + oin_t * jnp.concatenate(
                    [jnp.broadcast_to(frac[:, 0:LANES], (HEAD_DIM, LANES)),
                     jnp.broadcast_to(frac[:, LANES:], (HEAD_DIM, LANES))], axis=0)
            if has_gate:
                gt = block_of(gate_ref, g, j, cols)
                o = (ot.T * (gt * jax.nn.sigmoid(gt))).astype(o_ref.dtype)
            else:
                o = ot.astype(o_ref.dtype).T
            store_block(o_ref, g, j, cols, o)
            if want_stats:
                lsebuf[slot, 2 * hp:2 * hp + 1, :] = lse[:, 0:LANES]
                lsebuf[slot, 2 * hp + 1:2 * hp + 2, :] = lse[:, LANES:]
        if want_stats:
            terms = _split3(lsebuf[slot])
            pad = jnp.zeros((LANES - STAT_LANES, BLOCK), jnp.float32)
            tile = jnp.concatenate(list(terms) + [pad], axis=0)
            store_block(st_ref, g, j, slice(0, LANES), tile.astype(st_ref.dtype).T)

    jobs = group * nblk
    for step in range(jobs + 2):
        if step - 2 >= 0:
            value_stage(step - 2)
        if 0 <= step - 1 < jobs:
            softmax_stage(step - 1)
        if step < jobs:
            score_stage(step)


def _band_bias(max_dist, mid):
    r = np.arange(BLOCK)
    pos = MID_DIL * (r % MID_RUN) + r // MID_RUN if mid else r
    qpos = pos[None, :] + BLOCK
    kpos = np.concatenate([pos, pos + BLOCK])[:, None]
    dist = qpos - kpos
    valid = (dist >= 0) & (dist <= max_dist)
    first = valid & (np.arange(2 * BLOCK)[:, None] >= BLOCK)
    masks = np.where(np.stack([valid, first]), 0.0, NEG)
    onehot = np.tile(np.eye(BLOCK), (2, 1))
    return jnp.asarray(np.concatenate([masks, onehot[None]]), jnp.bfloat16)


def _attn_scratch(pairs, slabs, key_blocks, want_stats):
    scratch = [pltpu.VMEM((slabs, key_blocks, LANES + ONES_ROWS, BLOCK), jnp.bfloat16),
               pltpu.VMEM((2, pairs, 2 * BLOCK, 2 * LANES), jnp.float32),
               pltpu.VMEM((2, pairs, 2 * BLOCK, 2 * LANES), jnp.bfloat16),
               pltpu.VMEM((2, pairs, 1, 2 * LANES), jnp.float32)]
    if want_stats:
        scratch.append(pltpu.VMEM((2, 2 * pairs, BLOCK), jnp.float32))
    return scratch


def _attn_call(q, k, v, max_dist, *, sinks=None, gate=None, merge=None, want_stats=False, name):
    n, seq, qw = q.shape
    kvw = k.shape[-1]
    pairs, slabs = qw // LANES, kvw // LANES
    tq = min(seq, ATTN_BLOCKS * BLOCK)
    sub = tq // BLOCK
    group = ATTN_BLOCKS // sub
    cur = lambda w: pl.BlockSpec((group, tq, w), lambda b, i: (b, i, 0))
    prev = lambda w: pl.BlockSpec((group, BLOCK, w), lambda b, i: (b, jnp.maximum(i * sub - 1, 0), 0))
    bias = _band_bias(max_dist, mid=False)
    args = [q, k, k, v, v, bias]
    in_specs = [cur(qw), cur(kvw), prev(kvw), cur(kvw), prev(kvw),
                pl.BlockSpec(bias.shape, lambda b, i: (0, 0, 0))]
    if sinks is not None:
        args.append(sinks)
        in_specs.append(pl.BlockSpec(memory_space=pltpu.SMEM))
    if gate is not None:
        args.append(gate)
        in_specs.append(cur(qw))
    if merge is not None:
        args.extend(merge)
        in_specs.extend([cur(qw), cur(LANES)])
    out_shape = [jax.ShapeDtypeStruct((n, seq, qw), jnp.bfloat16)]
    out_specs = [cur(qw)]
    if want_stats:
        out_shape.append(jax.ShapeDtypeStruct((n, seq, LANES), jnp.bfloat16))
        out_specs.append(cur(LANES))
    kernel = functools.partial(_attn_kernel, pairs=pairs, slabs=slabs, group=group, nblk=sub,
                               strided=False, whole=tq == seq, has_sinks=sinks is not None,
                               has_gate=gate is not None, has_merge=merge is not None,
                               want_stats=want_stats)
    return pl.pallas_call(
        kernel,
        grid=(n // group, seq // tq),
        in_specs=in_specs,
        out_specs=out_specs,
        out_shape=out_shape,
        scratch_shapes=_attn_scratch(pairs, slabs, group * (sub + 1), want_stats),
        compiler_params=pltpu.CompilerParams(
            dimension_semantics=("parallel", "arbitrary"), vmem_limit_bytes=VMEM_LIMIT),
        name=name,
    )(*args)


def _attn_mid_call(q, k, v, max_dist, *, name):
    b, classes, per_class, w = q.shape
    outer = classes // MID_DIL
    nblk = per_class // MID_RUN
    group = max(1, min(MID_DIL, ATTN_BLOCKS // nblk))
    view = lambda t: t.reshape(b, outer, MID_DIL, nblk, MID_RUN, t.shape[-1])
    spec = lambda width: pl.BlockSpec((None, outer, group, nblk, MID_RUN, width),
                                      lambda i, e: (i, 0, e, 0, 0, 0))
    pairs = w // LANES
    bias = _band_bias(max_dist, mid=True)
    shape6 = (b, outer, MID_DIL, nblk, MID_RUN)
    kernel = functools.partial(_attn_kernel, pairs=pairs, slabs=pairs, group=group, nblk=nblk,
                               strided=True, whole=True, has_sinks=False, has_gate=False,
                               has_merge=False, want_stats=True)
    o, st = pl.pallas_call(
        kernel,
        grid=(b, MID_DIL // group),
        in_specs=[spec(w), spec(w), spec(w), pl.BlockSpec(bias.shape, lambda i, e: (0, 0, 0))],
        out_specs=[spec(w), spec(LANES)],
        out_shape=[jax.ShapeDtypeStruct(shape6 + (w,), jnp.bfloat16),
                   jax.ShapeDtypeStruct(shape6 + (LANES,), jnp.bfloat16)],
        scratch_shapes=_attn_scratch(pairs, pairs, group * (nblk + 1), True),
        compiler_params=pltpu.CompilerParams(
            dimension_semantics=("parallel", "arbitrary"), vmem_limit_bytes=VMEM_LIMIT),
        name=name,
    )(view(q), view(k), view(v), bias)
    return o.reshape(b, classes, per_class, w), st.reshape(b, classes, per_class, LANES)


def _expand_heads(w, expand):
    hi, mid, lo = _split3(w)
    packed = hi + pltpu.roll(mid, B_HEADS, 1) + pltpu.roll(lo, 2 * B_HEADS, 1)
    return _dot(packed.astype(jnp.bfloat16), expand)


def _stat_sum(st):
    return (st + pltpu.roll(st, LANES - B_HEADS, 1) + pltpu.roll(st, LANES - 2 * B_HEADS, 1))


def _out_kernel(x_ref, ma_ref, o1_ref, s1_ref, oc_ref, sc_ref, gb_ref,
                permt_ref, expand_ref, w_ref, out_ref, ubuf, mbuf):
    permt = permt_ref[...]
    expand = expand_ref[...]
    head_lane = lax.broadcasted_iota(jnp.int32, (1, LANES), 1) < B_HEADS
    nloc = x_ref.shape[0] // LOCAL

    def to_token_order(j):
        sel = slice(j * BF16_ROWS, (j + 1) * BF16_ROWS)
        y = jnp.concatenate([jnp.concatenate([oc_ref[c, sel, :] for c in range(CLASSES)], axis=0),
                             jnp.concatenate([sc_ref[c, sel, :] for c in range(CLASSES)], axis=0)],
                            axis=1)
        ubuf[j % 2] = _dot(permt, y)

    def merge_gate(j):
        rows = slice(j * LOCAL, (j + 1) * LOCAL)
        o1 = o1_ref[rows, :].astype(jnp.float32)
        l1 = _stat_sum(s1_ref[rows, :].astype(jnp.float32))
        oc = ubuf[j % 2, :, 0:B_WIDTH]
        lc = _stat_sum(ubuf[j % 2, :, B_WIDTH:])
        top = jnp.maximum(l1, lc)
        e1, ec = jnp.exp2(l1 - top), jnp.exp2(lc - top)
        w1 = jnp.where(head_lane, e1 / (e1 + ec), 0.0)
        ob = oc + _expand_heads(w1, expand) * (o1 - oc)
        gb = gb_ref[rows, :]
        mbuf[j % 2] = (ob * (gb * jax.nn.sigmoid(gb))).astype(mbuf.dtype)

    def project(j):
        rows = slice(j * LOCAL, (j + 1) * LOCAL)
        out_ref[rows, :] = (x_ref[rows, :] + _dot(ma_ref[rows, :], w_ref[0:A_WIDTH, :])
                            + _dot(mbuf[j % 2], w_ref[A_WIDTH:, :]))

    for step in range(nloc + 2):
        if step - 2 >= 0:
            project(step - 2)
        if 0 <= step - 1 < nloc:
            merge_gate(step - 1)
        if step < nloc:
            to_token_order(step)


def _out_call(x2d, mixed_a, o1, s1, oc, sc, gate_b, permt, expand, w_bf, seq):
    tokens, d_model = x2d.shape
    tm = OUT_ROWS
    steps_per_seq = seq // tm
    row = lambda w: pl.BlockSpec((tm, w), lambda g: (g, 0))
    full = lambda a: pl.BlockSpec(a.shape, lambda g: (0,) * a.ndim)
    cmaj = lambda w: pl.BlockSpec((None, CLASSES, tm // CLASSES, w),
                                  lambda g: (g // steps_per_seq, 0, g % steps_per_seq, 0))
    return pl.pallas_call(
        _out_kernel,
        grid=(tokens // tm,),
        in_specs=[row(d_model), row(A_WIDTH), row(B_WIDTH), row(LANES),
                  cmaj(B_WIDTH), cmaj(LANES),
                  row(B_WIDTH), full(permt), full(expand), full(w_bf)],
        out_specs=row(d_model),
        out_shape=jax.ShapeDtypeStruct((tokens, d_model), jnp.float32),
        scratch_shapes=[pltpu.VMEM((2, LOCAL, B_WIDTH + LANES), jnp.float32),
                        pltpu.VMEM((2, LOCAL, B_WIDTH), jnp.bfloat16)],
        compiler_params=pltpu.CompilerParams(
            dimension_semantics=("parallel",), vmem_limit_bytes=VMEM_LIMIT),
        name="merge_out",
    )(x2d, mixed_a, o1, s1, oc, sc, gate_b, permt, expand, w_bf)


def _rope_tables(seq):
    inv = ROPE_THETA ** (-jnp.arange(HALF, dtype=jnp.float32) / HALF)
    ang = jnp.arange(seq).astype(jnp.float32)[:, None] * inv[None, :]
    reps = LANES // HALF
    cos_t = jnp.tile(jnp.cos(ang), (1, reps))
    sign = jnp.tile(jnp.concatenate([-jnp.ones((HALF,), jnp.float32), jnp.ones((HALF,), jnp.float32)]),
                    LANES // HEAD_DIM)
    sin_t = jnp.tile(jnp.sin(ang), (1, reps)) * sign[None, :]
    return cos_t, sin_t


def kernel(x, norm_gain, w_in, q_norm_a, k_norm_a, sinks_a, q_norm_b, k_norm_b, w_out):
    b, seq, d_model = x.shape
    tokens = b * seq
    depth = norm_gain.shape[0]
    assert [d for _, d in B_PATTERNS] == [1, MID_DIL, CLASSES]
    assert seq % PROJ_ROWS == 0 and PROJ_ROWS % LOCAL == 0 and (seq // CLASSES) % BLOCK == 0
    assert seq % OUT_ROWS == 0 and OUT_ROWS % LOCAL == 0
    scale = HEAD_DIM ** -0.5 * LOG2E
    cos_t, sin_t = _rope_tables(seq)
    seg_i = np.arange(SEG_TILE) // HEAD_DIM
    seg = jnp.asarray(seg_i[:, None] == seg_i[None, :], jnp.bfloat16)
    r = np.arange(LOCAL)
    perm_np = (CLASSES * (r % BF16_ROWS) + r // BF16_ROWS)[:, None] == r[None, :]
    perm = jnp.asarray(perm_np, jnp.bfloat16)
    permt = jnp.asarray(perm_np.T, jnp.bfloat16)
    term_head = np.where(np.arange(LANES) < STAT_LANES, np.arange(LANES) % B_HEADS, -1)
    expand = jnp.asarray(term_head[:, None] == (np.arange(B_WIDTH) // HEAD_DIM)[None, :], jnp.bfloat16)

    for i in range(depth):
        x2d = x.reshape(tokens, d_model)
        tile = lambda g, n, s=1.0: jnp.tile(g * s, n)[None, :]
        qa, ka, va, ga, qb, kb, vb, gb, qc, kc, vc = _proj_call(
            x2d, norm_gain[i][None, :], w_in[i].astype(jnp.bfloat16), seg, perm, cos_t, sin_t,
            tile(q_norm_a[i], A_HEADS, scale), tile(k_norm_a[i], A_KV_HEADS),
            tile(q_norm_b[i], B_HEADS, scale), tile(k_norm_b[i], B_HEADS), b, seq)

        r3 = lambda t: t.reshape(b, seq, t.shape[-1])
        (mixed_a,) = _attn_call(r3(qa), r3(ka), r3(va), A_WINDOW - 1,
                                sinks=sinks_a[i], gate=r3(ga), name="attn_a")

        (w1, d1), (w4, d4), (w16, d16) = B_PATTERNS
        o1, s1 = _attn_call(r3(qb), r3(kb), r3(vb), w1 // d1, want_stats=True, name="attn_b1")
        o4, s4 = _attn_mid_call(qc, kc, vc, w4 // d4, name="attn_b4")
        fold = lambda t: t.reshape(b * CLASSES, seq // CLASSES, t.shape[-1])
        oc, sc = _attn_call(fold(qc), fold(kc), fold(vc), w16 // d16, merge=(fold(o4), fold(s4)),
                            want_stats=True, name="attn_b16")
        unfold = lambda t: t.reshape(b, CLASSES, seq // CLASSES, t.shape[-1])

        out = _out_call(x2d, mixed_a.reshape(tokens, A_WIDTH),
                        o1.reshape(tokens, B_WIDTH), s1.reshape(tokens, LANES),
                        unfold(oc), unfold(sc), gb, permt, expand,
                        w_out[i].astype(jnp.bfloat16), seq)
        x = out.reshape(b, seq, d_model)
    return x
```

```python
import functools

import numpy as np
import jax
import jax.numpy as jnp
from jax import lax
from jax.experimental import pallas as pl
from jax.experimental.pallas import tpu as pltpu

HEAD_DIM = 64
HALF = HEAD_DIM // 2
A_HEADS = 8
A_KV_HEADS = 2
A_WINDOW = 128
B_HEADS = 8
B_PATTERNS = ((128, 1), (512, 4), (2048, 16))
BLOCK = 128
ROPE_THETA = 10000.0
EPS = 1e-6
NEG = -1e30
LOG2E = 1.4426950408889634

A_WIDTH = A_HEADS * HEAD_DIM
A_KV_WIDTH = A_KV_HEADS * HEAD_DIM
B_WIDTH = B_HEADS * HEAD_DIM

LANES = 128
BF16_ROWS = 16
SEG_TILE = 256
PROJ_ROWS = 512
OUT_ROWS = 1024
VMEM_LIMIT = 56 * 1024 * 1024

A_KV_PAIR_WIDTH = 2 * A_KV_HEADS * HEAD_DIM
assert A_KV_WIDTH == LANES

CLASSES = max(d for _, d in B_PATTERNS)
LOCAL = CLASSES * BF16_ROWS
MID_DIL = 4
MID_RUN = BLOCK // MID_DIL
STAT_LANES = 3 * B_HEADS


def _dot(a, b):
    return jnp.dot(a, b, preferred_element_type=jnp.float32)


def _split3(v):
    hi = v.astype(jnp.bfloat16).astype(jnp.float32)
    r1 = v - hi
    mid = r1.astype(jnp.bfloat16).astype(jnp.float32)
    lo = (r1 - mid).astype(jnp.bfloat16).astype(jnp.float32)
    return hi, mid, lo


def _head_norm_rope(p, seg, gain, cos, sin, hi_mask):
    width = p.shape[-1]
    sq = (p * p).astype(jnp.bfloat16)
    step = min(width, SEG_TILE)
    sums = []
    for c in range(0, width, step):
        sums.append(_dot(sq[:, c:c + step], seg[:step, :step]))
    ss = sums[0] if len(sums) == 1 else jnp.concatenate(sums, axis=-1)
    y = p * lax.rsqrt(ss * (1.0 / HEAD_DIM) + EPS) * gain
    outs = []
    for c in range(0, width, LANES):
        yc = y[:, c:c + LANES]
        partner = jnp.where(hi_mask, pltpu.roll(yc, HALF, 1), pltpu.roll(yc, LANES - HALF, 1))
        outs.append(yc * cos + partner * sin)
    return outs[0] if len(outs) == 1 else jnp.concatenate(outs, axis=-1)


def _proj_kernel(x_ref, gain_ref, w_ref, seg_ref, perm_ref, cos_ref, sin_ref,
                 gqa_ref, gka_ref, gqb_ref, gkb_ref,
                 qa_ref, ka_ref, va_ref, ga_ref, qb_ref, kb_ref, vb_ref, gb_ref,
                 qc_ref, kc_ref, vc_ref, pbuf):
    xf = x_ref[...]
    ms = jnp.mean(xf * xf, axis=-1, keepdims=True)
    h = (xf * lax.rsqrt(ms + EPS) * gain_ref[...]).astype(jnp.bfloat16)
    seg = seg_ref[...]
    perm = perm_ref[...]
    cos = cos_ref[...]
    sin = sin_ref[...]
    lane = lax.broadcasted_iota(jnp.int32, (1, LANES), 1)
    hi_mask = (lane & HALF) != 0
    low = lane < HEAD_DIM
    bf = jnp.bfloat16

    def per_pair(y):
        swapped = pltpu.roll(y, HEAD_DIM, 1)
        return jnp.concatenate([jnp.where(low, y, swapped), jnp.where(low, swapped, y)], axis=-1)

    def normed(gain_ref_):
        return lambda p: _head_norm_rope(p, seg, gain_ref_[...], cos, sin, hi_mask)

    plain = lambda p: p
    groups = [
        [(A_WIDTH, normed(gqa_ref), qa_ref, None)],
        [(A_KV_WIDTH, lambda p: per_pair(normed(gka_ref)(p)), ka_ref, None),
         (A_KV_WIDTH, per_pair, va_ref, None)],
        [(A_WIDTH, plain, ga_ref, None)],
        [(B_WIDTH, normed(gqb_ref), qb_ref, qc_ref)],
        [(B_WIDTH, normed(gkb_ref), kb_ref, kc_ref)],
        [(B_WIDTH, plain, vb_ref, vc_ref)],
        [(B_WIDTH, plain, gb_ref, None)],
    ]
    widths = [sum(part[0] for part in g) for g in groups]
    starts = np.cumsum([0] + widths)

    def project(i):
        pbuf[i % 2, :, 0:widths[i]] = _dot(h, w_ref[:, starts[i]:starts[i] + widths[i]])

    def finish(i):
        lo = 0
        for width, epilogue, nat_ref, _ in groups[i]:
            nat_ref[...] = epilogue(pbuf[i % 2, :, lo:lo + width]).astype(nat_ref.dtype)
            lo += width

    def to_class_major(i):
        _, _, nat_ref, cm_ref = groups[i][0]
        if cm_ref is None:
            return
        for j in range(nat_ref.shape[0] // LOCAL):
            z = _dot(perm, nat_ref[j * LOCAL:(j + 1) * LOCAL, :]).astype(bf)
            for c in range(CLASSES):
                cm_ref[c, j * BF16_ROWS:(j + 1) * BF16_ROWS, :] = z[c * BF16_ROWS:(c + 1) * BF16_ROWS, :]

    n = len(groups)
    for step in range(n + 2):
        if step < n:
            project(step)
        if 0 <= step - 1 < n:
            finish(step - 1)
        if 0 <= step - 2 < n:
            to_class_major(step - 2)


def _proj_call(x2d, gain, w_bf, seg, perm, cos_t, sin_t, gqa, gka, gqb, gkb, batch, seq):
    tokens, d_model = x2d.shape
    tm = PROJ_ROWS
    steps_per_seq = seq // tm
    row = lambda w: pl.BlockSpec((tm, w), lambda g: (g, 0))
    full = lambda a: pl.BlockSpec(a.shape, lambda g: (0,) * a.ndim)
    table = pl.BlockSpec((tm, LANES), lambda g: (g % steps_per_seq, 0))
    cmaj = pl.BlockSpec((None, CLASSES, tm // CLASSES, B_WIDTH),
                        lambda g: (g // steps_per_seq, 0, g % steps_per_seq, 0))
    bf = jnp.bfloat16
    cm_shape = jax.ShapeDtypeStruct((batch, CLASSES, seq // CLASSES, B_WIDTH), bf)
    out_shape = [
        jax.ShapeDtypeStruct((tokens, A_WIDTH), bf),
        jax.ShapeDtypeStruct((tokens, A_KV_PAIR_WIDTH), bf),
        jax.ShapeDtypeStruct((tokens, A_KV_PAIR_WIDTH), bf),
        jax.ShapeDtypeStruct((tokens, A_WIDTH), jnp.float32),
        jax.ShapeDtypeStruct((tokens, B_WIDTH), bf),
        jax.ShapeDtypeStruct((tokens, B_WIDTH), bf),
        jax.ShapeDtypeStruct((tokens, B_WIDTH), bf),
        jax.ShapeDtypeStruct((tokens, B_WIDTH), jnp.float32),
        cm_shape, cm_shape, cm_shape,
    ]
    return pl.pallas_call(
        _proj_kernel,
        grid=(tokens // tm,),
        in_specs=[row(d_model), full(gain), full(w_bf), full(seg), full(perm), table, table,
                  full(gqa), full(gka), full(gqb), full(gkb)],
        out_specs=[row(s.shape[1]) for s in out_shape[:8]] + [cmaj] * 3,
        out_shape=out_shape,
        scratch_shapes=[pltpu.VMEM((2, tm, max(A_WIDTH, B_WIDTH)), jnp.float32)],
        compiler_params=pltpu.CompilerParams(
            dimension_semantics=("parallel",), vmem_limit_bytes=VMEM_LIMIT),
        name="proj",
    )(x2d, gain, w_bf, seg, perm, cos_t, sin_t, gqa, gka, gqb, gkb)


ONES_ROWS = BF16_ROWS
ATTN_BLOCKS = 16


def _attn_kernel(*refs, pairs, slabs, group, nblk, strided, whole, has_sinks, has_gate, has_merge,
                 want_stats):
    it = iter(refs)
    if strided:
        q_ref, kc_ref, vc_ref, bias_ref = (next(it) for _ in range(4))
        kp_ref = vp_ref = None
    else:
        q_ref, kc_ref, kp_ref, vc_ref, vp_ref, bias_ref = (next(it) for _ in range(6))
    sink_ref = next(it) if has_sinks else None
    gate_ref = next(it) if has_gate else None
    oin_ref, sin_ref = (next(it), next(it)) if has_merge else (None, None)
    o_ref = next(it)
    st_ref = next(it) if want_stats else None
    vtbuf, sbuf, pbuf, mbuf = (next(it) for _ in range(4))
    lsebuf = next(it) if want_stats else None

    def block_of(ref, g, m, cols):
        if strided:
            return ref[:, g, m, :, cols].reshape(BLOCK, LANES)
        return ref[g, m * BLOCK:(m + 1) * BLOCK, cols]

    def key_block(cur_ref, prev_ref, g, kb, cols):
        if kb > 0:
            return block_of(cur_ref, g, kb - 1, cols)
        return block_of(cur_ref, g, 0, cols) if strided else prev_ref[g, :, cols]

    def store_block(ref, g, m, cols, val):
        if strided:
            ref[:, g, m, :, cols] = val.reshape(BLOCK // MID_RUN, MID_RUN, LANES)
        else:
            ref[g, m * BLOCK:(m + 1) * BLOCK, cols] = val

    if strided:
        first_bias = bias_ref[1]
    else:
        first_bias = bias_ref[jnp.where(pl.program_id(1) == 0, 1, 0)]

    ones = jnp.ones((ONES_ROWS, BLOCK), vtbuf.dtype)

    def transpose_values(g, kb):
        for sl in range(slabs):
            src = slice(sl * LANES, (sl + 1) * LANES)
            vtbuf[sl, g * (nblk + 1) + kb, 0:LANES, :] = key_block(vc_ref, vp_ref, g, kb, src).T
            vtbuf[sl, g * (nblk + 1) + kb, LANES:, :] = ones

    zeros = jnp.zeros((HEAD_DIM, BLOCK), q_ref.dtype)
    onehot = jnp.concatenate([bias_ref[2, 0:BLOCK], bias_ref[2, 0:BLOCK]], axis=1)

    def sink_row(hp):
        return jnp.concatenate([jnp.full((1, LANES), sink_ref[2 * hp] * LOG2E, jnp.float32),
                                jnp.full((1, LANES), sink_ref[2 * hp + 1] * LOG2E, jnp.float32)], axis=1)

    def window(j):
        return slice(BLOCK, 2 * BLOCK) if (whole and j == 0) else slice(0, 2 * BLOCK)

    def score_stage(t):
        slot, (g, j) = t % 2, divmod(t, nblk)
        win = window(j)
        bias_t = (first_bias if j == 0 else bias_ref[0])[win]
        for hp in range(pairs):
            sl = hp * slabs // pairs
            kcols = slice(sl * LANES, (sl + 1) * LANES)
            qt = block_of(q_ref, g, j, slice(hp * LANES, (hp + 1) * LANES)).T
            qm = jnp.concatenate([jnp.concatenate([qt[0:HEAD_DIM], zeros], axis=0),
                                  jnp.concatenate([zeros, qt[HEAD_DIM:]], axis=0)], axis=1)
            kw = key_block(kc_ref, kp_ref, g, j + 1, kcols)
            if win.start == 0:
                kw = jnp.concatenate([key_block(kc_ref, kp_ref, g, j, kcols), kw], axis=0)
            sbuf[slot, hp, win] = _dot(jnp.concatenate([kw, bias_t], axis=1),
                                       jnp.concatenate([qm, onehot], axis=0))

    def softmax_stage(t):
        slot, (g, j) = t % 2, divmod(t, nblk)
        win = window(j)
        if j == 0 and win.start == 0:
            transpose_values(g, 0)
        transpose_values(g, j + 1)
        for hp in range(pairs):
            m = jnp.max(sbuf[slot, hp, win], axis=0, keepdims=True)
            if has_sinks:
                m = jnp.maximum(m, sink_row(hp))
            pbuf[slot, hp, win] = jnp.exp2(sbuf[slot, hp, win] - m).astype(pbuf.dtype)
            mbuf[slot, hp] = m

    def value_stage(t):
        slot, (g, j) = t % 2, divmod(t, nblk)
        win = window(j)
        kb0 = g * (nblk + 1) + j
        if has_merge:
            terms = block_of(sin_ref, g, j, slice(0, LANES)).T.astype(jnp.float32)
            lse_in = (terms[0:B_HEADS] + terms[B_HEADS:2 * B_HEADS] + terms[2 * B_HEADS:3 * B_HEADS])
        for hp in range(pairs):
            sl = hp * slabs // pairs
            cols = slice(hp * LANES, (hp + 1) * LANES)
            m = mbuf[slot, hp]
            vt = vtbuf[sl, kb0 + 1]
            if win.start == 0:
                vt = jnp.concatenate([vtbuf[sl, kb0], vt], axis=1)
            r = _dot(vt, pbuf[slot, hp, win])
            l = r[LANES:LANES + 1, :]
            if has_sinks:
                l = l + jnp.exp2(sink_row(hp) - m)
            inv = 1.0 / l
            lse = m + jnp.log2(l)
            if has_merge:
                other = jnp.concatenate([lse_in[2 * hp:2 * hp + 1], lse_in[2 * hp + 1:2 * hp + 2]], axis=1)
                top = jnp.maximum(lse, other)
                mine, theirs = jnp.exp2(lse - top), jnp.exp2(other - top)
                total = mine + theirs
                inv, frac = inv * (mine / total), theirs / total
                lse = top + jnp.log2(total)
            ot = jnp.concatenate([r[0:HEAD_DIM, 0:LANES] * inv[:, 0:LANES],
                                  r[HEAD_DIM:LANES, LANES:] * inv[:, LANES:]], axis=0)
            if has_merge:
                oin_t = block_of(oin_ref, g, j, cols).T.astype(jnp.float32)
                ot = ot + oin_t * jnp.concatenate(
                    [jnp.broadcast_to(frac[:, 0:LANES], (HEAD_DIM, LANES)),
                     jnp.broadcast_to(frac[:, LANES:], (HEAD_DIM, LANES))], axis=0)
            if has_gate:
                gt = block_of(gate_ref, g, j, cols)
                o = (ot.T * (gt * jax.nn.sigmoid(gt))).astype(o_ref.dtype)
            else:
                o = ot.astype(o_ref.dtype).T
            store_block(o_ref, g, j, cols, o)
            if want_stats:
                lsebuf[slot, 2 * hp:2 * hp + 1, :] = lse[:, 0:LANES]
                lsebuf[slot, 2 * hp + 1:2 * hp + 2, :] = lse[:, LANES:]
        if want_stats:
            terms = _split3(lsebuf[slot])
            pad = jnp.zeros((LANES - STAT_LANES, BLOCK), jnp.float32)
            tile = jnp.concatenate(list(terms) + [pad], axis=0)
            store_block(st_ref, g, j, slice(0, LANES), tile.astype(st_ref.dtype).T)

    jobs = group * nblk
    for step in range(jobs + 2):
        if step - 2 >= 0:
            value_stage(step - 2)
        if 0 <= step - 1 < jobs:
            softmax_stage(step - 1)
        if step < jobs:
            score_stage(step)


def _band_bias(max_dist, mid):
    r = np.arange(BLOCK)
    pos = MID_DIL * (r % MID_RUN) + r // MID_RUN if mid else r
    qpos = pos[None, :] + BLOCK
    kpos = np.concatenate([pos, pos + BLOCK])[:, None]
    dist = qpos - kpos
    valid = (dist >= 0) & (dist <= max_dist)
    first = valid & (np.arange(2 * BLOCK)[:, None] >= BLOCK)
    masks = np.where(np.stack([valid, first]), 0.0, NEG)
    onehot = np.tile(np.eye(BLOCK), (2, 1))
    return jnp.asarray(np.concatenate([masks, onehot[None]]), jnp.bfloat16)


def _attn_scratch(pairs, slabs, key_blocks, want_stats):
    scratch = [pltpu.VMEM((slabs, key_blocks, LANES + ONES_ROWS, BLOCK), jnp.bfloat16),
               pltpu.VMEM((2, pairs, 2 * BLOCK, 2 * LANES), jnp.float32),
               pltpu.VMEM((2, pairs, 2 * BLOCK, 2 * LANES), jnp.bfloat16),
               pltpu.VMEM((2, pairs, 1, 2 * LANES), jnp.float32)]
    if want_stats:
        scratch.append(pltpu.VMEM((2, 2 * pairs, BLOCK), jnp.float32))
    return scratch


def _attn_call(q, k, v, max_dist, *, sinks=None, gate=None, merge=None, want_stats=False, name):
    n, seq, qw = q.shape
    kvw = k.shape[-1]
    pairs, slabs = qw // LANES, kvw // LANES
    tq = min(seq, ATTN_BLOCKS * BLOCK)
    sub = tq // BLOCK
    group = ATTN_BLOCKS // sub
    cur = lambda w: pl.BlockSpec((group, tq, w), lambda b, i: (b, i, 0))
    prev = lambda w: pl.BlockSpec((group, BLOCK, w), lambda b, i: (b, jnp.maximum(i * sub - 1, 0), 0))
    bias = _band_bias(max_dist, mid=False)
    args = [q, k, k, v, v, bias]
    in_specs = [cur(qw), cur(kvw), prev(kvw), cur(kvw), prev(kvw),
                pl.BlockSpec(bias.shape, lambda b, i: (0, 0, 0))]
    if sinks is not None:
        args.append(sinks)
        in_specs.append(pl.BlockSpec(memory_space=pltpu.SMEM))
    if gate is not None:
        args.append(gate)
        in_specs.append(cur(qw))
    if merge is not None:
        args.extend(merge)
        in_specs.extend([cur(qw), cur(LANES)])
    out_shape = [jax.ShapeDtypeStruct((n, seq, qw), jnp.bfloat16)]
    out_specs = [cur(qw)]
    if want_stats:
        out_shape.append(jax.ShapeDtypeStruct((n, seq, LANES), jnp.bfloat16))
        out_specs.append(cur(LANES))
    kernel = functools.partial(_attn_kernel, pairs=pairs, slabs=slabs, group=group, nblk=sub,
                               strided=False, whole=tq == seq, has_sinks=sinks is not None,
                               has_gate=gate is not None, has_merge=merge is not None,
                               want_stats=want_stats)
    return pl.pallas_call(
        kernel,
        grid=(n // group, seq // tq),
        in_specs=in_specs,
        out_specs=out_specs,
        out_shape=out_shape,
        scratch_shapes=_attn_scratch(pairs, slabs, group * (sub + 1), want_stats),
        compiler_params=pltpu.CompilerParams(
            dimension_semantics=("parallel", "arbitrary"), vmem_limit_bytes=VMEM_LIMIT),
        name=name,
    )(*args)


def _attn_mid_call(q, k, v, max_dist, *, name):
    b, classes, per_class, w = q.shape
    outer = classes // MID_DIL
    nblk = per_class // MID_RUN
    group = max(1, min(MID_DIL, ATTN_BLOCKS // nblk))
    view = lambda t: t.reshape(b, outer, MID_DIL, nblk, MID_RUN, t.shape[-1])
    spec = lambda width: pl.BlockSpec((None, outer, group, nblk, MID_RUN, width),
                                      lambda i, e: (i, 0, e, 0, 0, 0))
    pairs = w // LANES
    bias = _band_bias(max_dist, mid=True)
    shape6 = (b, outer, MID_DIL, nblk, MID_RUN)
    kernel = functools.partial(_attn_kernel, pairs=pairs, slabs=pairs, group=group, nblk=nblk,
                               strided=True, whole=True, has_sinks=False, has_gate=False,
                               has_merge=False, want_stats=True)
    o, st = pl.pallas_call(
        kernel,
        grid=(b, MID_DIL // group),
        in_specs=[spec(w), spec(w), spec(w), pl.BlockSpec(bias.shape, lambda i, e: (0, 0, 0))],
        out_specs=[spec(w), spec(LANES)],
        out_shape=[jax.ShapeDtypeStruct(shape6 + (w,), jnp.bfloat16),
                   jax.ShapeDtypeStruct(shape6 + (LANES,), jnp.bfloat16)],
        scratch_shapes=_attn_scratch(pairs, pairs, group * (nblk + 1), True),
        compiler_params=pltpu.CompilerParams(
            dimension_semantics=("parallel", "arbitrary"), vmem_limit_bytes=VMEM_LIMIT),
        name=name,
    )(view(q), view(k), view(v), bias)
    return o.reshape(b, classes, per_class, w), st.reshape(b, classes, per_class, LANES)


def _expand_heads(w, expand):
    hi, mid, lo = _split3(w)
    packed = hi + pltpu.roll(mid, B_HEADS, 1) + pltpu.roll(lo, 2 * B_HEADS, 1)
    return _dot(packed.astype(jnp.bfloat16), expand)


def _stat_sum(st):
    return (st + pltpu.roll(st, LANES - B_HEADS, 1) + pltpu.roll(st, LANES - 2 * B_HEADS, 1))


def _out_kernel(x_ref, ma_ref, o1_ref, s1_ref, oc_ref, sc_ref, gb_ref,
                permt_ref, expand_ref, w_ref, out_ref, ubuf, mbuf):
    permt = permt_ref[...]
    expand = expand_ref[...]
    head_lane = lax.broadcasted_iota(jnp.int32, (1, LANES), 1) < B_HEADS
    nloc = x_ref.shape[0] // LOCAL

    def to_token_order(j):
        sel = slice(j * BF16_ROWS, (j + 1) * BF16_ROWS)
        y = jnp.concatenate([jnp.concatenate([oc_ref[c, sel, :] for c in range(CLASSES)], axis=0),
                             jnp.concatenate([sc_ref[c, sel, :] for c in range(CLASSES)], axis=0)],
                            axis=1)
        ubuf[j % 2] = _dot(permt, y)

    def merge_gate(j):
        rows = slice(j * LOCAL, (j + 1) * LOCAL)
        o1 = o1_ref[rows, :].astype(jnp.float32)
        l1 = _stat_sum(s1_ref[rows, :].astype(jnp.float32))
        oc = ubuf[j % 2, :, 0:B_WIDTH]
        lc = _stat_sum(ubuf[j % 2, :, B_WIDTH:])
        top = jnp.maximum(l1, lc)
        e1, ec = jnp.exp2(l1 - top), jnp.exp2(lc - top)
        w1 = jnp.where(head_lane, e1 / (e1 + ec), 0.0)
        ob = oc + _expand_heads(w1, expand) * (o1 - oc)
        gb = gb_ref[rows, :]
        mbuf[j % 2] = (ob * (gb * jax.nn.sigmoid(gb))).astype(mbuf.dtype)

    def project(j):
        rows = slice(j * LOCAL, (j + 1) * LOCAL)
        out_ref[rows, :] = (x_ref[rows, :] + _dot(ma_ref[rows, :], w_ref[0:A_WIDTH, :])
                            + _dot(mbuf[j % 2], w_ref[A_WIDTH:, :]))

    for step in range(nloc + 2):
        if step - 2 >= 0:
            project(step - 2)
        if 0 <= step - 1 < nloc:
            merge_gate(step - 1)
        if step < nloc:
            to_token_order(step)


def _attn_out_kernel(*refs, n_attn_in, attn_kwargs):
    attn_in = refs[:n_attn_in]
    x_ref, o1_ref, s1_ref, oc_ref, sc_ref, gb_ref, permt_ref, expand_ref, w_ref, out_ref = refs[n_attn_in:n_attn_in + 10]
    scratch = refs[n_attn_in + 10:]
    attn_scratch, (ma_buf, ubuf, mbuf) = scratch[:-3], scratch[-3:]
    _attn_kernel(*attn_in, ma_buf, *attn_scratch, **attn_kwargs)
    _out_kernel(x_ref, ma_buf.at[0], o1_ref, s1_ref, oc_ref, sc_ref, gb_ref,
                permt_ref, expand_ref, w_ref, out_ref, ubuf, mbuf)


def _attn_out_call(x2d, q, k, v, max_dist, sinks, gate, o1, s1, oc, sc, gate_b, permt, expand, w_bf):
    tokens, d_model = x2d.shape
    n, seq, qw = q.shape
    kvw = k.shape[-1]
    pairs, slabs = qw // LANES, kvw // LANES
    tq = OUT_ROWS
    sub = tq // BLOCK
    steps = seq // tq
    cur = lambda w: pl.BlockSpec((1, tq, w), lambda b, i: (b, i, 0))
    prev = lambda w: pl.BlockSpec((1, BLOCK, w), lambda b, i: (b, jnp.maximum(i * sub - 1, 0), 0))
    row = lambda w: pl.BlockSpec((tq, w), lambda b, i: (b * steps + i, 0))
    full = lambda a: pl.BlockSpec(a.shape, lambda b, i: (0,) * a.ndim)
    cmaj = lambda w: pl.BlockSpec((None, CLASSES, tq // CLASSES, w), lambda b, i: (b, 0, i, 0))
    bias = _band_bias(max_dist, mid=False)
    attn_args = [q, k, k, v, v, bias, sinks, gate]
    attn_specs = [cur(qw), cur(kvw), prev(kvw), cur(kvw), prev(kvw), full(bias),
                  pl.BlockSpec(memory_space=pltpu.SMEM), cur(qw)]
    attn_kwargs = dict(pairs=pairs, slabs=slabs, group=1, nblk=sub, strided=False, whole=tq == seq,
                       has_sinks=True, has_gate=True, has_merge=False, want_stats=False)
    kernel = functools.partial(_attn_out_kernel, n_attn_in=len(attn_args), attn_kwargs=attn_kwargs)
    return pl.pallas_call(
        kernel,
        grid=(n, steps),
        in_specs=attn_specs + [row(d_model), row(B_WIDTH), row(LANES), cmaj(B_WIDTH), cmaj(LANES),
                               row(B_WIDTH), full(permt), full(expand), full(w_bf)],
        out_specs=row(d_model),
        out_shape=jax.ShapeDtypeStruct((tokens, d_model), jnp.float32),
        scratch_shapes=_attn_scratch(pairs, slabs, sub + 1, False) + [
            pltpu.VMEM((1, tq, qw), jnp.bfloat16),
            pltpu.VMEM((2, LOCAL, B_WIDTH + LANES), jnp.float32),
            pltpu.VMEM((2, LOCAL, B_WIDTH), jnp.bfloat16)],
        compiler_params=pltpu.CompilerParams(
            dimension_semantics=("parallel", "arbitrary"), vmem_limit_bytes=VMEM_LIMIT),
        name="attn_a_out",
    )(*attn_args, x2d, o1, s1, oc, sc, gate_b, permt, expand, w_bf)


def _rope_tables(seq):
    inv = ROPE_THETA ** (-jnp.arange(HALF, dtype=jnp.float32) / HALF)
    ang = jnp.arange(seq).astype(jnp.float32)[:, None] * inv[None, :]
    reps = LANES // HALF
    cos_t = jnp.tile(jnp.cos(ang), (1, reps))
    sign = jnp.tile(jnp.concatenate([-jnp.ones((HALF,), jnp.float32), jnp.ones((HALF,), jnp.float32)]),
                    LANES // HEAD_DIM)
    sin_t = jnp.tile(jnp.sin(ang), (1, reps)) * sign[None, :]
    return cos_t, sin_t


def kernel(x, norm_gain, w_in, q_norm_a, k_norm_a, sinks_a, q_norm_b, k_norm_b, w_out):
    b, seq, d_model = x.shape
    tokens = b * seq
    depth = norm_gain.shape[0]
    assert [d for _, d in B_PATTERNS] == [1, MID_DIL, CLASSES]
    assert seq % PROJ_ROWS == 0 and PROJ_ROWS % LOCAL == 0 and (seq // CLASSES) % BLOCK == 0
    assert seq % OUT_ROWS == 0 and OUT_ROWS % LOCAL == 0
    scale = HEAD_DIM ** -0.5 * LOG2E
    cos_t, sin_t = _rope_tables(seq)
    seg_i = np.arange(SEG_TILE) // HEAD_DIM
    seg = jnp.asarray(seg_i[:, None] == seg_i[None, :], jnp.bfloat16)
    r = np.arange(LOCAL)
    perm_np = (CLASSES * (r % BF16_ROWS) + r // BF16_ROWS)[:, None] == r[None, :]
    perm = jnp.asarray(perm_np, jnp.bfloat16)
    permt = jnp.asarray(perm_np.T, jnp.bfloat16)
    term_head = np.where(np.arange(LANES) < STAT_LANES, np.arange(LANES) % B_HEADS, -1)
    expand = jnp.asarray(term_head[:, None] == (np.arange(B_WIDTH) // HEAD_DIM)[None, :], jnp.bfloat16)

    for i in range(depth):
        x2d = x.reshape(tokens, d_model)
        tile = lambda g, n, s=1.0: jnp.tile(g * s, n)[None, :]
        qa, ka, va, ga, qb, kb, vb, gb, qc, kc, vc = _proj_call(
            x2d, norm_gain[i][None, :], w_in[i].astype(jnp.bfloat16), seg, perm, cos_t, sin_t,
            tile(q_norm_a[i], A_HEADS, scale), tile(k_norm_a[i], A_KV_HEADS),
            tile(q_norm_b[i], B_HEADS, scale), tile(k_norm_b[i], B_HEADS), b, seq)

        r3 = lambda t: t.reshape(b, seq, t.shape[-1])
        (w1, d1), (w4, d4), (w16, d16) = B_PATTERNS
        o1, s1 = _attn_call(r3(qb), r3(kb), r3(vb), w1 // d1, want_stats=True, name="attn_b1")
        o4, s4 = _attn_mid_call(qc, kc, vc, w4 // d4, name="attn_b4")
        fold = lambda t: t.reshape(b * CLASSES, seq // CLASSES, t.shape[-1])
        oc, sc = _attn_call(fold(qc), fold(kc), fold(vc), w16 // d16, merge=(fold(o4), fold(s4)),
                            want_stats=True, name="attn_b16")
        unfold = lambda t: t.reshape(b, CLASSES, seq // CLASSES, t.shape[-1])

        out = _attn_out_call(x2d, r3(qa), r3(ka), r3(va), A_WINDOW - 1, sinks_a[i], r3(ga),
                             o1.reshape(tokens, B_WIDTH), s1.reshape(tokens, LANES),
                             unfold(oc), unfold(sc), gb, permt, expand, w_out[i].astype(jnp.bfloat16))
        x = out.reshape(b, seq, d_model)
    return x
```

```python
import functools

import numpy as np
import jax
import jax.numpy as jnp
from jax import lax
from jax.experimental import pallas as pl
from jax.experimental.pallas import tpu as pltpu

HEAD_DIM = 64
HALF = HEAD_DIM // 2
A_HEADS = 8
A_KV_HEADS = 2
A_WINDOW = 128
B_HEADS = 8
B_PATTERNS = ((128, 1), (512, 4), (2048, 16))
BLOCK = 128
ROPE_THETA = 10000.0
EPS = 1e-6
NEG = -1e30
LOG2E = 1.4426950408889634

A_WIDTH = A_HEADS * HEAD_DIM
A_KV_WIDTH = A_KV_HEADS * HEAD_DIM
B_WIDTH = B_HEADS * HEAD_DIM

LANES = 128
BF16_ROWS = 16
SEG_TILE = 256
PROJ_ROWS = 1024
OUT_ROWS = 1024
VMEM_LIMIT = 56 * 1024 * 1024

A_KV_PAIR_WIDTH = 2 * A_KV_HEADS * HEAD_DIM
assert A_KV_WIDTH == LANES

CLASSES = max(d for _, d in B_PATTERNS)
LOCAL = CLASSES * BF16_ROWS
MID_DIL = 4
MID_RUN = BLOCK // MID_DIL
STAT_LANES = 3 * B_HEADS


def _dot(a, b):
    return jnp.dot(a, b, preferred_element_type=jnp.float32)


def _split3(v):
    hi = v.astype(jnp.bfloat16).astype(jnp.float32)
    r1 = v - hi
    mid = r1.astype(jnp.bfloat16).astype(jnp.float32)
    lo = (r1 - mid).astype(jnp.bfloat16).astype(jnp.float32)
    return hi, mid, lo


def _head_norm_rope(p, seg, gain, cos, sin, hi_mask):
    width = p.shape[-1]
    sq = (p * p).astype(jnp.bfloat16)
    step = min(width, SEG_TILE)
    sums = []
    for c in range(0, width, step):
        sums.append(_dot(sq[:, c:c + step], seg[:step, :step]))
    ss = sums[0] if len(sums) == 1 else jnp.concatenate(sums, axis=-1)
    y = p * lax.rsqrt(ss * (1.0 / HEAD_DIM) + EPS) * gain
    outs = []
    for c in range(0, width, LANES):
        yc = y[:, c:c + LANES]
        partner = jnp.where(hi_mask, pltpu.roll(yc, HALF, 1), pltpu.roll(yc, LANES - HALF, 1))
        outs.append(yc * cos + partner * sin)
    return outs[0] if len(outs) == 1 else jnp.concatenate(outs, axis=-1)


def _proj_kernel(x_ref, gain_ref, w_ref, seg_ref, perm_ref, cos_ref, sin_ref,
                 gqa_ref, gka_ref, gqb_ref, gkb_ref,
                 qa_ref, ka_ref, va_ref, ga_ref, qb_ref, kb_ref, vb_ref, gb_ref,
                 qc_ref, kc_ref, vc_ref, pbuf):
    xf = x_ref[...]
    ms = jnp.mean(xf * xf, axis=-1, keepdims=True)
    h = (xf * lax.rsqrt(ms + EPS) * gain_ref[...]).astype(jnp.bfloat16)
    seg = seg_ref[...]
    perm = perm_ref[...]
    cos = cos_ref[...]
    sin = sin_ref[...]
    lane = lax.broadcasted_iota(jnp.int32, (1, LANES), 1)
    hi_mask = (lane & HALF) != 0
    low = lane < HEAD_DIM
    bf = jnp.bfloat16

    def per_pair(y):
        swapped = pltpu.roll(y, HEAD_DIM, 1)
        return jnp.concatenate([jnp.where(low, y, swapped), jnp.where(low, swapped, y)], axis=-1)

    def normed(gain_ref_):
        return lambda p: _head_norm_rope(p, seg, gain_ref_[...], cos, sin, hi_mask)

    plain = lambda p: p
    groups = [
        [(A_WIDTH, normed(gqa_ref), qa_ref, None)],
        [(A_KV_WIDTH, lambda p: per_pair(normed(gka_ref)(p)), ka_ref, None),
         (A_KV_WIDTH, per_pair, va_ref, None)],
        [(A_WIDTH, plain, ga_ref, None)],
        [(B_WIDTH, normed(gqb_ref), qb_ref, qc_ref)],
        [(B_WIDTH, normed(gkb_ref), kb_ref, kc_ref)],
        [(B_WIDTH, plain, vb_ref, vc_ref)],
        [(B_WIDTH, plain, gb_ref, None)],
    ]
    widths = [sum(part[0] for part in g) for g in groups]
    starts = np.cumsum([0] + widths)

    def project(i):
        pbuf[i % 2, :, 0:widths[i]] = _dot(h, w_ref[:, starts[i]:starts[i] + widths[i]])

    def finish(i):
        lo = 0
        for width, epilogue, nat_ref, _ in groups[i]:
            nat_ref[...] = epilogue(pbuf[i % 2, :, lo:lo + width]).astype(nat_ref.dtype)
            lo += width

    def to_class_major(i):
        _, _, nat_ref, cm_ref = groups[i][0]
        if cm_ref is None:
            return
        for j in range(nat_ref.shape[0] // LOCAL):
            z = _dot(perm, nat_ref[j * LOCAL:(j + 1) * LOCAL, :]).astype(bf)
            for c in range(CLASSES):
                cm_ref[c, j * BF16_ROWS:(j + 1) * BF16_ROWS, :] = z[c * BF16_ROWS:(c + 1) * BF16_ROWS, :]

    n = len(groups)
    for step in range(n + 2):
        if step < n:
            project(step)
        if 0 <= step - 1 < n:
            finish(step - 1)
        if 0 <= step - 2 < n:
            to_class_major(step - 2)


def _proj_call(x2d, gain, w_bf, seg, perm, cos_t, sin_t, gqa, gka, gqb, gkb, batch, seq):
    tokens, d_model = x2d.shape
    tm = PROJ_ROWS
    steps_per_seq = seq // tm
    row = lambda w: pl.BlockSpec((tm, w), lambda g: (g, 0))
    full = lambda a: pl.BlockSpec(a.shape, lambda g: (0,) * a.ndim)
    table = pl.BlockSpec((tm, LANES), lambda g: (g % steps_per_seq, 0))
    cmaj = pl.BlockSpec((None, CLASSES, tm // CLASSES, B_WIDTH),
                        lambda g: (g // steps_per_seq, 0, g % steps_per_seq, 0))
    bf = jnp.bfloat16
    cm_shape = jax.ShapeDtypeStruct((batch, CLASSES, seq // CLASSES, B_WIDTH), bf)
    out_shape = [
        jax.ShapeDtypeStruct((tokens, A_WIDTH), bf),
        jax.ShapeDtypeStruct((tokens, A_KV_PAIR_WIDTH), bf),
        jax.ShapeDtypeStruct((tokens, A_KV_PAIR_WIDTH), bf),
        jax.ShapeDtypeStruct((tokens, A_WIDTH), jnp.float32),
        jax.ShapeDtypeStruct((tokens, B_WIDTH), bf),
        jax.ShapeDtypeStruct((tokens, B_WIDTH), bf),
        jax.ShapeDtypeStruct((tokens, B_WIDTH), bf),
        jax.ShapeDtypeStruct((tokens, B_WIDTH), jnp.float32),
        cm_shape, cm_shape, cm_shape,
    ]
    return pl.pallas_call(
        _proj_kernel,
        grid=(tokens // tm,),
        in_specs=[row(d_model), full(gain), full(w_bf), full(seg), full(perm), table, table,
                  full(gqa), full(gka), full(gqb), full(gkb)],
        out_specs=[row(s.shape[1]) for s in out_shape[:8]] + [cmaj] * 3,
        out_shape=out_shape,
        scratch_shapes=[pltpu.VMEM((2, tm, max(A_WIDTH, B_WIDTH)), jnp.float32)],
        compiler_params=pltpu.CompilerParams(
            dimension_semantics=("parallel",), vmem_limit_bytes=VMEM_LIMIT),
        name="proj",
    )(x2d, gain, w_bf, seg, perm, cos_t, sin_t, gqa, gka, gqb, gkb)


ONES_ROWS = BF16_ROWS
ATTN_BLOCKS = 16


def _attn_kernel(*refs, pairs, slabs, group, nblk, strided, whole, has_sinks, has_gate, has_merge,
                 want_stats):
    it = iter(refs)
    if strided:
        q_ref, kc_ref, vc_ref, bias_ref = (next(it) for _ in range(4))
        kp_ref = vp_ref = None
    else:
        q_ref, kc_ref, kp_ref, vc_ref, vp_ref, bias_ref = (next(it) for _ in range(6))
    sink_ref = next(it) if has_sinks else None
    gate_ref = next(it) if has_gate else None
    oin_ref, sin_ref = (next(it), next(it)) if has_merge else (None, None)
    o_ref = next(it)
    st_ref = next(it) if want_stats else None
    vtbuf, sbuf, pbuf, mbuf = (next(it) for _ in range(4))
    lsebuf = next(it) if want_stats else None

    def block_of(ref, g, m, cols):
        if strided:
            return ref[:, g, m, :, cols].reshape(BLOCK, LANES)
        return ref[g, m * BLOCK:(m + 1) * BLOCK, cols]

    def key_block(cur_ref, prev_ref, g, kb, cols):
        if kb > 0:
            return block_of(cur_ref, g, kb - 1, cols)
        return block_of(cur_ref, g, 0, cols) if strided else prev_ref[g, :, cols]

    def store_block(ref, g, m, cols, val):
        if strided:
            ref[:, g, m, :, cols] = val.reshape(BLOCK // MID_RUN, MID_RUN, LANES)
        else:
            ref[g, m * BLOCK:(m + 1) * BLOCK, cols] = val

    if strided:
        first_bias = bias_ref[1]
    else:
        first_bias = bias_ref[jnp.where(pl.program_id(1) == 0, 1, 0)]

    ones = jnp.ones((ONES_ROWS, BLOCK), vtbuf.dtype)

    def transpose_values(g, kb):
        for sl in range(slabs):
            src = slice(sl * LANES, (sl + 1) * LANES)
            vtbuf[sl, g * (nblk + 1) + kb, 0:LANES, :] = key_block(vc_ref, vp_ref, g, kb, src).T
            vtbuf[sl, g * (nblk + 1) + kb, LANES:, :] = ones

    zeros = jnp.zeros((HEAD_DIM, BLOCK), q_ref.dtype)
    onehot = jnp.concatenate([bias_ref[2, 0:BLOCK], bias_ref[2, 0:BLOCK]], axis=1)

    def sink_row(hp):
        return jnp.concatenate([jnp.full((1, LANES), sink_ref[2 * hp] * LOG2E, jnp.float32),
                                jnp.full((1, LANES), sink_ref[2 * hp + 1] * LOG2E, jnp.float32)], axis=1)

    def window(j):
        return slice(BLOCK, 2 * BLOCK) if (whole and j == 0) else slice(0, 2 * BLOCK)

    def score_stage(t):
        slot, (g, j) = t % 2, divmod(t, nblk)
        win = window(j)
        bias_t = (first_bias if j == 0 else bias_ref[0])[win]
        for hp in range(pairs):
            sl = hp * slabs // pairs
            kcols = slice(sl * LANES, (sl + 1) * LANES)
            qt = block_of(q_ref, g, j, slice(hp * LANES, (hp + 1) * LANES)).T
            qm = jnp.concatenate([jnp.concatenate([qt[0:HEAD_DIM], zeros], axis=0),
                                  jnp.concatenate([zeros, qt[HEAD_DIM:]], axis=0)], axis=1)
            kw = key_block(kc_ref, kp_ref, g, j + 1, kcols)
            if win.start == 0:
                kw = jnp.concatenate([key_block(kc_ref, kp_ref, g, j, kcols), kw], axis=0)
            sbuf[slot, hp, win] = _dot(jnp.concatenate([kw, bias_t], axis=1),
                                       jnp.concatenate([qm, onehot], axis=0))

    def softmax_stage(t):
        slot, (g, j) = t % 2, divmod(t, nblk)
        win = window(j)
        if j == 0 and win.start == 0:
            transpose_values(g, 0)
        transpose_values(g, j + 1)
        for hp in range(pairs):
            m = jnp.max(sbuf[slot, hp, win], axis=0, keepdims=True)
            if has_sinks:
                m = jnp.maximum(m, sink_row(hp))
            pbuf[slot, hp, win] = jnp.exp2(sbuf[slot, hp, win] - m).astype(pbuf.dtype)
            mbuf[slot, hp] = m

    def value_stage(t):
        slot, (g, j) = t % 2, divmod(t, nblk)
        win = window(j)
        kb0 = g * (nblk + 1) + j
        if has_merge:
            terms = block_of(sin_ref, g, j, slice(0, LANES)).T.astype(jnp.float32)
            lse_in = (terms[0:B_HEADS] + terms[B_HEADS:2 * B_HEADS] + terms[2 * B_HEADS:3 * B_HEADS])
        for hp in range(pairs):
            sl = hp * slabs // pairs
            cols = slice(hp * LANES, (hp + 1) * LANES)
            m = mbuf[slot, hp]
            vt = vtbuf[sl, kb0 + 1]
            if win.start == 0:
                vt = jnp.concatenate([vtbuf[sl, kb0], vt], axis=1)
            r = _dot(vt, pbuf[slot, hp, win])
            l = r[LANES:LANES + 1, :]
            if has_sinks:
                l = l + jnp.exp2(sink_row(hp) - m)
            inv = 1.0 / l
            lse = m + jnp.log2(l)
            if has_merge:
                other = jnp.concatenate([lse_in[2 * hp:2 * hp + 1], lse_in[2 * hp + 1:2 * hp + 2]], axis=1)
                top = jnp.maximum(lse, other)
                mine, theirs = jnp.exp2(lse - top), jnp.exp2(other - top)
                total = mine + theirs
                inv, frac = inv * (mine / total), theirs / total
                lse = top + jnp.log2(total)
            ot = jnp.concatenate([r[0:HEAD_DIM, 0:LANES] * inv[:, 0:LANES],
                                  r[HEAD_DIM:LANES, LANES:] * inv[:, LANES:]], axis=0)
            if has_merge:
                oin_t = block_of(oin_ref, g, j, cols).T.astype(jnp.float32)
                ot = ot + oin_t * jnp.concatenate(
                    [jnp.broadcast_to(frac[:, 0:LANES], (HEAD_DIM, LANES)),
                     jnp.broadcast_to(frac[:, LANES:], (HEAD_DIM, LANES))], axis=0)
            if has_gate:
                gt = block_of(gate_ref, g, j, cols)
                o = (ot.T * (gt * jax.nn.sigmoid(gt))).astype(o_ref.dtype)
            else:
                o = ot.astype(o_ref.dtype).T
            store_block(o_ref, g, j, cols, o)
            if want_stats:
                lsebuf[slot, 2 * hp:2 * hp + 1, :] = lse[:, 0:LANES]
                lsebuf[slot, 2 * hp + 1:2 * hp + 2, :] = lse[:, LANES:]
        if want_stats:
            terms = _split3(lsebuf[slot])
            pad = jnp.zeros((LANES - STAT_LANES, BLOCK), jnp.float32)
            tile = jnp.concatenate(list(terms) + [pad], axis=0)
            store_block(st_ref, g, j, slice(0, LANES), tile.astype(st_ref.dtype).T)

    jobs = group * nblk
    for step in range(jobs + 2):
        if step - 2 >= 0:
            value_stage(step - 2)
        if 0 <= step - 1 < jobs:
            softmax_stage(step - 1)
        if step < jobs:
            score_stage(step)


def _band_bias(max_dist, mid):
    r = np.arange(BLOCK)
    pos = MID_DIL * (r % MID_RUN) + r // MID_RUN if mid else r
    qpos = pos[None, :] + BLOCK
    kpos = np.concatenate([pos, pos + BLOCK])[:, None]
    dist = qpos - kpos
    valid = (dist >= 0) & (dist <= max_dist)
    first = valid & (np.arange(2 * BLOCK)[:, None] >= BLOCK)
    masks = np.where(np.stack([valid, first]), 0.0, NEG)
    onehot = np.tile(np.eye(BLOCK), (2, 1))
    return jnp.asarray(np.concatenate([masks, onehot[None]]), jnp.bfloat16)


def _attn_scratch(pairs, slabs, key_blocks, want_stats):
    scratch = [pltpu.VMEM((slabs, key_blocks, LANES + ONES_ROWS, BLOCK), jnp.bfloat16),
               pltpu.VMEM((2, pairs, 2 * BLOCK, 2 * LANES), jnp.float32),
               pltpu.VMEM((2, pairs, 2 * BLOCK, 2 * LANES), jnp.bfloat16),
               pltpu.VMEM((2, pairs, 1, 2 * LANES), jnp.float32)]
    if want_stats:
        scratch.append(pltpu.VMEM((2, 2 * pairs, BLOCK), jnp.float32))
    return scratch


def _attn_call(q, k, v, max_dist, *, sinks=None, gate=None, merge=None, want_stats=False, name):
    n, seq, qw = q.shape
    kvw = k.shape[-1]
    pairs, slabs = qw // LANES, kvw // LANES
    tq = min(seq, ATTN_BLOCKS * BLOCK)
    sub = tq // BLOCK
    group = ATTN_BLOCKS // sub
    cur = lambda w: pl.BlockSpec((group, tq, w), lambda b, i: (b, i, 0))
    prev = lambda w: pl.BlockSpec((group, BLOCK, w), lambda b, i: (b, jnp.maximum(i * sub - 1, 0), 0))
    bias = _band_bias(max_dist, mid=False)
    args = [q, k, k, v, v, bias]
    in_specs = [cur(qw), cur(kvw), prev(kvw), cur(kvw), prev(kvw),
                pl.BlockSpec(bias.shape, lambda b, i: (0, 0, 0))]
    if sinks is not None:
        args.append(sinks)
        in_specs.append(pl.BlockSpec(memory_space=pltpu.SMEM))
    if gate is not None:
        args.append(gate)
        in_specs.append(cur(qw))
    if merge is not None:
        args.extend(merge)
        in_specs.extend([cur(qw), cur(LANES)])
    out_shape = [jax.ShapeDtypeStruct((n, seq, qw), jnp.bfloat16)]
    out_specs = [cur(qw)]
    if want_stats:
        out_shape.append(jax.ShapeDtypeStruct((n, seq, LANES), jnp.bfloat16))
        out_specs.append(cur(LANES))
    kernel = functools.partial(_attn_kernel, pairs=pairs, slabs=slabs, group=group, nblk=sub,
                               strided=False, whole=tq == seq, has_sinks=sinks is not None,
                               has_gate=gate is not None, has_merge=merge is not None,
                               want_stats=want_stats)
    return pl.pallas_call(
        kernel,
        grid=(n // group, seq // tq),
        in_specs=in_specs,
        out_specs=out_specs,
        out_shape=out_shape,
        scratch_shapes=_attn_scratch(pairs, slabs, group * (sub + 1), want_stats),
        compiler_params=pltpu.CompilerParams(
            dimension_semantics=("parallel", "arbitrary"), vmem_limit_bytes=VMEM_LIMIT),
        name=name,
    )(*args)


def _attn_mid_call(q, k, v, max_dist, *, name):
    b, classes, per_class, w = q.shape
    outer = classes // MID_DIL
    nblk = per_class // MID_RUN
    group = max(1, min(MID_DIL, ATTN_BLOCKS // nblk))
    view = lambda t: t.reshape(b, outer, MID_DIL, nblk, MID_RUN, t.shape[-1])
    spec = lambda width: pl.BlockSpec((None, outer, group, nblk, MID_RUN, width),
                                      lambda i, e: (i, 0, e, 0, 0, 0))
    pairs = w // LANES
    bias = _band_bias(max_dist, mid=True)
    shape6 = (b, outer, MID_DIL, nblk, MID_RUN)
    kernel = functools.partial(_attn_kernel, pairs=pairs, slabs=pairs, group=group, nblk=nblk,
                               strided=True, whole=True, has_sinks=False, has_gate=False,
                               has_merge=False, want_stats=True)
    o, st = pl.pallas_call(
        kernel,
        grid=(b, MID_DIL // group),
        in_specs=[spec(w), spec(w), spec(w), pl.BlockSpec(bias.shape, lambda i, e: (0, 0, 0))],
        out_specs=[spec(w), spec(LANES)],
        out_shape=[jax.ShapeDtypeStruct(shape6 + (w,), jnp.bfloat16),
                   jax.ShapeDtypeStruct(shape6 + (LANES,), jnp.bfloat16)],
        scratch_shapes=_attn_scratch(pairs, pairs, group * (nblk + 1), True),
        compiler_params=pltpu.CompilerParams(
            dimension_semantics=("parallel", "arbitrary"), vmem_limit_bytes=VMEM_LIMIT),
        name=name,
    )(view(q), view(k), view(v), bias)
    return o.reshape(b, classes, per_class, w), st.reshape(b, classes, per_class, LANES)


def _expand_heads(w, expand):
    hi, mid, lo = _split3(w)
    packed = hi + pltpu.roll(mid, B_HEADS, 1) + pltpu.roll(lo, 2 * B_HEADS, 1)
    return _dot(packed.astype(jnp.bfloat16), expand)


def _stat_sum(st):
    return (st + pltpu.roll(st, LANES - B_HEADS, 1) + pltpu.roll(st, LANES - 2 * B_HEADS, 1))


def _out_kernel(x_ref, ma_ref, o1_ref, s1_ref, oc_ref, sc_ref, gb_ref,
                permt_ref, expand_ref, w_ref, out_ref, ubuf, mbuf):
    permt = permt_ref[...]
    expand = expand_ref[...]
    head_lane = lax.broadcasted_iota(jnp.int32, (1, LANES), 1) < B_HEADS
    nloc = x_ref.shape[0] // LOCAL

    def to_token_order(j):
        sel = slice(j * BF16_ROWS, (j + 1) * BF16_ROWS)
        y = jnp.concatenate([jnp.concatenate([oc_ref[c, sel, :] for c in range(CLASSES)], axis=0),
                             jnp.concatenate([sc_ref[c, sel, :] for c in range(CLASSES)], axis=0)],
                            axis=1)
        ubuf[j % 2] = _dot(permt, y)

    def merge_gate(j):
        rows = slice(j * LOCAL, (j + 1) * LOCAL)
        o1 = o1_ref[rows, :].astype(jnp.float32)
        l1 = _stat_sum(s1_ref[rows, :].astype(jnp.float32))
        oc = ubuf[j % 2, :, 0:B_WIDTH]
        lc = _stat_sum(ubuf[j % 2, :, B_WIDTH:])
        top = jnp.maximum(l1, lc)
        e1, ec = jnp.exp2(l1 - top), jnp.exp2(lc - top)
        w1 = jnp.where(head_lane, e1 / (e1 + ec), 0.0)
        ob = oc + _expand_heads(w1, expand) * (o1 - oc)
        gb = gb_ref[rows, :]
        mbuf[j % 2] = (ob * (gb * jax.nn.sigmoid(gb))).astype(mbuf.dtype)

    def project(j):
        rows = slice(j * LOCAL, (j + 1) * LOCAL)
        out_ref[rows, :] = (x_ref[rows, :] + _dot(ma_ref[rows, :], w_ref[0:A_WIDTH, :])
                            + _dot(mbuf[j % 2], w_ref[A_WIDTH:, :]))

    for step in range(nloc + 2):
        if step - 2 >= 0:
            project(step - 2)
        if 0 <= step - 1 < nloc:
            merge_gate(step - 1)
        if step < nloc:
            to_token_order(step)


def _attn_out_kernel(*refs, n_attn_in, attn_kwargs):
    attn_in = refs[:n_attn_in]
    x_ref, o1_ref, s1_ref, oc_ref, sc_ref, gb_ref, permt_ref, expand_ref, w_ref, out_ref = refs[n_attn_in:n_attn_in + 10]
    scratch = refs[n_attn_in + 10:]
    attn_scratch, (ma_buf, ubuf, mbuf) = scratch[:-3], scratch[-3:]
    _attn_kernel(*attn_in, ma_buf, *attn_scratch, **attn_kwargs)
    _out_kernel(x_ref, ma_buf.at[0], o1_ref, s1_ref, oc_ref, sc_ref, gb_ref,
                permt_ref, expand_ref, w_ref, out_ref, ubuf, mbuf)


def _attn_out_call(x2d, q, k, v, max_dist, sinks, gate, o1, s1, oc, sc, gate_b, permt, expand, w_bf):
    tokens, d_model = x2d.shape
    n, seq, qw = q.shape
    kvw = k.shape[-1]
    pairs, slabs = qw // LANES, kvw // LANES
    tq = OUT_ROWS
    sub = tq // BLOCK
    steps = seq // tq
    cur = lambda w: pl.BlockSpec((1, tq, w), lambda b, i: (b, i, 0))
    prev = lambda w: pl.BlockSpec((1, BLOCK, w), lambda b, i: (b, jnp.maximum(i * sub - 1, 0), 0))
    row = lambda w: pl.BlockSpec((tq, w), lambda b, i: (b * steps + i, 0))
    full = lambda a: pl.BlockSpec(a.shape, lambda b, i: (0,) * a.ndim)
    cmaj = lambda w: pl.BlockSpec((None, CLASSES, tq // CLASSES, w), lambda b, i: (b, 0, i, 0))
    bias = _band_bias(max_dist, mid=False)
    attn_args = [q, k, k, v, v, bias, sinks, gate]
    attn_specs = [cur(qw), cur(kvw), prev(kvw), cur(kvw), prev(kvw), full(bias),
                  pl.BlockSpec(memory_space=pltpu.SMEM), cur(qw)]
    attn_kwargs = dict(pairs=pairs, slabs=slabs, group=1, nblk=sub, strided=False, whole=tq == seq,
                       has_sinks=True, has_gate=True, has_merge=False, want_stats=False)
    kernel = functools.partial(_attn_out_kernel, n_attn_in=len(attn_args), attn_kwargs=attn_kwargs)
    return pl.pallas_call(
        kernel,
        grid=(n, steps),
        in_specs=attn_specs + [row(d_model), row(B_WIDTH), row(LANES), cmaj(B_WIDTH), cmaj(LANES),
                               row(B_WIDTH), full(permt), full(expand), full(w_bf)],
        out_specs=row(d_model),
        out_shape=jax.ShapeDtypeStruct((tokens, d_model), jnp.float32),
        scratch_shapes=_attn_scratch(pairs, slabs, sub + 1, False) + [
            pltpu.VMEM((1, tq, qw), jnp.bfloat16),
            pltpu.VMEM((2, LOCAL, B_WIDTH + LANES), jnp.float32),
            pltpu.VMEM((2, LOCAL, B_WIDTH), jnp.bfloat16)],
        compiler_params=pltpu.CompilerParams(
            dimension_semantics=("parallel", "arbitrary"), vmem_limit_bytes=VMEM_LIMIT),
        name="attn_a_out",
    )(*attn_args, x2d, o1, s1, oc, sc, gate_b, permt, expand, w_bf)


def _rope_tables(seq):
    inv = ROPE_THETA ** (-jnp.arange(HALF, dtype=jnp.float32) / HALF)
    ang = jnp.arange(seq).astype(jnp.float32)[:, None] * inv[None, :]
    reps = LANES // HALF
    cos_t = jnp.tile(jnp.cos(ang), (1, reps))
    sign = jnp.tile(jnp.concatenate([-jnp.ones((HALF,), jnp.float32), jnp.ones((HALF,), jnp.float32)]),
                    LANES // HEAD_DIM)
    sin_t = jnp.tile(jnp.sin(ang), (1, reps)) * sign[None, :]
    return cos_t, sin_t


def kernel(x, norm_gain, w_in, q_norm_a, k_norm_a, sinks_a, q_norm_b, k_norm_b, w_out):
    b, seq, d_model = x.shape
    tokens = b * seq
    depth = norm_gain.shape[0]
    assert [d for _, d in B_PATTERNS] == [1, MID_DIL, CLASSES]
    assert seq % PROJ_ROWS == 0 and PROJ_ROWS % LOCAL == 0 and (seq // CLASSES) % BLOCK == 0
    assert seq % OUT_ROWS == 0 and OUT_ROWS % LOCAL == 0
    scale = HEAD_DIM ** -0.5 * LOG2E
    cos_t, sin_t = _rope_tables(seq)
    seg_i = np.arange(SEG_TILE) // HEAD_DIM
    seg = jnp.asarray(seg_i[:, None] == seg_i[None, :], jnp.bfloat16)
    r = np.arange(LOCAL)
    perm_np = (CLASSES * (r % BF16_ROWS) + r // BF16_ROWS)[:, None] == r[None, :]
    perm = jnp.asarray(perm_np, jnp.bfloat16)
    permt = jnp.asarray(perm_np.T, jnp.bfloat16)
    term_head = np.where(np.arange(LANES) < STAT_LANES, np.arange(LANES) % B_HEADS, -1)
    expand = jnp.asarray(term_head[:, None] == (np.arange(B_WIDTH) // HEAD_DIM)[None, :], jnp.bfloat16)

    for i in range(depth):
        x2d = x.reshape(tokens, d_model)
        tile = lambda g, n, s=1.0: jnp.tile(g * s, n)[None, :]
        qa, ka, va, ga, qb, kb, vb, gb, qc, kc, vc = _proj_call(
            x2d, norm_gain[i][None, :], w_in[i].astype(jnp.bfloat16), seg, perm, cos_t, sin_t,
            tile(q_norm_a[i], A_HEADS, scale), tile(k_norm_a[i], A_KV_HEADS),
            tile(q_norm_b[i], B_HEADS, scale), tile(k_norm_b[i], B_HEADS), b, seq)

        r3 = lambda t: t.reshape(b, seq, t.shape[-1])
        (w1, d1), (w4, d4), (w16, d16) = B_PATTERNS
        o1, s1 = _attn_call(r3(qb), r3(kb), r3(vb), w1 // d1, want_stats=True, name="attn_b1")
        o4, s4 = _attn_mid_call(qc, kc, vc, w4 // d4, name="attn_b4")
        fold = lambda t: t.reshape(b * CLASSES, seq // CLASSES, t.shape[-1])
        oc, sc = _attn_call(fold(qc), fold(kc), fold(vc), w16 // d16, merge=(fold(o4), fold(s4)),
                            want_stats=True, name="attn_b16")
        unfold = lambda t: t.reshape(b, CLASSES, seq // CLASSES, t.shape[-1])

        out = _attn_out_call(x2d, r3(qa), r3(ka), r3(va), A_WINDOW - 1, sinks_a[i], r3(ga),
                             o1.reshape(tokens, B_WIDTH), s1.reshape(tokens, LANES),
                             unfold(oc), unfold(sc), gb, permt, expand, w_out[i].astype(jnp.bfloat16))
        x = out.reshape(b, seq, d_model)
    return x
```

```python
import functools

import numpy as np
import jax
import jax.numpy as jnp
from jax import lax
from jax.experimental import pallas as pl
from jax.experimental.pallas import tpu as pltpu

HEAD_DIM = 64
HALF = HEAD_DIM // 2
A_HEADS = 8
A_KV_HEADS = 2
A_WINDOW = 128
B_HEADS = 8
B_PATTERNS = ((128, 1), (512, 4), (2048, 16))
BLOCK = 128
ROPE_THETA = 10000.0
EPS = 1e-6
NEG = -1e30
LOG2E = 1.4426950408889634

A_WIDTH = A_HEADS * HEAD_DIM
A_KV_WIDTH = A_KV_HEADS * HEAD_DIM
B_WIDTH = B_HEADS * HEAD_DIM

LANES = 128
BF16_ROWS = 16
SEG_TILE = 256
PROJ_ROWS = 1024
OUT_ROWS = 1024
PROJECT_BLOCKS = 2
VMEM_LIMIT = 56 * 1024 * 1024

A_KV_PAIR_WIDTH = 2 * A_KV_HEADS * HEAD_DIM
assert A_KV_WIDTH == LANES

CLASSES = max(d for _, d in B_PATTERNS)
LOCAL = CLASSES * BF16_ROWS
MID_DIL = 4
MID_RUN = BLOCK // MID_DIL
STAT_LANES = 3 * B_HEADS


def _dot(a, b):
    return jnp.dot(a, b, preferred_element_type=jnp.float32)


def _split3(v):
    hi = v.astype(jnp.bfloat16).astype(jnp.float32)
    r1 = v - hi
    mid = r1.astype(jnp.bfloat16).astype(jnp.float32)
    lo = (r1 - mid).astype(jnp.bfloat16).astype(jnp.float32)
    return hi, mid, lo


def _head_norm_rope(p, seg, gain, cos, sin, hi_mask):
    width = p.shape[-1]
    sq = (p * p).astype(jnp.bfloat16)
    step = min(width, SEG_TILE)
    sums = []
    for c in range(0, width, step):
        sums.append(_dot(sq[:, c:c + step], seg[:step, :step]))
    ss = sums[0] if len(sums) == 1 else jnp.concatenate(sums, axis=-1)
    y = p * lax.rsqrt(ss * (1.0 / HEAD_DIM) + EPS) * gain
    outs = []
    for c in range(0, width, LANES):
        yc = y[:, c:c + LANES]
        partner = jnp.where(hi_mask, pltpu.roll(yc, HALF, 1), pltpu.roll(yc, LANES - HALF, 1))
        outs.append(yc * cos + partner * sin)
    return outs[0] if len(outs) == 1 else jnp.concatenate(outs, axis=-1)


def _proj_kernel(x_ref, gain_ref, w_ref, seg_ref, perm_ref, cos_ref, sin_ref,
                 gqa_ref, gka_ref, gqb_ref, gkb_ref,
                 qa_ref, ka_ref, va_ref, ga_ref, qb_ref, kb_ref, vb_ref, gb_ref,
                 qc_ref, kc_ref, vc_ref, pbuf):
    xf = x_ref[...]
    ms = jnp.mean(xf * xf, axis=-1, keepdims=True)
    h = (xf * lax.rsqrt(ms + EPS) * gain_ref[...]).astype(jnp.bfloat16)
    seg = seg_ref[...]
    perm = perm_ref[...]
    cos = cos_ref[...]
    sin = sin_ref[...]
    lane = lax.broadcasted_iota(jnp.int32, (1, LANES), 1)
    hi_mask = (lane & HALF) != 0
    low = lane < HEAD_DIM
    bf = jnp.bfloat16

    def per_pair(y):
        swapped = pltpu.roll(y, HEAD_DIM, 1)
        return jnp.concatenate([jnp.where(low, y, swapped), jnp.where(low, swapped, y)], axis=-1)

    def normed(gain_ref_):
        return lambda p: _head_norm_rope(p, seg, gain_ref_[...], cos, sin, hi_mask)

    plain = lambda p: p
    groups = [
        [(A_WIDTH, normed(gqa_ref), qa_ref, None)],
        [(A_KV_WIDTH, lambda p: per_pair(normed(gka_ref)(p)), ka_ref, None),
         (A_KV_WIDTH, per_pair, va_ref, None)],
        [(A_WIDTH, plain, ga_ref, None)],
        [(B_WIDTH, normed(gqb_ref), qb_ref, qc_ref)],
        [(B_WIDTH, normed(gkb_ref), kb_ref, kc_ref)],
        [(B_WIDTH, plain, vb_ref, vc_ref)],
        [(B_WIDTH, plain, gb_ref, None)],
    ]
    widths = [sum(part[0] for part in g) for g in groups]
    starts = np.cumsum([0] + widths)

    def project(i):
        pbuf[i % 2, :, 0:widths[i]] = _dot(h, w_ref[:, starts[i]:starts[i] + widths[i]])

    def finish(i):
        lo = 0
        for width, epilogue, nat_ref, _ in groups[i]:
            nat_ref[...] = epilogue(pbuf[i % 2, :, lo:lo + width]).astype(nat_ref.dtype)
            lo += width

    def to_class_major(i):
        _, _, nat_ref, cm_ref = groups[i][0]
        if cm_ref is None:
            return
        for j in range(nat_ref.shape[0] // LOCAL):
            z = _dot(perm, nat_ref[j * LOCAL:(j + 1) * LOCAL, :]).astype(bf)
            for c in range(CLASSES):
                cm_ref[c, j * BF16_ROWS:(j + 1) * BF16_ROWS, :] = z[c * BF16_ROWS:(c + 1) * BF16_ROWS, :]

    n = len(groups)
    for step in range(n + 2):
        if step < n:
            project(step)
        if 0 <= step - 1 < n:
            finish(step - 1)
        if 0 <= step - 2 < n:
            to_class_major(step - 2)


def _proj_call(x2d, gain, w_bf, seg, perm, cos_t, sin_t, gqa, gka, gqb, gkb, batch, seq):
    tokens, d_model = x2d.shape
    tm = PROJ_ROWS
    steps_per_seq = seq // tm
    row = lambda w: pl.BlockSpec((tm, w), lambda g: (g, 0))
    full = lambda a: pl.BlockSpec(a.shape, lambda g: (0,) * a.ndim)
    table = pl.BlockSpec((tm, LANES), lambda g: (g % steps_per_seq, 0))
    cmaj = pl.BlockSpec((None, CLASSES, tm // CLASSES, B_WIDTH),
                        lambda g: (g // steps_per_seq, 0, g % steps_per_seq, 0))
    bf = jnp.bfloat16
    cm_shape = jax.ShapeDtypeStruct((batch, CLASSES, seq // CLASSES, B_WIDTH), bf)
    out_shape = [
        jax.ShapeDtypeStruct((tokens, A_WIDTH), bf),
        jax.ShapeDtypeStruct((tokens, A_KV_PAIR_WIDTH), bf),
        jax.ShapeDtypeStruct((tokens, A_KV_PAIR_WIDTH), bf),
        jax.ShapeDtypeStruct((tokens, A_WIDTH), jnp.float32),
        jax.ShapeDtypeStruct((tokens, B_WIDTH), bf),
        jax.ShapeDtypeStruct((tokens, B_WIDTH), bf),
        jax.ShapeDtypeStruct((tokens, B_WIDTH), bf),
        jax.ShapeDtypeStruct((tokens, B_WIDTH), jnp.float32),
        cm_shape, cm_shape, cm_shape,
    ]
    return pl.pallas_call(
        _proj_kernel,
        grid=(tokens // tm,),
        in_specs=[row(d_model), full(gain), full(w_bf), full(seg), full(perm), table, table,
                  full(gqa), full(gka), full(gqb), full(gkb)],
        out_specs=[row(s.shape[1]) for s in out_shape[:8]] + [cmaj] * 3,
        out_shape=out_shape,
        scratch_shapes=[pltpu.VMEM((2, tm, max(A_WIDTH, B_WIDTH)), jnp.float32)],
        compiler_params=pltpu.CompilerParams(
            dimension_semantics=("parallel",), vmem_limit_bytes=VMEM_LIMIT),
        name="proj",
    )(x2d, gain, w_bf, seg, perm, cos_t, sin_t, gqa, gka, gqb, gkb)


ONES_ROWS = BF16_ROWS
ATTN_BLOCKS = 16


def _attn_kernel(*refs, pairs, slabs, group, nblk, strided, whole, has_sinks, has_gate, has_merge,
                 want_stats):
    it = iter(refs)
    if strided:
        q_ref, kc_ref, vc_ref, bias_ref = (next(it) for _ in range(4))
        kp_ref = vp_ref = None
    else:
        q_ref, kc_ref, kp_ref, vc_ref, vp_ref, bias_ref = (next(it) for _ in range(6))
    sink_ref = next(it) if has_sinks else None
    gate_ref = next(it) if has_gate else None
    oin_ref, sin_ref = (next(it), next(it)) if has_merge else (None, None)
    o_ref = next(it)
    st_ref = next(it) if want_stats else None
    vtbuf, sbuf, pbuf, mbuf = (next(it) for _ in range(4))
    lsebuf = next(it) if want_stats else None

    def block_of(ref, g, m, cols):
        if strided:
            return ref[:, g, m, :, cols].reshape(BLOCK, LANES)
        return ref[g, m * BLOCK:(m + 1) * BLOCK, cols]

    def key_block(cur_ref, prev_ref, g, kb, cols):
        if kb > 0:
            return block_of(cur_ref, g, kb - 1, cols)
        return block_of(cur_ref, g, 0, cols) if strided else prev_ref[g, :, cols]

    def store_block(ref, g, m, cols, val):
        if strided:
            ref[:, g, m, :, cols] = val.reshape(BLOCK // MID_RUN, MID_RUN, LANES)
        else:
            ref[g, m * BLOCK:(m + 1) * BLOCK, cols] = val

    if strided:
        first_bias = bias_ref[1]
    else:
        first_bias = bias_ref[jnp.where(pl.program_id(1) == 0, 1, 0)]

    ones = jnp.ones((ONES_ROWS, BLOCK), vtbuf.dtype)

    def transpose_values(g, kb):
        for sl in range(slabs):
            src = slice(sl * LANES, (sl + 1) * LANES)
            vtbuf[sl, g * (nblk + 1) + kb, 0:LANES, :] = key_block(vc_ref, vp_ref, g, kb, src).T
            vtbuf[sl, g * (nblk + 1) + kb, LANES:, :] = ones

    zeros = jnp.zeros((HEAD_DIM, BLOCK), q_ref.dtype)
    onehot = jnp.concatenate([bias_ref[2, 0:BLOCK], bias_ref[2, 0:BLOCK]], axis=1)

    def sink_row(hp):
        return jnp.concatenate([jnp.full((1, LANES), sink_ref[2 * hp] * LOG2E, jnp.float32),
                                jnp.full((1, LANES), sink_ref[2 * hp + 1] * LOG2E, jnp.float32)], axis=1)

    def window(j):
        return slice(BLOCK, 2 * BLOCK) if (whole and j == 0) else slice(0, 2 * BLOCK)

    def score_stage(t):
        slot, (g, j) = t % 2, divmod(t, nblk)
        win = window(j)
        bias_t = (first_bias if j == 0 else bias_ref[0])[win]
        for hp in range(pairs):
            sl = hp * slabs // pairs
            kcols = slice(sl * LANES, (sl + 1) * LANES)
            qt = block_of(q_ref, g, j, slice(hp * LANES, (hp + 1) * LANES)).T
            qm = jnp.concatenate([jnp.concatenate([qt[0:HEAD_DIM], zeros], axis=0),
                                  jnp.concatenate([zeros, qt[HEAD_DIM:]], axis=0)], axis=1)
            kw = key_block(kc_ref, kp_ref, g, j + 1, kcols)
            if win.start == 0:
                kw = jnp.concatenate([key_block(kc_ref, kp_ref, g, j, kcols), kw], axis=0)
            sbuf[slot, hp, win] = _dot(jnp.concatenate([kw, bias_t], axis=1),
                                       jnp.concatenate([qm, onehot], axis=0))

    def softmax_stage(t):
        slot, (g, j) = t % 2, divmod(t, nblk)
        win = window(j)
        if j == 0 and win.start == 0:
            transpose_values(g, 0)
        transpose_values(g, j + 1)
        for hp in range(pairs):
            m = jnp.max(sbuf[slot, hp, win], axis=0, keepdims=True)
            if has_sinks:
                m = jnp.maximum(m, sink_row(hp))
            pbuf[slot, hp, win] = jnp.exp2(sbuf[slot, hp, win] - m).astype(pbuf.dtype)
            mbuf[slot, hp] = m

    def value_stage(t):
        slot, (g, j) = t % 2, divmod(t, nblk)
        win = window(j)
        kb0 = g * (nblk + 1) + j
        if has_merge:
            terms = block_of(sin_ref, g, j, slice(0, LANES)).T.astype(jnp.float32)
            lse_in = (terms[0:B_HEADS] + terms[B_HEADS:2 * B_HEADS] + terms[2 * B_HEADS:3 * B_HEADS])
        for hp in range(pairs):
            sl = hp * slabs // pairs
            cols = slice(hp * LANES, (hp + 1) * LANES)
            m = mbuf[slot, hp]
            vt = vtbuf[sl, kb0 + 1]
            if win.start == 0:
                vt = jnp.concatenate([vtbuf[sl, kb0], vt], axis=1)
            r = _dot(vt, pbuf[slot, hp, win])
            l = r[LANES:LANES + 1, :]
            if has_sinks:
                l = l + jnp.exp2(sink_row(hp) - m)
            inv = 1.0 / l
            lse = m + jnp.log2(l)
            if has_merge:
                other = jnp.concatenate([lse_in[2 * hp:2 * hp + 1], lse_in[2 * hp + 1:2 * hp + 2]], axis=1)
                top = jnp.maximum(lse, other)
                mine, theirs = jnp.exp2(lse - top), jnp.exp2(other - top)
                total = mine + theirs
                inv, frac = inv * (mine / total), theirs / total
                lse = top + jnp.log2(total)
            ot = jnp.concatenate([r[0:HEAD_DIM, 0:LANES] * inv[:, 0:LANES],
                                  r[HEAD_DIM:LANES, LANES:] * inv[:, LANES:]], axis=0)
            if has_merge:
                oin_t = block_of(oin_ref, g, j, cols).T.astype(jnp.float32)
                ot = ot + oin_t * jnp.concatenate(
                    [jnp.broadcast_to(frac[:, 0:LANES], (HEAD_DIM, LANES)),
                     jnp.broadcast_to(frac[:, LANES:], (HEAD_DIM, LANES))], axis=0)
            if has_gate:
                gt = block_of(gate_ref, g, j, cols)
                o = (ot.T * (gt * jax.nn.sigmoid(gt))).astype(o_ref.dtype)
            else:
                o = ot.astype(o_ref.dtype).T
            store_block(o_ref, g, j, cols, o)
            if want_stats:
                lsebuf[slot, 2 * hp:2 * hp + 1, :] = lse[:, 0:LANES]
                lsebuf[slot, 2 * hp + 1:2 * hp + 2, :] = lse[:, LANES:]
        if want_stats:
            terms = _split3(lsebuf[slot])
            pad = jnp.zeros((LANES - STAT_LANES, BLOCK), jnp.float32)
            tile = jnp.concatenate(list(terms) + [pad], axis=0)
            store_block(st_ref, g, j, slice(0, LANES), tile.astype(st_ref.dtype).T)

    jobs = group * nblk
    for step in range(jobs + 2):
        if step - 2 >= 0:
            value_stage(step - 2)
        if 0 <= step - 1 < jobs:
            softmax_stage(step - 1)
        if step < jobs:
            score_stage(step)


def _band_bias(max_dist, mid):
    r = np.arange(BLOCK)
    pos = MID_DIL * (r % MID_RUN) + r // MID_RUN if mid else r
    qpos = pos[None, :] + BLOCK
    kpos = np.concatenate([pos, pos + BLOCK])[:, None]
    dist = qpos - kpos
    valid = (dist >= 0) & (dist <= max_dist)
    first = valid & (np.arange(2 * BLOCK)[:, None] >= BLOCK)
    masks = np.where(np.stack([valid, first]), 0.0, NEG)
    onehot = np.tile(np.eye(BLOCK), (2, 1))
    return jnp.asarray(np.concatenate([masks, onehot[None]]), jnp.bfloat16)


def _attn_scratch(pairs, slabs, key_blocks, want_stats):
    scratch = [pltpu.VMEM((slabs, key_blocks, LANES + ONES_ROWS, BLOCK), jnp.bfloat16),
               pltpu.VMEM((2, pairs, 2 * BLOCK, 2 * LANES), jnp.float32),
               pltpu.VMEM((2, pairs, 2 * BLOCK, 2 * LANES), jnp.bfloat16),
               pltpu.VMEM((2, pairs, 1, 2 * LANES), jnp.float32)]
    if want_stats:
        scratch.append(pltpu.VMEM((2, 2 * pairs, BLOCK), jnp.float32))
    return scratch


def _attn_call(q, k, v, max_dist, *, sinks=None, gate=None, merge=None, want_stats=False, name):
    n, seq, qw = q.shape
    kvw = k.shape[-1]
    pairs, slabs = qw // LANES, kvw // LANES
    tq = min(seq, ATTN_BLOCKS * BLOCK)
    sub = tq // BLOCK
    group = ATTN_BLOCKS // sub
    cur = lambda w: pl.BlockSpec((group, tq, w), lambda b, i: (b, i, 0))
    prev = lambda w: pl.BlockSpec((group, BLOCK, w), lambda b, i: (b, jnp.maximum(i * sub - 1, 0), 0))
    bias = _band_bias(max_dist, mid=False)
    args = [q, k, k, v, v, bias]
    in_specs = [cur(qw), cur(kvw), prev(kvw), cur(kvw), prev(kvw),
                pl.BlockSpec(bias.shape, lambda b, i: (0, 0, 0))]
    if sinks is not None:
        args.append(sinks)
        in_specs.append(pl.BlockSpec(memory_space=pltpu.SMEM))
    if gate is not None:
        args.append(gate)
        in_specs.append(cur(qw))
    if merge is not None:
        args.extend(merge)
        in_specs.extend([cur(qw), cur(LANES)])
    out_shape = [jax.ShapeDtypeStruct((n, seq, qw), jnp.bfloat16)]
    out_specs = [cur(qw)]
    if want_stats:
        out_shape.append(jax.ShapeDtypeStruct((n, seq, LANES), jnp.bfloat16))
        out_specs.append(cur(LANES))
    kernel = functools.partial(_attn_kernel, pairs=pairs, slabs=slabs, group=group, nblk=sub,
                               strided=False, whole=tq == seq, has_sinks=sinks is not None,
                               has_gate=gate is not None, has_merge=merge is not None,
                               want_stats=want_stats)
    return pl.pallas_call(
        kernel,
        grid=(n // group, seq // tq),
        in_specs=in_specs,
        out_specs=out_specs,
        out_shape=out_shape,
        scratch_shapes=_attn_scratch(pairs, slabs, group * (sub + 1), want_stats),
        compiler_params=pltpu.CompilerParams(
            dimension_semantics=("parallel", "arbitrary"), vmem_limit_bytes=VMEM_LIMIT),
        name=name,
    )(*args)


def _attn_mid_call(q, k, v, max_dist, *, name):
    b, classes, per_class, w = q.shape
    outer = classes // MID_DIL
    nblk = per_class // MID_RUN
    group = max(1, min(MID_DIL, ATTN_BLOCKS // nblk))
    view = lambda t: t.reshape(b, outer, MID_DIL, nblk, MID_RUN, t.shape[-1])
    spec = lambda width: pl.BlockSpec((None, outer, group, nblk, MID_RUN, width),
                                      lambda i, e: (i, 0, e, 0, 0, 0))
    pairs = w // LANES
    bias = _band_bias(max_dist, mid=True)
    shape6 = (b, outer, MID_DIL, nblk, MID_RUN)
    kernel = functools.partial(_attn_kernel, pairs=pairs, slabs=pairs, group=group, nblk=nblk,
                               strided=True, whole=True, has_sinks=False, has_gate=False,
                               has_merge=False, want_stats=True)
    o, st = pl.pallas_call(
        kernel,
        grid=(b, MID_DIL // group),
        in_specs=[spec(w), spec(w), spec(w), pl.BlockSpec(bias.shape, lambda i, e: (0, 0, 0))],
        out_specs=[spec(w), spec(LANES)],
        out_shape=[jax.ShapeDtypeStruct(shape6 + (w,), jnp.bfloat16),
                   jax.ShapeDtypeStruct(shape6 + (LANES,), jnp.bfloat16)],
        scratch_shapes=_attn_scratch(pairs, pairs, group * (nblk + 1), True),
        compiler_params=pltpu.CompilerParams(
            dimension_semantics=("parallel", "arbitrary"), vmem_limit_bytes=VMEM_LIMIT),
        name=name,
    )(view(q), view(k), view(v), bias)
    return o.reshape(b, classes, per_class, w), st.reshape(b, classes, per_class, LANES)


def _expand_heads(w, expand):
    hi, mid, lo = _split3(w)
    packed = hi + pltpu.roll(mid, B_HEADS, 1) + pltpu.roll(lo, 2 * B_HEADS, 1)
    return _dot(packed.astype(jnp.bfloat16), expand)


def _stat_sum(st):
    return (st + pltpu.roll(st, LANES - B_HEADS, 1) + pltpu.roll(st, LANES - 2 * B_HEADS, 1))


def _out_kernel(x_ref, ma_ref, o1_ref, s1_ref, oc_ref, sc_ref, gb_ref,
                permt_ref, expand_ref, w_ref, out_ref, ubuf, mbuf):
    permt = permt_ref[...]
    expand = expand_ref[...]
    head_lane = lax.broadcasted_iota(jnp.int32, (1, LANES), 1) < B_HEADS
    nloc = x_ref.shape[0] // LOCAL

    def to_token_order(j):
        sel = slice(j * BF16_ROWS, (j + 1) * BF16_ROWS)
        y = jnp.concatenate([jnp.concatenate([oc_ref[c, sel, :] for c in range(CLASSES)], axis=0),
                             jnp.concatenate([sc_ref[c, sel, :] for c in range(CLASSES)], axis=0)],
                            axis=1)
        ubuf[j % 2] = _dot(permt, y)

    def merge_gate(j):
        rows = slice(j * LOCAL, (j + 1) * LOCAL)
        o1 = o1_ref[rows, :].astype(jnp.float32)
        l1 = _stat_sum(s1_ref[rows, :].astype(jnp.float32))
        oc = ubuf[j % 2, :, 0:B_WIDTH]
        lc = _stat_sum(ubuf[j % 2, :, B_WIDTH:])
        top = jnp.maximum(l1, lc)
        e1, ec = jnp.exp2(l1 - top), jnp.exp2(lc - top)
        w1 = jnp.where(head_lane, e1 / (e1 + ec), 0.0)
        ob = oc + _expand_heads(w1, expand) * (o1 - oc)
        gb = gb_ref[rows, :]
        mbuf[rows, :] = (ob * (gb * jax.nn.sigmoid(gb))).astype(mbuf.dtype)

    def project(p):
        rows = slice(p * PROJECT_BLOCKS * LOCAL, (p + 1) * PROJECT_BLOCKS * LOCAL)
        out_ref[rows, :] = (x_ref[rows, :] + _dot(ma_ref[rows, :], w_ref[0:A_WIDTH, :])
                            + _dot(mbuf[rows, :], w_ref[A_WIDTH:, :]))

    for step in range(nloc + 2):
        if step - 1 > 0 and (step - 1) % PROJECT_BLOCKS == 0:
            project((step - 1) // PROJECT_BLOCKS - 1)
        if 0 <= step - 1 < nloc:
            merge_gate(step - 1)
        if step < nloc:
            to_token_order(step)


def _attn_out_kernel(*refs, n_attn_in, attn_kwargs):
    attn_in = refs[:n_attn_in]
    x_ref, o1_ref, s1_ref, oc_ref, sc_ref, gb_ref, permt_ref, expand_ref, w_ref, out_ref = refs[n_attn_in:n_attn_in + 10]
    scratch = refs[n_attn_in + 10:]
    attn_scratch, (ma_buf, ubuf, mbuf) = scratch[:-3], scratch[-3:]
    _attn_kernel(*attn_in, ma_buf, *attn_scratch, **attn_kwargs)
    _out_kernel(x_ref, ma_buf.at[0], o1_ref, s1_ref, oc_ref, sc_ref, gb_ref,
                permt_ref, expand_ref, w_ref, out_ref, ubuf, mbuf)


def _attn_out_call(x2d, q, k, v, max_dist, sinks, gate, o1, s1, oc, sc, gate_b, permt, expand, w_bf):
    tokens, d_model = x2d.shape
    n, seq, qw = q.shape
    kvw = k.shape[-1]
    pairs, slabs = qw // LANES, kvw // LANES
    tq = OUT_ROWS
    sub = tq // BLOCK
    steps = seq // tq
    cur = lambda w: pl.BlockSpec((1, tq, w), lambda b, i: (b, i, 0))
    prev = lambda w: pl.BlockSpec((1, BLOCK, w), lambda b, i: (b, jnp.maximum(i * sub - 1, 0), 0))
    row = lambda w: pl.BlockSpec((tq, w), lambda b, i: (b * steps + i, 0))
    full = lambda a: pl.BlockSpec(a.shape, lambda b, i: (0,) * a.ndim)
    cmaj = lambda w: pl.BlockSpec((None, CLASSES, tq // CLASSES, w), lambda b, i: (b, 0, i, 0))
    bias = _band_bias(max_dist, mid=False)
    attn_args = [q, k, k, v, v, bias, sinks, gate]
    attn_specs = [cur(qw), cur(kvw), prev(kvw), cur(kvw), prev(kvw), full(bias),
                  pl.BlockSpec(memory_space=pltpu.SMEM), cur(qw)]
    attn_kwargs = dict(pairs=pairs, slabs=slabs, group=1, nblk=sub, strided=False, whole=tq == seq,
                       has_sinks=True, has_gate=True, has_merge=False, want_stats=False)
    kernel = functools.partial(_attn_out_kernel, n_attn_in=len(attn_args), attn_kwargs=attn_kwargs)
    return pl.pallas_call(
        kernel,
        grid=(n, steps),
        in_specs=attn_specs + [row(d_model), row(B_WIDTH), row(LANES), cmaj(B_WIDTH), cmaj(LANES),
                               row(B_WIDTH), full(permt), full(expand), full(w_bf)],
        out_specs=row(d_model),
        out_shape=jax.ShapeDtypeStruct((tokens, d_model), jnp.float32),
        scratch_shapes=_attn_scratch(pairs, slabs, sub + 1, False) + [
            pltpu.VMEM((1, tq, qw), jnp.bfloat16),
            pltpu.VMEM((2, LOCAL, B_WIDTH + LANES), jnp.float32),
            pltpu.VMEM((tq, B_WIDTH), jnp.bfloat16)],
        compiler_params=pltpu.CompilerParams(
            dimension_semantics=("parallel", "arbitrary"), vmem_limit_bytes=VMEM_LIMIT),
        name="attn_a_out",
    )(*attn_args, x2d, o1, s1, oc, sc, gate_b, permt, expand, w_bf)


def _rope_tables(seq):
    inv = ROPE_THETA ** (-jnp.arange(HALF, dtype=jnp.float32) / HALF)
    ang = jnp.arange(seq).astype(jnp.float32)[:, None] * inv[None, :]
    reps = LANES // HALF
    cos_t = jnp.tile(jnp.cos(ang), (1, reps))
    sign = jnp.tile(jnp.concatenate([-jnp.ones((HALF,), jnp.float32), jnp.ones((HALF,), jnp.float32)]),
                    LANES // HEAD_DIM)
    sin_t = jnp.tile(jnp.sin(ang), (1, reps)) * sign[None, :]
    return cos_t, sin_t


def kernel(x, norm_gain, w_in, q_norm_a, k_norm_a, sinks_a, q_norm_b, k_norm_b, w_out):
    b, seq, d_model = x.shape
    tokens = b * seq
    depth = norm_gain.shape[0]
    assert [d for _, d in B_PATTERNS] == [1, MID_DIL, CLASSES]
    assert seq % PROJ_ROWS == 0 and PROJ_ROWS % LOCAL == 0 and (seq // CLASSES) % BLOCK == 0
    assert seq % OUT_ROWS == 0 and OUT_ROWS % (PROJECT_BLOCKS * LOCAL) == 0
    scale = HEAD_DIM ** -0.5 * LOG2E
    cos_t, sin_t = _rope_tables(seq)
    seg_i = np.arange(SEG_TILE) // HEAD_DIM
    seg = jnp.asarray(seg_i[:, None] == seg_i[None, :], jnp.bfloat16)
    r = np.arange(LOCAL)
    perm_np = (CLASSES * (r % BF16_ROWS) + r // BF16_ROWS)[:, None] == r[None, :]
    perm = jnp.asarray(perm_np, jnp.bfloat16)
    permt = jnp.asarray(perm_np.T, jnp.bfloat16)
    term_head = np.where(np.arange(LANES) < STAT_LANES, np.arange(LANES) % B_HEADS, -1)
    expand = jnp.asarray(term_head[:, None] == (np.arange(B_WIDTH) // HEAD_DIM)[None, :], jnp.bfloat16)

    for i in range(depth):
        x2d = x.reshape(tokens, d_model)
        tile = lambda g, n, s=1.0: jnp.tile(g * s, n)[None, :]
        qa, ka, va, ga, qb, kb, vb, gb, qc, kc, vc = _proj_call(
            x2d, norm_gain[i][None, :], w_in[i].astype(jnp.bfloat16), seg, perm, cos_t, sin_t,
            tile(q_norm_a[i], A_HEADS, scale), tile(k_norm_a[i], A_KV_HEADS),
            tile(q_norm_b[i], B_HEADS, scale), tile(k_norm_b[i], B_HEADS), b, seq)

        r3 = lambda t: t.reshape(b, seq, t.shape[-1])
        (w1, d1), (w4, d4), (w16, d16) = B_PATTERNS
        o1, s1 = _attn_call(r3(qb), r3(kb), r3(vb), w1 // d1, want_stats=True, name="attn_b1")
        o4, s4 = _attn_mid_call(qc, kc, vc, w4 // d4, name="attn_b4")
        fold = lambda t: t.reshape(b * CLASSES, seq // CLASSES, t.shape[-1])
        oc, sc = _attn_call(fold(qc), fold(kc), fold(vc), w16 // d16, merge=(fold(o4), fold(s4)),
                            want_stats=True, name="attn_b16")
        unfold = lambda t: t.reshape(b, CLASSES, seq // CLASSES, t.shape[-1])

        out = _attn_out_call(x2d, r3(qa), r3(ka), r3(va), A_WINDOW - 1, sinks_a[i], r3(ga),
                             o1.reshape(tokens, B_WIDTH), s1.reshape(tokens, LANES),
                             unfold(oc), unfold(sc), gb, permt, expand, w_out[i].astype(jnp.bfloat16))
        x = out.reshape(b, seq, d_model)
    return x
```

```python
import functools

import numpy as np
import jax
import jax.numpy as jnp
from jax import lax
from jax.experimental import pallas as pl
from jax.experimental.pallas import tpu as pltpu

HEAD_DIM = 64
HALF = HEAD_DIM // 2
A_HEADS = 8
A_KV_HEADS = 2
A_WINDOW = 128
B_HEADS = 8
B_PATTERNS = ((128, 1), (512, 4), (2048, 16))
BLOCK = 128
ROPE_THETA = 10000.0
EPS = 1e-6
NEG = -1e30
LOG2E = 1.4426950408889634

A_WIDTH = A_HEADS * HEAD_DIM
A_KV_WIDTH = A_KV_HEADS * HEAD_DIM
B_WIDTH = B_HEADS * HEAD_DIM

LANES = 128
BF16_ROWS = 16
SEG_TILE = 256
PROJ_ROWS = 1024
OUT_ROWS = 1024
VMEM_BYTES = 64 * 1024 * 1024
VMEM_LIMIT = VMEM_BYTES * 7 // 8

A_KV_PAIR_WIDTH = 2 * A_KV_HEADS * HEAD_DIM
assert A_KV_WIDTH == LANES

CLASSES = max(d for _, d in B_PATTERNS)
LOCAL = CLASSES * BF16_ROWS
MID_DIL = 4
MID_RUN = BLOCK // MID_DIL
STAT_LANES = 3 * B_HEADS


def _dot(a, b):
    return jnp.dot(a, b, preferred_element_type=jnp.float32)


def _split3(v):
    hi = v.astype(jnp.bfloat16).astype(jnp.float32)
    r1 = v - hi
    mid = r1.astype(jnp.bfloat16).astype(jnp.float32)
    lo = (r1 - mid).astype(jnp.bfloat16).astype(jnp.float32)
    return hi, mid, lo


def _head_norm_rope(p, seg, gain, cos, sin, hi_mask):
    width = p.shape[-1]
    sq = (p * p).astype(jnp.bfloat16)
    step = min(width, SEG_TILE)
    sums = []
    for c in range(0, width, step):
        sums.append(_dot(sq[:, c:c + step], seg[:step, :step]))
    ss = sums[0] if len(sums) == 1 else jnp.concatenate(sums, axis=-1)
    y = p * lax.rsqrt(ss * (1.0 / HEAD_DIM) + EPS) * gain
    outs = []
    for c in range(0, width, LANES):
        yc = y[:, c:c + LANES]
        partner = jnp.where(hi_mask, pltpu.roll(yc, HALF, 1), pltpu.roll(yc, LANES - HALF, 1))
        outs.append(yc * cos + partner * sin)
    return outs[0] if len(outs) == 1 else jnp.concatenate(outs, axis=-1)


def _proj_kernel(x_ref, gain_ref, w_ref, seg_ref, perm_ref, cos_ref, sin_ref,
                 gqa_ref, gka_ref, gqb_ref, gkb_ref,
                 qa_ref, ka_ref, va_ref, ga_ref, qb_ref, kb_ref, vb_ref, gb_ref,
                 qc_ref, kc_ref, vc_ref, pbuf):
    xf = x_ref[...]
    ms = jnp.mean(xf * xf, axis=-1, keepdims=True)
    h = (xf * lax.rsqrt(ms + EPS) * gain_ref[...]).astype(jnp.bfloat16)
    seg = seg_ref[...]
    perm = perm_ref[...]
    cos = cos_ref[...]
    sin = sin_ref[...]
    lane = lax.broadcasted_iota(jnp.int32, (1, LANES), 1)
    hi_mask = (lane & HALF) != 0
    low = lane < HEAD_DIM
    bf = jnp.bfloat16

    def per_pair(y):
        swapped = pltpu.roll(y, HEAD_DIM, 1)
        return jnp.concatenate([jnp.where(low, y, swapped), jnp.where(low, swapped, y)], axis=-1)

    def normed(gain_ref_):
        return lambda p: _head_norm_rope(p, seg, gain_ref_[...], cos, sin, hi_mask)

    plain = lambda p: p
    groups = [
        [(A_WIDTH, normed(gqa_ref), qa_ref, None)],
        [(A_KV_WIDTH, lambda p: per_pair(normed(gka_ref)(p)), ka_ref, None),
         (A_KV_WIDTH, per_pair, va_ref, None)],
        [(A_WIDTH, plain, ga_ref, None)],
        [(B_WIDTH, normed(gqb_ref), qb_ref, qc_ref)],
        [(B_WIDTH, normed(gkb_ref), kb_ref, kc_ref)],
        [(B_WIDTH, plain, vb_ref, vc_ref)],
        [(B_WIDTH, plain, gb_ref, None)],
    ]
    widths = [sum(part[0] for part in g) for g in groups]
    starts = np.cumsum([0] + widths)

    def project(i):
        pbuf[i % 2, :, 0:widths[i]] = _dot(h, w_ref[:, starts[i]:starts[i] + widths[i]])

    def finish(i):
        lo = 0
        for width, epilogue, nat_ref, _ in groups[i]:
            nat_ref[...] = epilogue(pbuf[i % 2, :, lo:lo + width]).astype(nat_ref.dtype)
            lo += width

    def to_class_major(i):
        _, _, nat_ref, cm_ref = groups[i][0]
        if cm_ref is None:
            return
        for j in range(nat_ref.shape[0] // LOCAL):
            z = _dot(perm, nat_ref[j * LOCAL:(j + 1) * LOCAL, :]).astype(bf)
            for c in range(CLASSES):
                cm_ref[c, j * BF16_ROWS:(j + 1) * BF16_ROWS, :] = z[c * BF16_ROWS:(c + 1) * BF16_ROWS, :]

    n = len(groups)
    for step in range(n + 2):
        if step < n:
            project(step)
        if 0 <= step - 1 < n:
            finish(step - 1)
        if 0 <= step - 2 < n:
            to_class_major(step - 2)


def _proj_call(x2d, gain, w_bf, seg, perm, cos_t, sin_t, gqa, gka, gqb, gkb, batch, seq):
    tokens, d_model = x2d.shape
    tm = PROJ_ROWS
    steps_per_seq = seq // tm
    row = lambda w: pl.BlockSpec((tm, w), lambda g: (g, 0))
    full = lambda a: pl.BlockSpec(a.shape, lambda g: (0,) * a.ndim)
    table = pl.BlockSpec((tm, LANES), lambda g: (g % steps_per_seq, 0))
    cmaj = pl.BlockSpec((None, CLASSES, tm // CLASSES, B_WIDTH),
                        lambda g: (g // steps_per_seq, 0, g % steps_per_seq, 0))
    bf = jnp.bfloat16
    cm_shape = jax.ShapeDtypeStruct((batch, CLASSES, seq // CLASSES, B_WIDTH), bf)
    out_shape = [
        jax.ShapeDtypeStruct((tokens, A_WIDTH), bf),
        jax.ShapeDtypeStruct((tokens, A_KV_PAIR_WIDTH), bf),
        jax.ShapeDtypeStruct((tokens, A_KV_PAIR_WIDTH), bf),
        jax.ShapeDtypeStruct((tokens, A_WIDTH), jnp.float32),
        jax.ShapeDtypeStruct((tokens, B_WIDTH), bf),
        jax.ShapeDtypeStruct((tokens, B_WIDTH), bf),
        jax.ShapeDtypeStruct((tokens, B_WIDTH), bf),
        jax.ShapeDtypeStruct((tokens, B_WIDTH), jnp.float32),
        cm_shape, cm_shape, cm_shape,
    ]
    return pl.pallas_call(
        _proj_kernel,
        grid=(tokens // tm,),
        in_specs=[row(d_model), full(gain), full(w_bf), full(seg), full(perm), table, table,
                  full(gqa), full(gka), full(gqb), full(gkb)],
        out_specs=[row(s.shape[1]) for s in out_shape[:8]] + [cmaj] * 3,
        out_shape=out_shape,
        scratch_shapes=[pltpu.VMEM((2, tm, max(A_WIDTH, B_WIDTH)), jnp.float32)],
        compiler_params=pltpu.CompilerParams(
            dimension_semantics=("parallel",), vmem_limit_bytes=VMEM_LIMIT),
        name="proj",
    )(x2d, gain, w_bf, seg, perm, cos_t, sin_t, gqa, gka, gqb, gkb)


ONES_ROWS = BF16_ROWS
ATTN_BLOCKS = 16


def _attn_kernel(*refs, pairs, slabs, group, nblk, strided, whole, has_sinks, has_gate, has_merge,
                 want_stats):
    it = iter(refs)
    if strided:
        q_ref, kc_ref, vc_ref, bias_ref = (next(it) for _ in range(4))
        kp_ref = vp_ref = None
    else:
        q_ref, kc_ref, kp_ref, vc_ref, vp_ref, bias_ref = (next(it) for _ in range(6))
    sink_ref = next(it) if has_sinks else None
    gate_ref = next(it) if has_gate else None
    oin_ref, sin_ref = (next(it), next(it)) if has_merge else (None, None)
    o_ref = next(it)
    st_ref = next(it) if want_stats else None
    vtbuf, sbuf, pbuf, mbuf = (next(it) for _ in range(4))
    lsebuf = next(it) if want_stats else None

    def block_of(ref, g, m, cols):
        if strided:
            return ref[:, g, m, :, cols].reshape(BLOCK, LANES)
        return ref[g, m * BLOCK:(m + 1) * BLOCK, cols]

    def key_block(cur_ref, prev_ref, g, kb, cols):
        if kb > 0:
            return block_of(cur_ref, g, kb - 1, cols)
        return block_of(cur_ref, g, 0, cols) if strided else prev_ref[g, :, cols]

    def store_block(ref, g, m, cols, val):
        if strided:
            ref[:, g, m, :, cols] = val.reshape(BLOCK // MID_RUN, MID_RUN, LANES)
        else:
            ref[g, m * BLOCK:(m + 1) * BLOCK, cols] = val

    if strided:
        first_bias = bias_ref[1]
    else:
        first_bias = bias_ref[jnp.where(pl.program_id(1) == 0, 1, 0)]

    ones = jnp.ones((ONES_ROWS, BLOCK), vtbuf.dtype)

    def transpose_values(g, kb):
        for sl in range(slabs):
            src = slice(sl * LANES, (sl + 1) * LANES)
            vtbuf[sl, g * (nblk + 1) + kb, 0:LANES, :] = key_block(vc_ref, vp_ref, g, kb, src).T
            vtbuf[sl, g * (nblk + 1) + kb, LANES:, :] = ones

    zeros = jnp.zeros((HEAD_DIM, BLOCK), q_ref.dtype)
    onehot = jnp.concatenate([bias_ref[2, 0:BLOCK], bias_ref[2, 0:BLOCK]], axis=1)

    def sink_row(hp):
        return jnp.concatenate([jnp.full((1, LANES), sink_ref[2 * hp] * LOG2E, jnp.float32),
                                jnp.full((1, LANES), sink_ref[2 * hp + 1] * LOG2E, jnp.float32)], axis=1)

    def window(j):
        return slice(BLOCK, 2 * BLOCK) if (whole and j == 0) else slice(0, 2 * BLOCK)

    def score_stage(t):
        slot, (g, j) = t % 2, divmod(t, nblk)
        win = window(j)
        bias_t = (first_bias if j == 0 else bias_ref[0])[win]
        for hp in range(pairs):
            sl = hp * slabs // pairs
            kcols = slice(sl * LANES, (sl + 1) * LANES)
            qt = block_of(q_ref, g, j, slice(hp * LANES, (hp + 1) * LANES)).T
            qm = jnp.concatenate([jnp.concatenate([qt[0:HEAD_DIM], zeros], axis=0),
                                  jnp.concatenate([zeros, qt[HEAD_DIM:]], axis=0)], axis=1)
            kw = key_block(kc_ref, kp_ref, g, j + 1, kcols)
            if win.start == 0:
                kw = jnp.concatenate([key_block(kc_ref, kp_ref, g, j, kcols), kw], axis=0)
            sbuf[slot, hp, win] = _dot(jnp.concatenate([kw, bias_t], axis=1),
                                       jnp.concatenate([qm, onehot], axis=0))

    def softmax_stage(t):
        slot, (g, j) = t % 2, divmod(t, nblk)
        win = window(j)
        if j == 0 and win.start == 0:
            transpose_values(g, 0)
        transpose_values(g, j + 1)
        for hp in range(pairs):
            m = jnp.max(sbuf[slot, hp, win], axis=0, keepdims=True)
            if has_sinks:
                m = jnp.maximum(m, sink_row(hp))
            pbuf[slot, hp, win] = jnp.exp2(sbuf[slot, hp, win] - m).astype(pbuf.dtype)
            mbuf[slot, hp] = m

    def value_stage(t):
        slot, (g, j) = t % 2, divmod(t, nblk)
        win = window(j)
        kb0 = g * (nblk + 1) + j
        if has_merge:
            terms = block_of(sin_ref, g, j, slice(0, LANES)).T.astype(jnp.float32)
            lse_in = (terms[0:B_HEADS] + terms[B_HEADS:2 * B_HEADS] + terms[2 * B_HEADS:3 * B_HEADS])
        for hp in range(pairs):
            sl = hp * slabs // pairs
            cols = slice(hp * LANES, (hp + 1) * LANES)
            m = mbuf[slot, hp]
            vt = vtbuf[sl, kb0 + 1]
            if win.start == 0:
                vt = jnp.concatenate([vtbuf[sl, kb0], vt], axis=1)
            r = _dot(vt, pbuf[slot, hp, win])
            l = r[LANES:LANES + 1, :]
            if has_sinks:
                l = l + jnp.exp2(sink_row(hp) - m)
            inv = 1.0 / l
            lse = m + jnp.log2(l)
            if has_merge:
                other = jnp.concatenate([lse_in[2 * hp:2 * hp + 1], lse_in[2 * hp + 1:2 * hp + 2]], axis=1)
                top = jnp.maximum(lse, other)
                mine, theirs = jnp.exp2(lse - top), jnp.exp2(other - top)
                total = mine + theirs
                inv, frac = inv * (mine / total), theirs / total
                lse = top + jnp.log2(total)
            ot = jnp.concatenate([r[0:HEAD_DIM, 0:LANES] * inv[:, 0:LANES],
                                  r[HEAD_DIM:LANES, LANES:] * inv[:, LANES:]], axis=0)
            if has_merge:
                oin_t = block_of(oin_ref, g, j, cols).T.astype(jnp.float32)
                ot = ot + oin_t * jnp.concatenate(
                    [jnp.broadcast_to(frac[:, 0:LANES], (HEAD_DIM, LANES)),
                     jnp.broadcast_to(frac[:, LANES:], (HEAD_DIM, LANES))], axis=0)
            if has_gate:
                gt = block_of(gate_ref, g, j, cols)
                o = (ot.T * (gt * jax.nn.sigmoid(gt))).astype(o_ref.dtype)
            else:
                o = ot.astype(o_ref.dtype).T
            store_block(o_ref, g, j, cols, o)
            if want_stats:
                lsebuf[slot, 2 * hp:2 * hp + 1, :] = lse[:, 0:LANES]
                lsebuf[slot, 2 * hp + 1:2 * hp + 2, :] = lse[:, LANES:]
        if want_stats:
            terms = _split3(lsebuf[slot])
            pad = jnp.zeros((LANES - STAT_LANES, BLOCK), jnp.float32)
            tile = jnp.concatenate(list(terms) + [pad], axis=0)
            store_block(st_ref, g, j, slice(0, LANES), tile.astype(st_ref.dtype).T)

    jobs = group * nblk
    for step in range(jobs + 2):
        if step - 2 >= 0:
            value_stage(step - 2)
        if 0 <= step - 1 < jobs:
            softmax_stage(step - 1)
        if step < jobs:
            score_stage(step)


def _band_bias(max_dist, mid):
    r = np.arange(BLOCK)
    pos = MID_DIL * (r % MID_RUN) + r // MID_RUN if mid else r
    qpos = pos[None, :] + BLOCK
    kpos = np.concatenate([pos, pos + BLOCK])[:, None]
    dist = qpos - kpos
    valid = (dist >= 0) & (dist <= max_dist)
    first = valid & (np.arange(2 * BLOCK)[:, None] >= BLOCK)
    masks = np.where(np.stack([valid, first]), 0.0, NEG)
    onehot = np.tile(np.eye(BLOCK), (2, 1))
    return jnp.asarray(np.concatenate([masks, onehot[None]]), jnp.bfloat16)


def _attn_scratch(pairs, slabs, key_blocks, want_stats):
    scratch = [pltpu.VMEM((slabs, key_blocks, LANES + ONES_ROWS, BLOCK), jnp.bfloat16),
               pltpu.VMEM((2, pairs, 2 * BLOCK, 2 * LANES), jnp.float32),
               pltpu.VMEM((2, pairs, 2 * BLOCK, 2 * LANES), jnp.bfloat16),
               pltpu.VMEM((2, pairs, 1, 2 * LANES), jnp.float32)]
    if want_stats:
        scratch.append(pltpu.VMEM((2, 2 * pairs, BLOCK), jnp.float32))
    return scratch


def _attn_call(q, k, v, max_dist, *, sinks=None, gate=None, merge=None, want_stats=False,
               blocks=ATTN_BLOCKS, name):
    n, seq, qw = q.shape
    kvw = k.shape[-1]
    pairs, slabs = qw // LANES, kvw // LANES
    tq = min(seq, blocks * BLOCK)
    sub = tq // BLOCK
    group = blocks // sub
    cur = lambda w: pl.BlockSpec((group, tq, w), lambda b, i: (b, i, 0))
    prev = lambda w: pl.BlockSpec((group, BLOCK, w), lambda b, i: (b, jnp.maximum(i * sub - 1, 0), 0))
    bias = _band_bias(max_dist, mid=False)
    args = [q, k, k, v, v, bias]
    in_specs = [cur(qw), cur(kvw), prev(kvw), cur(kvw), prev(kvw),
                pl.BlockSpec(bias.shape, lambda b, i: (0, 0, 0))]
    if sinks is not None:
        args.append(sinks)
        in_specs.append(pl.BlockSpec(memory_space=pltpu.SMEM))
    if gate is not None:
        args.append(gate)
        in_specs.append(cur(qw))
    if merge is not None:
        args.extend(merge)
        in_specs.extend([cur(qw), cur(LANES)])
    out_shape = [jax.ShapeDtypeStruct((n, seq, qw), jnp.bfloat16)]
    out_specs = [cur(qw)]
    if want_stats:
        out_shape.append(jax.ShapeDtypeStruct((n, seq, LANES), jnp.bfloat16))
        out_specs.append(cur(LANES))
    kernel = functools.partial(_attn_kernel, pairs=pairs, slabs=slabs, group=group, nblk=sub,
                               strided=False, whole=tq == seq, has_sinks=sinks is not None,
                               has_gate=gate is not None, has_merge=merge is not None,
                               want_stats=want_stats)
    return pl.pallas_call(
        kernel,
        grid=(n // group, seq // tq),
        in_specs=in_specs,
        out_specs=out_specs,
        out_shape=out_shape,
        scratch_shapes=_attn_scratch(pairs, slabs, group * (sub + 1), want_stats),
        compiler_params=pltpu.CompilerParams(
            dimension_semantics=("parallel", "arbitrary"), vmem_limit_bytes=VMEM_LIMIT),
        name=name,
    )(*args)


def _attn_mid_call(q, k, v, max_dist, *, name):
    b, classes, per_class, w = q.shape
    outer = classes // MID_DIL
    nblk = per_class // MID_RUN
    group = max(1, min(MID_DIL, ATTN_BLOCKS // nblk))
    view = lambda t: t.reshape(b, outer, MID_DIL, nblk, MID_RUN, t.shape[-1])
    spec = lambda width: pl.BlockSpec((None, outer, group, nblk, MID_RUN, width),
                                      lambda i, e: (i, 0, e, 0, 0, 0))
    pairs = w // LANES
    bias = _band_bias(max_dist, mid=True)
    shape6 = (b, outer, MID_DIL, nblk, MID_RUN)
    kernel = functools.partial(_attn_kernel, pairs=pairs, slabs=pairs, group=group, nblk=nblk,
                               strided=True, whole=True, has_sinks=False, has_gate=False,
                               has_merge=False, want_stats=True)
    o, st = pl.pallas_call(
        kernel,
        grid=(b, MID_DIL // group),
        in_specs=[spec(w), spec(w), spec(w), pl.BlockSpec(bias.shape, lambda i, e: (0, 0, 0))],
        out_specs=[spec(w), spec(LANES)],
        out_shape=[jax.ShapeDtypeStruct(shape6 + (w,), jnp.bfloat16),
                   jax.ShapeDtypeStruct(shape6 + (LANES,), jnp.bfloat16)],
        scratch_shapes=_attn_scratch(pairs, pairs, group * (nblk + 1), True),
        compiler_params=pltpu.CompilerParams(
            dimension_semantics=("parallel", "arbitrary"), vmem_limit_bytes=VMEM_LIMIT),
        name=name,
    )(view(q), view(k), view(v), bias)
    return o.reshape(b, classes, per_class, w), st.reshape(b, classes, per_class, LANES)


def _expand_heads(w, expand):
    hi, mid, lo = _split3(w)
    packed = hi + pltpu.roll(mid, B_HEADS, 1) + pltpu.roll(lo, 2 * B_HEADS, 1)
    return _dot(packed.astype(jnp.bfloat16), expand)


def _stat_sum(st):
    return (st + pltpu.roll(st, LANES - B_HEADS, 1) + pltpu.roll(st, LANES - 2 * B_HEADS, 1))


def _out_kernel(x_ref, ma_ref, o1_ref, s1_ref, oc_ref, sc_ref, gb_ref,
                permt_ref, expand_ref, w_ref, out_ref, ubuf, mbuf):
    permt = permt_ref[...]
    expand = expand_ref[...]
    head_lane = lax.broadcasted_iota(jnp.int32, (1, LANES), 1) < B_HEADS
    nloc = x_ref.shape[0] // LOCAL

    def to_token_order(j):
        sel = slice(j * BF16_ROWS, (j + 1) * BF16_ROWS)
        y = jnp.concatenate([jnp.concatenate([oc_ref[c, sel, :] for c in range(CLASSES)], axis=0),
                             jnp.concatenate([sc_ref[c, sel, :] for c in range(CLASSES)], axis=0)],
                            axis=1)
        ubuf[j % 2] = _dot(permt, y)

    def merge_gate(j):
        rows = slice(j * LOCAL, (j + 1) * LOCAL)
        o1 = o1_ref[rows, :].astype(jnp.float32)
        l1 = _stat_sum(s1_ref[rows, :].astype(jnp.float32))
        oc = ubuf[j % 2, :, 0:B_WIDTH]
        lc = _stat_sum(ubuf[j % 2, :, B_WIDTH:])
        top = jnp.maximum(l1, lc)
        e1, ec = jnp.exp2(l1 - top), jnp.exp2(lc - top)
        w1 = jnp.where(head_lane, e1 / (e1 + ec), 0.0)
        ob = oc + _expand_heads(w1, expand) * (o1 - oc)
        gb = gb_ref[rows, :]
        mbuf[j % 2] = (ob * (gb * jax.nn.sigmoid(gb))).astype(mbuf.dtype)

    def project(j):
        rows = slice(j * LOCAL, (j + 1) * LOCAL)
        out_ref[rows, :] = (x_ref[rows, :] + _dot(ma_ref[rows, :], w_ref[0:A_WIDTH, :])
                            + _dot(mbuf[j % 2], w_ref[A_WIDTH:, :]))

    for step in range(nloc + 2):
        if step - 2 >= 0:
            project(step - 2)
        if 0 <= step - 1 < nloc:
            merge_gate(step - 1)
        if step < nloc:
            to_token_order(step)


def _attn_out_kernel(*refs, n_attn_in, attn_kwargs):
    attn_in = refs[:n_attn_in]
    x_ref, o1_ref, s1_ref, oc_ref, sc_ref, gb_ref, permt_ref, expand_ref, w_ref, out_ref = refs[n_attn_in:n_attn_in + 10]
    scratch = refs[n_attn_in + 10:]
    attn_scratch, (ma_buf, ubuf, mbuf) = scratch[:-3], scratch[-3:]
    _attn_kernel(*attn_in, ma_buf, *attn_scratch, **attn_kwargs)
    _out_kernel(x_ref, ma_buf.at[0], o1_ref, s1_ref, oc_ref, sc_ref, gb_ref,
                permt_ref, expand_ref, w_ref, out_ref, ubuf, mbuf)


def _attn_out_call(x2d, q, k, v, max_dist, sinks, gate, o1, s1, oc, sc, gate_b, permt, expand, w_bf):
    tokens, d_model = x2d.shape
    n, seq, qw = q.shape
    kvw = k.shape[-1]
    pairs, slabs = qw // LANES, kvw // LANES
    tq = OUT_ROWS
    sub = tq // BLOCK
    steps = seq // tq
    cur = lambda w: pl.BlockSpec((1, tq, w), lambda b, i: (b, i, 0))
    prev = lambda w: pl.BlockSpec((1, BLOCK, w), lambda b, i: (b, jnp.maximum(i * sub - 1, 0), 0))
    row = lambda w: pl.BlockSpec((tq, w), lambda b, i: (b * steps + i, 0))
    full = lambda a: pl.BlockSpec(a.shape, lambda b, i: (0,) * a.ndim)
    cmaj = lambda w: pl.BlockSpec((None, CLASSES, tq // CLASSES, w), lambda b, i: (b, 0, i, 0))
    bias = _band_bias(max_dist, mid=False)
    attn_args = [q, k, k, v, v, bias, sinks, gate]
    attn_specs = [cur(qw), cur(kvw), prev(kvw), cur(kvw), prev(kvw), full(bias),
                  pl.BlockSpec(memory_space=pltpu.SMEM), cur(qw)]
    attn_kwargs = dict(pairs=pairs, slabs=slabs, group=1, nblk=sub, strided=False, whole=tq == seq,
                       has_sinks=True, has_gate=True, has_merge=False, want_stats=False)
    kernel = functools.partial(_attn_out_kernel, n_attn_in=len(attn_args), attn_kwargs=attn_kwargs)
    return pl.pallas_call(
        kernel,
        grid=(n, steps),
        in_specs=attn_specs + [row(d_model), row(B_WIDTH), row(LANES), cmaj(B_WIDTH), cmaj(LANES),
                               row(B_WIDTH), full(permt), full(expand), full(w_bf)],
        out_specs=row(d_model),
        out_shape=jax.ShapeDtypeStruct((tokens, d_model), jnp.float32),
        scratch_shapes=_attn_scratch(pairs, slabs, sub + 1, False) + [
            pltpu.VMEM((1, tq, qw), jnp.bfloat16),
            pltpu.VMEM((2, LOCAL, B_WIDTH + LANES), jnp.float32),
            pltpu.VMEM((2, LOCAL, B_WIDTH), jnp.bfloat16)],
        compiler_params=pltpu.CompilerParams(
            dimension_semantics=("parallel", "arbitrary"), vmem_limit_bytes=VMEM_LIMIT),
        name="attn_a_out",
    )(*attn_args, x2d, o1, s1, oc, sc, gate_b, permt, expand, w_bf)


def _rope_tables(seq):
    inv = ROPE_THETA ** (-jnp.arange(HALF, dtype=jnp.float32) / HALF)
    ang = jnp.arange(seq).astype(jnp.float32)[:, None] * inv[None, :]
    reps = LANES // HALF
    cos_t = jnp.tile(jnp.cos(ang), (1, reps))
    sign = jnp.tile(jnp.concatenate([-jnp.ones((HALF,), jnp.float32), jnp.ones((HALF,), jnp.float32)]),
                    LANES // HEAD_DIM)
    sin_t = jnp.tile(jnp.sin(ang), (1, reps)) * sign[None, :]
    return cos_t, sin_t


def kernel(x, norm_gain, w_in, q_norm_a, k_norm_a, sinks_a, q_norm_b, k_norm_b, w_out):
    b, seq, d_model = x.shape
    tokens = b * seq
    depth = norm_gain.shape[0]
    assert [d for _, d in B_PATTERNS] == [1, MID_DIL, CLASSES]
    assert seq % PROJ_ROWS == 0 and PROJ_ROWS % LOCAL == 0 and (seq // CLASSES) % BLOCK == 0
    assert seq % OUT_ROWS == 0 and OUT_ROWS % LOCAL == 0
    scale = HEAD_DIM ** -0.5 * LOG2E
    cos_t, sin_t = _rope_tables(seq)
    seg_i = np.arange(SEG_TILE) // HEAD_DIM
    seg = jnp.asarray(seg_i[:, None] == seg_i[None, :], jnp.bfloat16)
    r = np.arange(LOCAL)
    perm_np = (CLASSES * (r % BF16_ROWS) + r // BF16_ROWS)[:, None] == r[None, :]
    perm = jnp.asarray(perm_np, jnp.bfloat16)
    permt = jnp.asarray(perm_np.T, jnp.bfloat16)
    term_head = np.where(np.arange(LANES) < STAT_LANES, np.arange(LANES) % B_HEADS, -1)
    expand = jnp.asarray(term_head[:, None] == (np.arange(B_WIDTH) // HEAD_DIM)[None, :], jnp.bfloat16)

    for i in range(depth):
        x2d = x.reshape(tokens, d_model)
        tile = lambda g, n, s=1.0: jnp.tile(g * s, n)[None, :]
        qa, ka, va, ga, qb, kb, vb, gb, qc, kc, vc = _proj_call(
            x2d, norm_gain[i][None, :], w_in[i].astype(jnp.bfloat16), seg, perm, cos_t, sin_t,
            tile(q_norm_a[i], A_HEADS, scale), tile(k_norm_a[i], A_KV_HEADS),
            tile(q_norm_b[i], B_HEADS, scale), tile(k_norm_b[i], B_HEADS), b, seq)

        r3 = lambda t: t.reshape(b, seq, t.shape[-1])
        (w1, d1), (w4, d4), (w16, d16) = B_PATTERNS
        o1, s1 = _attn_call(r3(qb), r3(kb), r3(vb), w1 // d1, want_stats=True, blocks=seq // BLOCK,
                            name="attn_b1")
        o4, s4 = _attn_mid_call(qc, kc, vc, w4 // d4, name="attn_b4")
        fold = lambda t: t.reshape(b * CLASSES, seq // CLASSES, t.shape[-1])
        oc, sc = _attn_call(fold(qc), fold(kc), fold(vc), w16 // d16, merge=(fold(o4), fold(s4)),
                            want_stats=True, name="attn_b16")
        unfold = lambda t: t.reshape(b, CLASSES, seq // CLASSES, t.shape[-1])

        out = _attn_out_call(x2d, r3(qa), r3(ka), r3(va), A_WINDOW - 1, sinks_a[i], r3(ga),
                             o1.reshape(tokens, B_WIDTH), s1.reshape(tokens, LANES),
                             unfold(oc), unfold(sc), gb, permt, expand, w_out[i].astype(jnp.bfloat16))
        x = out.reshape(b, seq, d_model)
    return x
```

```python
import functools

import numpy as np
import jax
import jax.numpy as jnp
from jax import lax
from jax.experimental import pallas as pl
from jax.experimental.pallas import tpu as pltpu

HEAD_DIM = 64
HALF = HEAD_DIM // 2
A_HEADS = 8
A_KV_HEADS = 2
A_WINDOW = 128
B_HEADS = 8
B_PATTERNS = ((128, 1), (512, 4), (2048, 16))
BLOCK = 128
ROPE_THETA = 10000.0
EPS = 1e-6
NEG = -1e30
LOG2E = 1.4426950408889634

A_WIDTH = A_HEADS * HEAD_DIM
A_KV_WIDTH = A_KV_HEADS * HEAD_DIM
B_WIDTH = B_HEADS * HEAD_DIM

LANES = 128
BF16_ROWS = 16
SEG_TILE = 256
PROJ_ROWS = 1024
OUT_ROWS = 1024
VMEM_BYTES = 64 * 1024 * 1024
VMEM_LIMIT = VMEM_BYTES * 7 // 8

A_KV_PAIR_WIDTH = 2 * A_KV_HEADS * HEAD_DIM
assert A_KV_WIDTH == LANES

CLASSES = max(d for _, d in B_PATTERNS)
LOCAL = CLASSES * BF16_ROWS
MID_DIL = 4
MID_RUN = BLOCK // MID_DIL
STAT_LANES = 3 * B_HEADS


def _dot(a, b):
    return jnp.dot(a, b, preferred_element_type=jnp.float32)


def _split3(v):
    hi = v.astype(jnp.bfloat16).astype(jnp.float32)
    r1 = v - hi
    mid = r1.astype(jnp.bfloat16).astype(jnp.float32)
    lo = (r1 - mid).astype(jnp.bfloat16).astype(jnp.float32)
    return hi, mid, lo


def _head_norm_rope(p, seg, gain, cos, sin, hi_mask):
    width = p.shape[-1]
    sq = (p * p).astype(jnp.bfloat16)
    step = min(width, SEG_TILE)
    sums = []
    for c in range(0, width, step):
        sums.append(_dot(sq[:, c:c + step], seg[:step, :step]))
    ss = sums[0] if len(sums) == 1 else jnp.concatenate(sums, axis=-1)
    y = p * lax.rsqrt(ss * (1.0 / HEAD_DIM) + EPS) * gain
    outs = []
    for c in range(0, width, LANES):
        yc = y[:, c:c + LANES]
        partner = jnp.where(hi_mask, pltpu.roll(yc, HALF, 1), pltpu.roll(yc, LANES - HALF, 1))
        outs.append(yc * cos + partner * sin)
    return outs[0] if len(outs) == 1 else jnp.concatenate(outs, axis=-1)


def _proj_kernel(x_ref, gain_ref, w_ref, seg_ref, perm_ref, cos_ref, sin_ref,
                 gqa_ref, gka_ref, gqb_ref, gkb_ref,
                 qa_ref, ka_ref, va_ref, ga_ref, qb_ref, kb_ref, vb_ref, gb_ref,
                 qc_ref, kc_ref, vc_ref, pbuf):
    xf = x_ref[...]
    ms = jnp.mean(xf * xf, axis=-1, keepdims=True)
    h = (xf * lax.rsqrt(ms + EPS) * gain_ref[...]).astype(jnp.bfloat16)
    seg = seg_ref[...]
    perm = perm_ref[...]
    cos = cos_ref[...]
    sin = sin_ref[...]
    lane = lax.broadcasted_iota(jnp.int32, (1, LANES), 1)
    hi_mask = (lane & HALF) != 0
    low = lane < HEAD_DIM
    bf = jnp.bfloat16

    def per_pair(y):
        swapped = pltpu.roll(y, HEAD_DIM, 1)
        return jnp.concatenate([jnp.where(low, y, swapped), jnp.where(low, swapped, y)], axis=-1)

    def normed(gain_ref_):
        return lambda p: _head_norm_rope(p, seg, gain_ref_[...], cos, sin, hi_mask)

    plain = lambda p: p
    groups = [
        [(A_WIDTH, normed(gqa_ref), qa_ref, None)],
        [(A_KV_WIDTH, lambda p: per_pair(normed(gka_ref)(p)), ka_ref, None),
         (A_KV_WIDTH, per_pair, va_ref, None)],
        [(A_WIDTH, plain, ga_ref, None)],
        [(B_WIDTH, normed(gqb_ref), qb_ref, qc_ref)],
        [(B_WIDTH, normed(gkb_ref), kb_ref, kc_ref)],
        [(B_WIDTH, plain, vb_ref, vc_ref)],
        [(B_WIDTH, plain, gb_ref, None)],
    ]
    widths = [sum(part[0] for part in g) for g in groups]
    starts = np.cumsum([0] + widths)

    def project(i):
        pbuf[i % 2, :, 0:widths[i]] = _dot(h, w_ref[:, starts[i]:starts[i] + widths[i]])

    def finish(i):
        lo = 0
        for width, epilogue, nat_ref, _ in groups[i]:
            nat_ref[...] = epilogue(pbuf[i % 2, :, lo:lo + width]).astype(nat_ref.dtype)
            lo += width

    def to_class_major(i):
        _, _, nat_ref, cm_ref = groups[i][0]
        if cm_ref is None:
            return
        for j in range(nat_ref.shape[0] // LOCAL):
            z = _dot(perm, nat_ref[j * LOCAL:(j + 1) * LOCAL, :]).astype(bf)
            for c in range(CLASSES):
                cm_ref[c, j * BF16_ROWS:(j + 1) * BF16_ROWS, :] = z[c * BF16_ROWS:(c + 1) * BF16_ROWS, :]

    n = len(groups)
    for step in range(n + 2):
        if step < n:
            project(step)
        if 0 <= step - 1 < n:
            finish(step - 1)
        if 0 <= step - 2 < n:
            to_class_major(step - 2)


def _proj_call(x2d, gain, w_bf, seg, perm, cos_t, sin_t, gqa, gka, gqb, gkb, batch, seq):
    tokens, d_model = x2d.shape
    tm = PROJ_ROWS
    steps_per_seq = seq // tm
    row = lambda w: pl.BlockSpec((tm, w), lambda g: (g, 0))
    full = lambda a: pl.BlockSpec(a.shape, lambda g: (0,) * a.ndim)
    table = pl.BlockSpec((tm, LANES), lambda g: (g % steps_per_seq, 0))
    cmaj = pl.BlockSpec((None, CLASSES, tm // CLASSES, B_WIDTH),
                        lambda g: (g // steps_per_seq, 0, g % steps_per_seq, 0))
    bf = jnp.bfloat16
    cm_shape = jax.ShapeDtypeStruct((batch, CLASSES, seq // CLASSES, B_WIDTH), bf)
    out_shape = [
        jax.ShapeDtypeStruct((tokens, A_WIDTH), bf),
        jax.ShapeDtypeStruct((tokens, A_KV_PAIR_WIDTH), bf),
        jax.ShapeDtypeStruct((tokens, A_KV_PAIR_WIDTH), bf),
        jax.ShapeDtypeStruct((tokens, A_WIDTH), jnp.float32),
        jax.ShapeDtypeStruct((tokens, B_WIDTH), bf),
        jax.ShapeDtypeStruct((tokens, B_WIDTH), bf),
        jax.ShapeDtypeStruct((tokens, B_WIDTH), bf),
        jax.ShapeDtypeStruct((tokens, B_WIDTH), jnp.float32),
        cm_shape, cm_shape, cm_shape,
    ]
    return pl.pallas_call(
        _proj_kernel,
        grid=(tokens // tm,),
        in_specs=[row(d_model), full(gain), full(w_bf), full(seg), full(perm), table, table,
                  full(gqa), full(gka), full(gqb), full(gkb)],
        out_specs=[row(s.shape[1]) for s in out_shape[:8]] + [cmaj] * 3,
        out_shape=out_shape,
        scratch_shapes=[pltpu.VMEM((2, tm, max(A_WIDTH, B_WIDTH)), jnp.float32)],
        compiler_params=pltpu.CompilerParams(
            dimension_semantics=("parallel",), vmem_limit_bytes=VMEM_LIMIT),
        name="proj",
    )(x2d, gain, w_bf, seg, perm, cos_t, sin_t, gqa, gka, gqb, gkb)


ONES_ROWS = BF16_ROWS
ATTN_BLOCKS = 16


def _attn_kernel(*refs, pairs, slabs, group, nblk, strided, whole, has_sinks, has_gate, has_merge,
                 want_stats):
    it = iter(refs)
    if strided:
        q_ref, kc_ref, vc_ref, bias_ref = (next(it) for _ in range(4))
        kp_ref = vp_ref = None
    else:
        q_ref, kc_ref, kp_ref, vc_ref, vp_ref, bias_ref = (next(it) for _ in range(6))
    sink_ref = next(it) if has_sinks else None
    gate_ref = next(it) if has_gate else None
    oin_ref, sin_ref = (next(it), next(it)) if has_merge else (None, None)
    o_ref = next(it)
    st_ref = next(it) if want_stats else None
    vtbuf, sbuf, pbuf, mbuf = (next(it) for _ in range(4))
    lsebuf = next(it) if want_stats else None

    def block_of(ref, g, m, cols):
        if strided:
            return ref[:, g, m, :, cols].reshape(BLOCK, LANES)
        return ref[g, m * BLOCK:(m + 1) * BLOCK, cols]

    def key_block(cur_ref, prev_ref, g, kb, cols):
        if kb > 0:
            return block_of(cur_ref, g, kb - 1, cols)
        return block_of(cur_ref, g, 0, cols) if strided else prev_ref[g, :, cols]

    def store_block(ref, g, m, cols, val):
        if strided:
            ref[:, g, m, :, cols] = val.reshape(BLOCK // MID_RUN, MID_RUN, LANES)
        else:
            ref[g, m * BLOCK:(m + 1) * BLOCK, cols] = val

    if strided:
        first_bias = bias_ref[1]
    else:
        first_bias = bias_ref[jnp.where(pl.program_id(1) == 0, 1, 0)]

    ones = jnp.ones((ONES_ROWS, BLOCK), vtbuf.dtype)

    def transpose_values(g, kb):
        for sl in range(slabs):
            src = slice(sl * LANES, (sl + 1) * LANES)
            vtbuf[sl, g * (nblk + 1) + kb, 0:LANES, :] = key_block(vc_ref, vp_ref, g, kb, src).T
            vtbuf[sl, g * (nblk + 1) + kb, LANES:, :] = ones

    zeros = jnp.zeros((HEAD_DIM, BLOCK), q_ref.dtype)
    onehot = jnp.concatenate([bias_ref[2, 0:BLOCK], bias_ref[2, 0:BLOCK]], axis=1)

    def sink_row(hp):
        return jnp.concatenate([jnp.full((1, LANES), sink_ref[2 * hp] * LOG2E, jnp.float32),
                                jnp.full((1, LANES), sink_ref[2 * hp + 1] * LOG2E, jnp.float32)], axis=1)

    def window(j):
        return slice(BLOCK, 2 * BLOCK) if (whole and j == 0) else slice(0, 2 * BLOCK)

    def score_stage(t):
        slot, (g, j) = t % 2, divmod(t, nblk)
        win = window(j)
        bias_t = (first_bias if j == 0 else bias_ref[0])[win]
        for hp in range(pairs):
            sl = hp * slabs // pairs
            kcols = slice(sl * LANES, (sl + 1) * LANES)
            qt = block_of(q_ref, g, j, slice(hp * LANES, (hp + 1) * LANES)).T
            qm = jnp.concatenate([jnp.concatenate([qt[0:HEAD_DIM], zeros], axis=0),
                                  jnp.concatenate([zeros, qt[HEAD_DIM:]], axis=0)], axis=1)
            kw = key_block(kc_ref, kp_ref, g, j + 1, kcols)
            if win.start == 0:
                kw = jnp.concatenate([key_block(kc_ref, kp_ref, g, j, kcols), kw], axis=0)
            st = _dot(jnp.concatenate([kw, bias_t], axis=1),
                      jnp.concatenate([qm, onehot], axis=0))
            m = jnp.max(st, axis=0, keepdims=True)
            if has_sinks:
                m = jnp.maximum(m, sink_row(hp))
            sbuf[slot, hp, win] = st
            mbuf[slot, hp] = m

    def softmax_stage(t):
        slot, (g, j) = t % 2, divmod(t, nblk)
        win = window(j)
        if j == 0 and win.start == 0:
            transpose_values(g, 0)
        transpose_values(g, j + 1)
        for hp in range(pairs):
            pbuf[slot, hp, win] = jnp.exp2(sbuf[slot, hp, win] - mbuf[slot, hp]).astype(pbuf.dtype)

    def value_stage(t):
        slot, (g, j) = t % 2, divmod(t, nblk)
        win = window(j)
        kb0 = g * (nblk + 1) + j
        if has_merge:
            terms = block_of(sin_ref, g, j, slice(0, LANES)).T.astype(jnp.float32)
            lse_in = (terms[0:B_HEADS] + terms[B_HEADS:2 * B_HEADS] + terms[2 * B_HEADS:3 * B_HEADS])
        for hp in range(pairs):
            sl = hp * slabs // pairs
            cols = slice(hp * LANES, (hp + 1) * LANES)
            m = mbuf[slot, hp]
            vt = vtbuf[sl, kb0 + 1]
            if win.start == 0:
                vt = jnp.concatenate([vtbuf[sl, kb0], vt], axis=1)
            r = _dot(vt, pbuf[slot, hp, win])
            l = r[LANES:LANES + 1, :]
            if has_sinks:
                l = l + jnp.exp2(sink_row(hp) - m)
            inv = 1.0 / l
            lse = m + jnp.log2(l)
            if has_merge:
                other = jnp.concatenate([lse_in[2 * hp:2 * hp + 1], lse_in[2 * hp + 1:2 * hp + 2]], axis=1)
                top = jnp.maximum(lse, other)
                mine, theirs = jnp.exp2(lse - top), jnp.exp2(other - top)
                total = mine + theirs
                inv, frac = inv * (mine / total), theirs / total
                lse = top + jnp.log2(total)
            ot = jnp.concatenate([r[0:HEAD_DIM, 0:LANES] * inv[:, 0:LANES],
                                  r[HEAD_DIM:LANES, LANES:] * inv[:, LANES:]], axis=0)
            if has_merge:
                oin_t = block_of(oin_ref, g, j, cols).T.astype(jnp.float32)
                ot = ot + oin_t * jnp.concatenate(
                    [jnp.broadcast_to(frac[:, 0:LANES], (HEAD_DIM, LANES)),
                     jnp.broadcast_to(frac[:, LANES:], (HEAD_DIM, LANES))], axis=0)
            if has_gate:
                gt = block_of(gate_ref, g, j, cols)
                o = (ot.T * (gt * jax.nn.sigmoid(gt))).astype(o_ref.dtype)
            else:
                o = ot.astype(o_ref.dtype).T
            store_block(o_ref, g, j, cols, o)
            if want_stats:
                lsebuf[slot, 2 * hp:2 * hp + 1, :] = lse[:, 0:LANES]
                lsebuf[slot, 2 * hp + 1:2 * hp + 2, :] = lse[:, LANES:]
        if want_stats:
            terms = _split3(lsebuf[slot])
            pad = jnp.zeros((LANES - STAT_LANES, BLOCK), jnp.float32)
            tile = jnp.concatenate(list(terms) + [pad], axis=0)
            store_block(st_ref, g, j, slice(0, LANES), tile.astype(st_ref.dtype).T)

    jobs = group * nblk
    for step in range(jobs + 2):
        if step - 2 >= 0:
            value_stage(step - 2)
        if 0 <= step - 1 < jobs:
            softmax_stage(step - 1)
        if step < jobs:
            score_stage(step)


def _band_bias(max_dist, mid):
    r = np.arange(BLOCK)
    pos = MID_DIL * (r % MID_RUN) + r // MID_RUN if mid else r
    qpos = pos[None, :] + BLOCK
    kpos = np.concatenate([pos, pos + BLOCK])[:, None]
    dist = qpos - kpos
    valid = (dist >= 0) & (dist <= max_dist)
    first = valid & (np.arange(2 * BLOCK)[:, None] >= BLOCK)
    masks = np.where(np.stack([valid, first]), 0.0, NEG)
    onehot = np.tile(np.eye(BLOCK), (2, 1))
    return jnp.asarray(np.concatenate([masks, onehot[None]]), jnp.bfloat16)


def _attn_scratch(pairs, slabs, key_blocks, want_stats):
    scratch = [pltpu.VMEM((slabs, key_blocks, LANES + ONES_ROWS, BLOCK), jnp.bfloat16),
               pltpu.VMEM((2, pairs, 2 * BLOCK, 2 * LANES), jnp.float32),
               pltpu.VMEM((2, pairs, 2 * BLOCK, 2 * LANES), jnp.bfloat16),
               pltpu.VMEM((2, pairs, 1, 2 * LANES), jnp.float32)]
    if want_stats:
        scratch.append(pltpu.VMEM((2, 2 * pairs, BLOCK), jnp.float32))
    return scratch


def _attn_call(q, k, v, max_dist, *, sinks=None, gate=None, merge=None, want_stats=False,
               blocks=ATTN_BLOCKS, name):
    n, seq, qw = q.shape
    kvw = k.shape[-1]
    pairs, slabs = qw // LANES, kvw // LANES
    tq = min(seq, blocks * BLOCK)
    sub = tq // BLOCK
    group = blocks // sub
    cur = lambda w: pl.BlockSpec((group, tq, w), lambda b, i: (b, i, 0))
    prev = lambda w: pl.BlockSpec((group, BLOCK, w), lambda b, i: (b, jnp.maximum(i * sub - 1, 0), 0))
    bias = _band_bias(max_dist, mid=False)
    args = [q, k, k, v, v, bias]
    in_specs = [cur(qw), cur(kvw), prev(kvw), cur(kvw), prev(kvw),
                pl.BlockSpec(bias.shape, lambda b, i: (0, 0, 0))]
    if sinks is not None:
        args.append(sinks)
        in_specs.append(pl.BlockSpec(memory_space=pltpu.SMEM))
    if gate is not None:
        args.append(gate)
        in_specs.append(cur(qw))
    if merge is not None:
        args.extend(merge)
        in_specs.extend([cur(qw), cur(LANES)])
    out_shape = [jax.ShapeDtypeStruct((n, seq, qw), jnp.bfloat16)]
    out_specs = [cur(qw)]
    if want_stats:
        out_shape.append(jax.ShapeDtypeStruct((n, seq, LANES), jnp.bfloat16))
        out_specs.append(cur(LANES))
    kernel = functools.partial(_attn_kernel, pairs=pairs, slabs=slabs, group=group, nblk=sub,
                               strided=False, whole=tq == seq, has_sinks=sinks is not None,
                               has_gate=gate is not None, has_merge=merge is not None,
                               want_stats=want_stats)
    return pl.pallas_call(
        kernel,
        grid=(n // group, seq // tq),
        in_specs=in_specs,
        out_specs=out_specs,
        out_shape=out_shape,
        scratch_shapes=_attn_scratch(pairs, slabs, group * (sub + 1), want_stats),
        compiler_params=pltpu.CompilerParams(
            dimension_semantics=("parallel", "arbitrary"), vmem_limit_bytes=VMEM_LIMIT),
        name=name,
    )(*args)


def _attn_mid_call(q, k, v, max_dist, *, name):
    b, classes, per_class, w = q.shape
    outer = classes // MID_DIL
    nblk = per_class // MID_RUN
    group = max(1, min(MID_DIL, ATTN_BLOCKS // nblk))
    view = lambda t: t.reshape(b, outer, MID_DIL, nblk, MID_RUN, t.shape[-1])
    spec = lambda width: pl.BlockSpec((None, outer, group, nblk, MID_RUN, width),
                                      lambda i, e: (i, 0, e, 0, 0, 0))
    pairs = w // LANES
    bias = _band_bias(max_dist, mid=True)
    shape6 = (b, outer, MID_DIL, nblk, MID_RUN)
    kernel = functools.partial(_attn_kernel, pairs=pairs, slabs=pairs, group=group, nblk=nblk,
                               strided=True, whole=True, has_sinks=False, has_gate=False,
                               has_merge=False, want_stats=True)
    o, st = pl.pallas_call(
        kernel,
        grid=(b, MID_DIL // group),
        in_specs=[spec(w), spec(w), spec(w), pl.BlockSpec(bias.shape, lambda i, e: (0, 0, 0))],
        out_specs=[spec(w), spec(LANES)],
        out_shape=[jax.ShapeDtypeStruct(shape6 + (w,), jnp.bfloat16),
                   jax.ShapeDtypeStruct(shape6 + (LANES,), jnp.bfloat16)],
        scratch_shapes=_attn_scratch(pairs, pairs, group * (nblk + 1), True),
        compiler_params=pltpu.CompilerParams(
            dimension_semantics=("parallel", "arbitrary"), vmem_limit_bytes=VMEM_LIMIT),
        name=name,
    )(view(q), view(k), view(v), bias)
    return o.reshape(b, classes, per_class, w), st.reshape(b, classes, per_class, LANES)


def _expand_heads(w, expand):
    hi, mid, lo = _split3(w)
    packed = hi + pltpu.roll(mid, B_HEADS, 1) + pltpu.roll(lo, 2 * B_HEADS, 1)
    return _dot(packed.astype(jnp.bfloat16), expand)


def _stat_sum(st):
    return (st + pltpu.roll(st, LANES - B_HEADS, 1) + pltpu.roll(st, LANES - 2 * B_HEADS, 1))


def _out_kernel(x_ref, ma_ref, o1_ref, s1_ref, oc_ref, sc_ref, gb_ref,
                permt_ref, expand_ref, w_ref, out_ref, ubuf, mbuf):
    permt = permt_ref[...]
    expand = expand_ref[...]
    head_lane = lax.broadcasted_iota(jnp.int32, (1, LANES), 1) < B_HEADS
    nloc = x_ref.shape[0] // LOCAL

    def to_token_order(j):
        sel = slice(j * BF16_ROWS, (j + 1) * BF16_ROWS)
        y = jnp.concatenate([jnp.concatenate([oc_ref[c, sel, :] for c in range(CLASSES)], axis=0),
                             jnp.concatenate([sc_ref[c, sel, :] for c in range(CLASSES)], axis=0)],
                            axis=1)
        ubuf[j % 2] = _dot(permt, y)

    def merge_gate(j):
        rows = slice(j * LOCAL, (j + 1) * LOCAL)
        o1 = o1_ref[rows, :].astype(jnp.float32)
        l1 = _stat_sum(s1_ref[rows, :].astype(jnp.float32))
        oc = ubuf[j % 2, :, 0:B_WIDTH]
        lc = _stat_sum(ubuf[j % 2, :, B_WIDTH:])
        top = jnp.maximum(l1, lc)
        e1, ec = jnp.exp2(l1 - top), jnp.exp2(lc - top)
        w1 = jnp.where(head_lane, e1 / (e1 + ec), 0.0)
        ob = oc + _expand_heads(w1, expand) * (o1 - oc)
        gb = gb_ref[rows, :]
        mbuf[j % 2] = (ob * (gb * jax.nn.sigmoid(gb))).astype(mbuf.dtype)

    def project(j):
        rows = slice(j * LOCAL, (j + 1) * LOCAL)
        out_ref[rows, :] = (x_ref[rows, :] + _dot(ma_ref[rows, :], w_ref[0:A_WIDTH, :])
                            + _dot(mbuf[j % 2], w_ref[A_WIDTH:, :]))

    for step in range(nloc + 2):
        if step - 2 >= 0:
            project(step - 2)
        if 0 <= step - 1 < nloc:
            merge_gate(step - 1)
        if step < nloc:
            to_token_order(step)


def _attn_out_kernel(*refs, n_attn_in, attn_kwargs):
    attn_in = refs[:n_attn_in]
    x_ref, o1_ref, s1_ref, oc_ref, sc_ref, gb_ref, permt_ref, expand_ref, w_ref, out_ref = refs[n_attn_in:n_attn_in + 10]
    scratch = refs[n_attn_in + 10:]
    attn_scratch, (ma_buf, ubuf, mbuf) = scratch[:-3], scratch[-3:]
    _attn_kernel(*attn_in, ma_buf, *attn_scratch, **attn_kwargs)
    _out_kernel(x_ref, ma_buf.at[0], o1_ref, s1_ref, oc_ref, sc_ref, gb_ref,
                permt_ref, expand_ref, w_ref, out_ref, ubuf, mbuf)


def _attn_out_call(x2d, q, k, v, max_dist, sinks, gate, o1, s1, oc, sc, gate_b, permt, expand, w_bf):
    tokens, d_model = x2d.shape
    n, seq, qw = q.shape
    kvw = k.shape[-1]
    pairs, slabs = qw // LANES, kvw // LANES
    tq = OUT_ROWS
    sub = tq // BLOCK
    steps = seq // tq
    cur = lambda w: pl.BlockSpec((1, tq, w), lambda b, i: (b, i, 0))
    prev = lambda w: pl.BlockSpec((1, BLOCK, w), lambda b, i: (b, jnp.maximum(i * sub - 1, 0), 0))
    row = lambda w: pl.BlockSpec((tq, w), lambda b, i: (b * steps + i, 0))
    full = lambda a: pl.BlockSpec(a.shape, lambda b, i: (0,) * a.ndim)
    cmaj = lambda w: pl.BlockSpec((None, CLASSES, tq // CLASSES, w), lambda b, i: (b, 0, i, 0))
    bias = _band_bias(max_dist, mid=False)
    attn_args = [q, k, k, v, v, bias, sinks, gate]
    attn_specs = [cur(qw), cur(kvw), prev(kvw), cur(kvw), prev(kvw), full(bias),
                  pl.BlockSpec(memory_space=pltpu.SMEM), cur(qw)]
    attn_kwargs = dict(pairs=pairs, slabs=slabs, group=1, nblk=sub, strided=False, whole=tq == seq,
                       has_sinks=True, has_gate=True, has_merge=False, want_stats=False)
    kernel = functools.partial(_attn_out_kernel, n_attn_in=len(attn_args), attn_kwargs=attn_kwargs)
    return pl.pallas_call(
        kernel,
        grid=(n, steps),
        in_specs=attn_specs + [row(d_model), row(B_WIDTH), row(LANES), cmaj(B_WIDTH), cmaj(LANES),
                               row(B_WIDTH), full(permt), full(expand), full(w_bf)],
        out_specs=row(d_model),
        out_shape=jax.ShapeDtypeStruct((tokens, d_model), jnp.float32),
        scratch_shapes=_attn_scratch(pairs, slabs, sub + 1, False) + [
            pltpu.VMEM((1, tq, qw), jnp.bfloat16),
            pltpu.VMEM((2, LOCAL, B_WIDTH + LANES), jnp.float32),
            pltpu.VMEM((2, LOCAL, B_WIDTH), jnp.bfloat16)],
        compiler_params=pltpu.CompilerParams(
            dimension_semantics=("parallel", "arbitrary"), vmem_limit_bytes=VMEM_LIMIT),
        name="attn_a_out",
    )(*attn_args, x2d, o1, s1, oc, sc, gate_b, permt, expand, w_bf)


def _rope_tables(seq):
    inv = ROPE_THETA ** (-jnp.arange(HALF, dtype=jnp.float32) / HALF)
    ang = jnp.arange(seq).astype(jnp.float32)[:, None] * inv[None, :]
    reps = LANES // HALF
    cos_t = jnp.tile(jnp.cos(ang), (1, reps))
    sign = jnp.tile(jnp.concatenate([-jnp.ones((HALF,), jnp.float32), jnp.ones((HALF,), jnp.float32)]),
                    LANES // HEAD_DIM)
    sin_t = jnp.tile(jnp.sin(ang), (1, reps)) * sign[None, :]
    return cos_t, sin_t


def kernel(x, norm_gain, w_in, q_norm_a, k_norm_a, sinks_a, q_norm_b, k_norm_b, w_out):
    b, seq, d_model = x.shape
    tokens = b * seq
    depth = norm_gain.shape[0]
    assert [d for _, d in B_PATTERNS] == [1, MID_DIL, CLASSES]
    assert seq % PROJ_ROWS == 0 and PROJ_ROWS % LOCAL == 0 and (seq // CLASSES) % BLOCK == 0
    assert seq % OUT_ROWS == 0 and OUT_ROWS % LOCAL == 0
    scale = HEAD_DIM ** -0.5 * LOG2E
    cos_t, sin_t = _rope_tables(seq)
    seg_i = np.arange(SEG_TILE) // HEAD_DIM
    seg = jnp.asarray(seg_i[:, None] == seg_i[None, :], jnp.bfloat16)
    r = np.arange(LOCAL)
    perm_np = (CLASSES * (r % BF16_ROWS) + r // BF16_ROWS)[:, None] == r[None, :]
    perm = jnp.asarray(perm_np, jnp.bfloat16)
    permt = jnp.asarray(perm_np.T, jnp.bfloat16)
    term_head = np.where(np.arange(LANES) < STAT_LANES, np.arange(LANES) % B_HEADS, -1)
    expand = jnp.asarray(term_head[:, None] == (np.arange(B_WIDTH) // HEAD_DIM)[None, :], jnp.bfloat16)

    for i in range(depth):
        x2d = x.reshape(tokens, d_model)
        tile = lambda g, n, s=1.0: jnp.tile(g * s, n)[None, :]
        qa, ka, va, ga, qb, kb, vb, gb, qc, kc, vc = _proj_call(
            x2d, norm_gain[i][None, :], w_in[i].astype(jnp.bfloat16), seg, perm, cos_t, sin_t,
            tile(q_norm_a[i], A_HEADS, scale), tile(k_norm_a[i], A_KV_HEADS),
            tile(q_norm_b[i], B_HEADS, scale), tile(k_norm_b[i], B_HEADS), b, seq)

        r3 = lambda t: t.reshape(b, seq, t.shape[-1])
        (w1, d1), (w4, d4), (w16, d16) = B_PATTERNS
        o1, s1 = _attn_call(r3(qb), r3(kb), r3(vb), w1 // d1, want_stats=True, blocks=seq // BLOCK,
                            name="attn_b1")
        o4, s4 = _attn_mid_call(qc, kc, vc, w4 // d4, name="attn_b4")
        fold = lambda t: t.reshape(b * CLASSES, seq // CLASSES, t.shape[-1])
        oc, sc = _attn_call(fold(qc), fold(kc), fold(vc), w16 // d16, merge=(fold(o4), fold(s4)),
                            want_stats=True, name="attn_b16")
        unfold = lambda t: t.reshape(b, CLASSES, seq // CLASSES, t.shape[-1])

        out = _attn_out_call(x2d, r3(qa), r3(ka), r3(va), A_WINDOW - 1, sinks_a[i], r3(ga),
                             o1.reshape(tokens, B_WIDTH), s1.reshape(tokens, LANES),
                             unfold(oc), unfold(sc), gb, permt, expand, w_out[i].astype(jnp.bfloat16))
        x = out.reshape(b, seq, d_model)
    return x
```
